```python
import math
import jax, jax.numpy as jnp
from jax import lax
import numpy as np

D_MODEL = 2048
BATCH = 4
SEQ = 2048
DEPTH = 1
DEC_BATCH = 128
DEC_SEQ = 1
PAST_LEN = 2048
PAGE_SIZE = 128

HEAD_DIM = 128
N_HEADS_DIFF = 8
N_HEADS_FOX = 8
DIFF_HALF = HEAD_DIM // 2
WIDTH_DIFF = N_HEADS_DIFF * HEAD_DIM
WIDTH_FOX = N_HEADS_FOX * HEAD_DIM
D_FF = -(-8 * D_MODEL // (3 * 256)) * 256
ROPE_THETA = 10000.0
QBLOCK = 128
EPS = 1e-6
IN_WIDTHS = (WIDTH_DIFF, WIDTH_DIFF, WIDTH_DIFF, WIDTH_FOX, WIDTH_FOX, WIDTH_FOX, N_HEADS_FOX, D_MODEL, D_MODEL)
D_IN = WIDTH_DIFF * 3 + WIDTH_FOX * 3 + N_HEADS_FOX + 2 * D_MODEL

kernel_name = 'hybrid_diff_fox_decoder_step'


def _rms(x, g):
    xf = x.astype(jnp.float32)
    y = xf * lax.rsqrt(jnp.mean(xf * xf, axis=-1, keepdims=True) + EPS) * g.astype(jnp.float32)
    return y.astype(x.dtype)


def _rope(x, pos):
    dim = x.shape[-1]
    half = dim // 2
    inv = ROPE_THETA ** (-jnp.arange(half, dtype=jnp.float32) / half)
    ang = pos.astype(jnp.float32)[:, None] * inv[None, :]
    shape = (pos.shape[0],) + (1,) * (x.ndim - 3) + (dim,)
    cos = jnp.concatenate([jnp.cos(ang), jnp.cos(ang)], axis=-1).reshape(shape)
    sin = jnp.concatenate([jnp.sin(ang), jnp.sin(ang)], axis=-1).reshape(shape)
    xf = x.astype(jnp.float32)
    rot = jnp.concatenate([-xf[..., half:], xf[..., :half]], axis=-1)
    return (xf * cos + rot * sin).astype(x.dtype)


def _project(xn, pos, w_in, b_f, g_q_diff, g_k_diff, g_q_fox, g_k_fox):
    b, t = xn.shape[0], xn.shape[1]
    proj = jnp.einsum('btd,de->bte', xn, w_in)
    points = np.cumsum(np.array(IN_WIDTHS))[:-1].tolist()
    qd, kd, vd, qf, kf, vf, fl, gd, gf = jnp.split(proj, points, axis=-1)
    qd = _rope(_rms(qd.reshape(b, t, N_HEADS_DIFF, 2, DIFF_HALF), g_q_diff), pos)
    kd = _rope(_rms(kd.reshape(b, t, N_HEADS_DIFF, 2, DIFF_HALF), g_k_diff), pos)
    kd = kd.reshape(b, t, N_HEADS_DIFF, HEAD_DIM)
    vd = vd.reshape(b, t, N_HEADS_DIFF, HEAD_DIM)
    qf = _rms(qf.reshape(b, t, N_HEADS_FOX, HEAD_DIM), g_q_fox)
    kf = _rms(kf.reshape(b, t, N_HEADS_FOX, HEAD_DIM), g_k_fox)
    vf = vf.reshape(b, t, N_HEADS_FOX, HEAD_DIM)
    logf = jax.nn.log_sigmoid(fl.astype(jnp.float32) + b_f.astype(jnp.float32))
    return qd, kd, vd, qf, kf, vf, logf, gd, gf


def _diff_attend(q, k, v, q_pos, k_pos, lam):
    k = k.reshape(k.shape[:3] + (2, DIFF_HALF))
    logits = jnp.einsum('bqhcd,bkhcd->bchqk', q, k).astype(jnp.float32) * (DIFF_HALF ** -0.5)
    mask = k_pos[None, :] <= q_pos[:, None]
    p = jax.nn.softmax(jnp.where(mask, logits, -jnp.inf), axis=-1)
    a = p[:, 0] - lam * p[:, 1]
    return jnp.einsum('bhqk,bkhd->bqhd', a.astype(v.dtype), v)


def _fox_attend(q, k, v, cq, ck, q_pos, k_pos):
    logits = jnp.einsum('bqhd,bkhd->bhqk', q, k).astype(jnp.float32) * (HEAD_DIM ** -0.5)
    bias = cq.transpose(0, 2, 1)[..., :, None] - ck.transpose(0, 2, 1)[..., None, :]
    mask = k_pos[None, :] <= q_pos[:, None]
    p = jax.nn.softmax(jnp.where(mask, logits + bias, -jnp.inf), axis=-1)
    return jnp.einsum('bhqk,bkhd->bqhd', p.astype(v.dtype), v)


def _prompt_attention(qd, kd, vd, qf, kf, vf, cum, lam):
    b, s = qd.shape[0], qd.shape[1]
    n_blocks = s // QBLOCK
    k_pos = jnp.arange(s)

    def one_block(i):
        start = i * QBLOCK
        q_pos = start + jnp.arange(QBLOCK)
        qd_b = lax.dynamic_slice_in_dim(qd, start, QBLOCK, axis=1)
        qf_b = lax.dynamic_slice_in_dim(qf, start, QBLOCK, axis=1)
        cq_b = lax.dynamic_slice_in_dim(cum, start, QBLOCK, axis=1)
        return (_diff_attend(qd_b, kd, vd, q_pos, k_pos, lam),
                _fox_attend(qf_b, kf, vf, cq_b, cum, q_pos, k_pos))

    od, of = lax.map(one_block, jnp.arange(n_blocks))
    od = od.transpose(1, 0, 2, 3, 4).reshape(b, s, N_HEADS_DIFF, HEAD_DIM)
    of = of.transpose(1, 0, 2, 3, 4).reshape(b, s, N_HEADS_FOX, HEAD_DIM)
    return od, of


def _merge_ffn(x, od, of, gd, gf, g_sub, lam_init, w_branch_diff, w_branch_fox, w_o,
               g_norm_ffn, w_ffn_gate, w_ffn_up, w_ffn_down):
    b, t = x.shape[0], x.shape[1]
    od = (_rms(od, g_sub) * (1.0 - lam_init)).reshape(b, t, WIDTH_DIFF)
    of = of.reshape(b, t, WIDTH_FOX)
    merged = (jax.nn.sigmoid(gd) * jnp.einsum('bti,id->btd', od, w_branch_diff)
              + jax.nn.sigmoid(gf) * jnp.einsum('bti,id->btd', of, w_branch_fox))
    h = x + jnp.einsum('btd,de->bte', merged, w_o)
    hn = _rms(h, g_norm_ffn)
    ff = jax.nn.silu(hn @ w_ffn_gate) * (hn @ w_ffn_up)
    return h + ff @ w_ffn_down


def _gather(cache, layer, page_table):
    pages = cache[layer, page_table]
    return pages.reshape((pages.shape[0], pages.shape[1] * pages.shape[2]) + pages.shape[3:])


def setup_inputs(seed: int = 0) -> dict:
    key = jax.random.key(seed)
    ks = iter(jax.random.split(key, 40))
    f32 = jnp.float32
    n_pages = PAST_LEN // PAGE_SIZE
    n_used = DEC_BATCH * n_pages
    n_pool = n_used + n_used // 4

    def nrm(shape, scale=1.0):
        return jax.random.normal(next(ks), shape, f32) * scale

    def gain(shape):
        return 1.0 + 0.02 * jax.random.normal(next(ks), shape, f32)

    x_prompt = nrm((BATCH, SEQ, D_MODEL))
    x_sample = nrm((DEC_BATCH, DEC_SEQ, D_MODEL))
    cache_k_diff = nrm((DEPTH, n_pool, PAGE_SIZE, N_HEADS_DIFF, HEAD_DIM))
    cache_v_diff = nrm((DEPTH, n_pool, PAGE_SIZE, N_HEADS_DIFF, HEAD_DIM))
    cache_k_fox = nrm((DEPTH, n_pool, PAGE_SIZE, N_HEADS_FOX, HEAD_DIM))
    cache_v_fox = nrm((DEPTH, n_pool, PAGE_SIZE, N_HEADS_FOX, HEAD_DIM))
    cache_logf_fox = jax.nn.log_sigmoid(2.0 + nrm((DEPTH, n_pool, PAGE_SIZE, N_HEADS_FOX)))
    perm = jax.random.permutation(next(ks), n_pool)
    page_table = perm[:n_used].reshape(DEC_BATCH, n_pages).astype(jnp.int32)
    return {
        'x_prompt': x_prompt,
        'x_sample': x_sample,
        'cache_k_diff': cache_k_diff,
        'cache_v_diff': cache_v_diff,
        'cache_k_fox': cache_k_fox,
        'cache_v_fox': cache_v_fox,
        'cache_logf_fox': cache_logf_fox,
        'page_table': page_table,
        'g_norm_attn': gain((DEPTH, D_MODEL)),
        'w_in': nrm((DEPTH, D_MODEL, D_IN), D_MODEL ** -0.5),
        'b_f': 2.0 + 0.5 * nrm((DEPTH, N_HEADS_FOX)),
        'g_q_diff': gain((DEPTH, DIFF_HALF)),
        'g_k_diff': gain((DEPTH, DIFF_HALF)),
        'g_q_fox': gain((DEPTH, HEAD_DIM)),
        'g_k_fox': gain((DEPTH, HEAD_DIM)),
        'lambda_q1': nrm((DEPTH, DIFF_HALF), 0.1),
        'lambda_k1': nrm((DEPTH, DIFF_HALF), 0.1),
        'lambda_q2': nrm((DEPTH, DIFF_HALF), 0.1),
        'lambda_k2': nrm((DEPTH, DIFF_HALF), 0.1),
        'g_sub': gain((DEPTH, HEAD_DIM)),
        'w_branch_diff': nrm((DEPTH, WIDTH_DIFF, D_MODEL), WIDTH_DIFF ** -0.5),
        'w_branch_fox': nrm((DEPTH, WIDTH_FOX, D_MODEL), WIDTH_FOX ** -0.5),
        'w_o': nrm((DEPTH, D_MODEL, D_MODEL), D_MODEL ** -0.5),
        'g_norm_ffn': gain((DEPTH, D_MODEL)),
        'w_ffn_gate': nrm((DEPTH, D_MODEL, D_FF), D_MODEL ** -0.5),
        'w_ffn_up': nrm((DEPTH, D_MODEL, D_FF), D_MODEL ** -0.5),
        'w_ffn_down': nrm((DEPTH, D_FF, D_MODEL), D_FF ** -0.5),
    }


def reference(x_prompt, x_sample, cache_k_diff, cache_v_diff, cache_k_fox, cache_v_fox, cache_logf_fox,
              page_table, g_norm_attn, w_in, b_f, g_q_diff, g_k_diff, g_q_fox, g_k_fox,
              lambda_q1, lambda_k1, lambda_q2, lambda_k2, g_sub, w_branch_diff, w_branch_fox, w_o,
              g_norm_ffn, w_ffn_gate, w_ffn_up, w_ffn_down):
    past_len = page_table.shape[1] * PAGE_SIZE
    s = x_prompt.shape[1]
    t = x_sample.shape[1]
    pos_p = jnp.arange(s)
    pos_s = past_len + jnp.arange(t)
    k_pos_s = jnp.arange(past_len + t)
    xp, xs = x_prompt, x_sample
    new_p = [[], [], [], [], []]
    new_s = [[], [], [], [], []]
    for l in range(DEPTH):
        lam_init = 0.8 - 0.6 * math.exp(-0.3 * l)
        lam = (jnp.exp(jnp.sum(lambda_q1[l].astype(jnp.float32) * lambda_k1[l].astype(jnp.float32)))
               - jnp.exp(jnp.sum(lambda_q2[l].astype(jnp.float32) * lambda_k2[l].astype(jnp.float32)))
               + lam_init)
        proj_w = (w_in[l], b_f[l], g_q_diff[l], g_k_diff[l], g_q_fox[l], g_k_fox[l])
        merge_w = (g_sub[l], lam_init, w_branch_diff[l], w_branch_fox[l], w_o[l],
                   g_norm_ffn[l], w_ffn_gate[l], w_ffn_up[l], w_ffn_down[l])

        qd, kd, vd, qf, kf, vf, logf, gd, gf = _project(_rms(xp, g_norm_attn[l]), pos_p, *proj_w)
        cum = jnp.cumsum(logf, axis=1)
        od, of = _prompt_attention(qd, kd, vd, qf, kf, vf, cum, lam)
        xp_next = _merge_ffn(xp, od, of, gd, gf, *merge_w)
        for lst, arr in zip(new_p, (kd, vd, kf, vf, logf)):
            lst.append(arr)

        qd, kd, vd, qf, kf, vf, logf, gd, gf = _project(_rms(xs, g_norm_attn[l]), pos_s, *proj_w)
        kd_all = jnp.concatenate([_gather(cache_k_diff, l, page_table), kd], axis=1)
        vd_all = jnp.concatenate([_gather(cache_v_diff, l, page_table), vd], axis=1)
        kf_all = jnp.concatenate([_gather(cache_k_fox, l, page_table), kf], axis=1)
        vf_all = jnp.concatenate([_gather(cache_v_fox, l, page_table), vf], axis=1)
        logf_all = jnp.concatenate([_gather(cache_logf_fox, l, page_table).astype(jnp.float32), logf], axis=1)
        cum_all = jnp.cumsum(logf_all, axis=1)
        od = _diff_attend(qd, kd_all, vd_all, pos_s, k_pos_s, lam)
        of = _fox_attend(qf, kf_all, vf_all, cum_all[:, past_len:], cum_all, pos_s, k_pos_s)
        xs_next = _merge_ffn(xs, od, of, gd, gf, *merge_w)
        for lst, arr in zip(new_s, (kd, vd, kf, vf, logf)):
            lst.append(arr)

        xp, xs = xp_next, xs_next

    k_diff_prompt = jnp.stack(new_p[0], axis=0)
    v_diff_prompt = jnp.stack(new_p[1], axis=0)
    k_fox_prompt = jnp.stack(new_p[2], axis=0)
    v_fox_prompt = jnp.stack(new_p[3], axis=0)
    logf_fox_prompt = jnp.stack(new_p[4], axis=0)
    k_diff_sample = jnp.stack(new_s[0], axis=0)
    v_diff_sample = jnp.stack(new_s[1], axis=0)
    k_fox_sample = jnp.stack(new_s[2], axis=0)
    v_fox_sample = jnp.stack(new_s[3], axis=0)
    logf_fox_sample = jnp.stack(new_s[4], axis=0)
    return (xp, xs, k_diff_prompt, v_diff_prompt, k_fox_prompt, v_fox_prompt, logf_fox_prompt,
            k_diff_sample, v_diff_sample, k_fox_sample, v_fox_sample, logf_fox_sample)
```

```python
import functools
import math

import jax
import jax.numpy as jnp
from jax import lax
from jax.experimental import pallas as pl
from jax.experimental.pallas import tpu as pltpu

N_HEADS = 8
HEAD_DIM = 128
DIFF_HALF = HEAD_DIM // 2
WIDTH = N_HEADS * HEAD_DIM
ROPE_THETA = 10000.0
EPS = 1e-6
PAGE_SIZE = 128
LANES = 128
NEG = -1e30
MIB = 1024 * 1024

F32 = jnp.float32
BF16 = jnp.bfloat16
HIGHEST = lax.Precision.HIGHEST


def _params(semantics, vmem_mib):
    return pltpu.CompilerParams(dimension_semantics=semantics, vmem_limit_bytes=vmem_mib * MIB)


def _dot(a, b):
    return jnp.dot(a, b, preferred_element_type=F32)


def _dot_nt(a, b, precision=None):
    return lax.dot_general(a, b, (((1,), (1,)), ((), ())), preferred_element_type=F32, precision=precision)


def _lane_iota(shape):
    return lax.broadcasted_iota(jnp.int32, shape, len(shape) - 1)


def _rms_rope_head(a, g, cos, sin_signed, lane):
    sq = a * a
    lo = lane < DIFF_HALF
    s_lo = jnp.sum(jnp.where(lo, sq, 0.0), axis=-1, keepdims=True)
    s_hi = jnp.sum(jnp.where(lo, 0.0, sq), axis=-1, keepdims=True)
    ms = jnp.where(lo, s_lo, s_hi) * (1.0 / DIFF_HALF)
    y = a * lax.rsqrt(ms + EPS) * g
    first = (lane & (DIFF_HALF - 1)) < (DIFF_HALF // 2)
    rot = jnp.where(first, pltpu.roll(y, LANES - DIFF_HALF // 2, 1), pltpu.roll(y, DIFF_HALF // 2, 1))
    return y * cos + rot * sin_signed


def _rms_head(a, g):
    ms = jnp.mean(a * a, axis=-1, keepdims=True)
    return a * lax.rsqrt(ms + EPS) * g


def _log_sigmoid(z):
    return -(jnp.maximum(-z, 0.0) + jnp.log1p(jnp.exp(-jnp.abs(z))))


def _sigmoid(z):
    return 1.0 / (1.0 + jnp.exp(-z))


def _proj_kernel(x_ref, gn_ref, w_ref, wf_ref, bf_ref, gqd_ref, gkd_ref, gqf_ref, gkf_ref, cos_ref, sin_ref,
                 int_ref, kd_ref, vd_ref, kf_ref, vf_ref, logf_ref, xn_ref):
    j = pl.program_id(1)
    tm = x_ref.shape[0]

    @pl.when(j == 0)
    def _():
        x = x_ref[...]
        ms = jnp.mean(x * x, axis=-1, keepdims=True)
        xn_ref[...] = (x * lax.rsqrt(ms + EPS) * gn_ref[...]).astype(BF16)
        z = _dot(xn_ref[...], wf_ref[...]) + bf_ref[...]
        logf_ref[...] = _log_sigmoid(z)[:, :N_HEADS]

    acc = _dot(xn_ref[...], w_ref[...])
    lane = _lane_iota((tm, LANES))

    def heads(fn, out_ref):
        for h in range(N_HEADS):
            sl = slice(h * HEAD_DIM, (h + 1) * HEAD_DIM)
            out_ref[:, sl] = fn(acc[:, sl])

    @pl.when(j == 0)
    def _():
        heads(lambda a: _rms_rope_head(a, gqd_ref[...], cos_ref[...], sin_ref[...], lane), int_ref)

    @pl.when(j == 1)
    def _():
        heads(lambda a: _rms_head(a, gqf_ref[...]), int_ref)

    @pl.when((j >= 2) & (j < 6))
    def _():
        int_ref[...] = _sigmoid(acc)

    @pl.when(j == 6)
    def _():
        heads(lambda a: _rms_rope_head(a, gkd_ref[...], cos_ref[...], sin_ref[...], lane), kd_ref)

    @pl.when(j == 7)
    def _():
        vd_ref[...] = acc

    @pl.when(j == 8)
    def _():
        heads(lambda a: _rms_head(a, gkf_ref[...]), kf_ref)

    @pl.when(j == 9)
    def _():
        vf_ref[...] = acc


def _proj(x, gn, w_main, w_f, b_f, gqd, gkd, gqf, gkf, cos, sin_signed, *, tm, rows_per_table):
    m, d = x.shape
    n_int = 6
    n_tab = rows_per_table // tm
    row = lambda i, j: (i, 0)
    const = lambda i, j: (0, 0)
    kv_spec = pl.BlockSpec((tm, WIDTH), row)
    out_shape = (
        jax.ShapeDtypeStruct((m, n_int * WIDTH), F32),
        jax.ShapeDtypeStruct((m, WIDTH), F32),
        jax.ShapeDtypeStruct((m, WIDTH), F32),
        jax.ShapeDtypeStruct((m, WIDTH), F32),
        jax.ShapeDtypeStruct((m, WIDTH), F32),
        jax.ShapeDtypeStruct((m, N_HEADS), F32),
    )
    return pl.pallas_call(
        _proj_kernel,
        grid=(m // tm, n_int + 4),
        in_specs=[
            pl.BlockSpec((tm, d), row),
            pl.BlockSpec((1, d), const),
            pl.BlockSpec((d, WIDTH), lambda i, j: (0, j)),
            pl.BlockSpec((d, LANES), const),
            pl.BlockSpec((1, LANES), const),
            pl.BlockSpec((1, LANES), const),
            pl.BlockSpec((1, LANES), const),
            pl.BlockSpec((1, LANES), const),
            pl.BlockSpec((1, LANES), const),
            pl.BlockSpec((tm, LANES), lambda i, j: (i % n_tab, 0)),
            pl.BlockSpec((tm, LANES), lambda i, j: (i % n_tab, 0)),
        ],
        out_specs=(
            pl.BlockSpec((tm, WIDTH), lambda i, j: (i, jnp.minimum(j, n_int - 1))),
            kv_spec, kv_spec, kv_spec, kv_spec,
            pl.BlockSpec((tm, N_HEADS), row),
        ),
        out_shape=out_shape,
        scratch_shapes=[pltpu.VMEM((tm, d), BF16)],
        compiler_params=_params(("arbitrary", "arbitrary"), 48),
        name="proj",
    )(x, gn, w_main, w_f, b_f, gqd, gkd, gqf, gkf, cos, sin_signed)


def _pad_lanes(chunk, lane):
    out = jnp.zeros(lane.shape, F32)
    for h in range(N_HEADS):
        out = jnp.where(lane == h, chunk[:, h:h + 1], out)
    return out


def _cumsum_kernel(lf_ref, cum_ref, cumt_ref):
    s = lf_ref.shape[0]
    r = lax.broadcasted_iota(jnp.int32, (LANES, LANES), 0)
    c = lax.broadcasted_iota(jnp.int32, (LANES, LANES), 1)
    tri = (c <= r).astype(F32)
    carry = jnp.zeros((1, LANES), F32)
    for ci in range(s // LANES):
        rows = slice(ci * LANES, (ci + 1) * LANES)
        pad = _pad_lanes(lf_ref[rows, :], c)
        res = jnp.dot(tri, pad, preferred_element_type=F32, precision=HIGHEST) + carry
        cum_ref[rows, :] = res[:, :N_HEADS]
        cumt_ref[0, :, rows] = res.T[:N_HEADS, :]
        carry = res[LANES - 1:LANES, :]


def _cumsum(logf, batch, seq):
    return pl.pallas_call(
        _cumsum_kernel,
        grid=(batch,),
        in_specs=[pl.BlockSpec((seq, N_HEADS), lambda b: (b, 0))],
        out_specs=(
            pl.BlockSpec((seq, N_HEADS), lambda b: (b, 0)),
            pl.BlockSpec((1, N_HEADS, seq), lambda b: (b, 0, 0)),
        ),
        out_shape=(
            jax.ShapeDtypeStruct((batch * seq, N_HEADS), F32),
            jax.ShapeDtypeStruct((batch, N_HEADS, seq), F32),
        ),
        compiler_params=_params(("arbitrary",), 32),
        name="cumsum",
    )(logf)


def _flash_update(s, v_bf, m_ref, l_ref, acc_ref, h):
    m_prev = m_ref[h]
    m_new = jnp.maximum(m_prev, jnp.max(s, axis=-1, keepdims=True))
    alpha = jnp.exp(m_prev - m_new)
    p = jnp.exp(s - m_new)
    l_ref[h] = alpha * l_ref[h] + jnp.sum(p, axis=-1, keepdims=True)
    acc_ref[h] = alpha * acc_ref[h] + _dot(p.astype(BF16), v_bf)
    m_ref[h] = m_new


def _lambda_value(lq1_ref, lk1_ref, lq2_ref, lk2_ref, lam_init):
    a = jnp.sum(lq1_ref[...] * lk1_ref[...], axis=-1, keepdims=True)
    b = jnp.sum(lq2_ref[...] * lk2_ref[...], axis=-1, keepdims=True)
    return jnp.exp(a) - jnp.exp(b) + lam_init


def _sub_norm(o, g, lam_init):
    ms = jnp.mean(o * o, axis=-1, keepdims=True)
    return o * lax.rsqrt(ms + EPS) * g * (1.0 - lam_init)


def _causal_mask(s, qi, ki, tq, tk):
    rows = lax.broadcasted_iota(jnp.int32, s.shape, 0)
    qpos = qi * tq + jnp.where(rows >= tq, rows - tq, rows)
    kpos = ki * tk + lax.broadcasted_iota(jnp.int32, s.shape, 1)
    return jnp.where(kpos <= qpos, s, NEG)


def _diff_attn_kernel(lq1_ref, lk1_ref, lq2_ref, lk2_ref, gsub_ref, q_ref, k_ref, v_ref, o_ref,
                      qs_ref, m_ref, l_ref, acc_ref, *, lam_init):
    qi, ki = pl.program_id(1), pl.program_id(2)
    tq, tk = q_ref.shape[0], k_ref.shape[0]
    last = (qi * tq + tq - 1) // tk

    @pl.when(ki == 0)
    def _():
        lane = _lane_iota((tq, LANES))
        scale = DIFF_HALF ** -0.5
        for h in range(N_HEADS):
            qh = q_ref[:, h * HEAD_DIM:(h + 1) * HEAD_DIM] * scale
            qs_ref[h, :tq, :] = jnp.where(lane < DIFF_HALF, qh, 0.0).astype(BF16)
            qs_ref[h, tq:, :] = jnp.where(lane < DIFF_HALF, 0.0, qh).astype(BF16)
        m_ref[...] = jnp.full(m_ref.shape, NEG, F32)
        l_ref[...] = jnp.zeros(l_ref.shape, F32)
        acc_ref[...] = jnp.zeros(acc_ref.shape, F32)

    def step(masked):
        for h in range(N_HEADS):
            sl = slice(h * HEAD_DIM, (h + 1) * HEAD_DIM)
            s = _dot_nt(qs_ref[h], k_ref[:, sl].astype(BF16))
            if masked:
                s = _causal_mask(s, qi, ki, tq, tk)
            _flash_update(s, v_ref[:, sl].astype(BF16), m_ref, l_ref, acc_ref, h)

    @pl.when(ki < last)
    def _():
        step(False)

    @pl.when(ki == last)
    def _():
        step(True)
        lam = _lambda_value(lq1_ref, lk1_ref, lq2_ref, lk2_ref, lam_init)
        for h in range(N_HEADS):
            o = acc_ref[h] / l_ref[h]
            o = o[:tq] - lam * o[tq:]
            o_ref[:, h * HEAD_DIM:(h + 1) * HEAD_DIM] = _sub_norm(o, gsub_ref[...], lam_init)


def _fox_attn_kernel(q_ref, k_ref, v_ref, cq_ref, ckt_ref, o_ref, qs_ref, m_ref, l_ref, acc_ref):
    qi, ki = pl.program_id(1), pl.program_id(2)
    tq, tk = q_ref.shape[0], k_ref.shape[0]
    last = (qi * tq + tq - 1) // tk
    scale = HEAD_DIM ** -0.5

    @pl.when(ki == 0)
    def _():
        qs_ref[...] = q_ref[...].astype(BF16)
        m_ref[...] = jnp.full(m_ref.shape, NEG, F32)
        l_ref[...] = jnp.zeros(l_ref.shape, F32)
        acc_ref[...] = jnp.zeros(acc_ref.shape, F32)

    def step(masked):
        for h in range(N_HEADS):
            sl = slice(h * HEAD_DIM, (h + 1) * HEAD_DIM)
            s = _dot_nt(qs_ref[:, sl], k_ref[:, sl].astype(BF16)) * scale
            s = s + (cq_ref[:, h:h + 1] - ckt_ref[0, h:h + 1, :])
            if masked:
                s = _causal_mask(s, qi, ki, tq, tk)
            _flash_update(s, v_ref[:, sl].astype(BF16), m_ref, l_ref, acc_ref, h)

    @pl.when(ki < last)
    def _():
        step(False)

    @pl.when(ki == last)
    def _():
        step(True)
        for h in range(N_HEADS):
            o_ref[:, h * HEAD_DIM:(h + 1) * HEAD_DIM] = acc_ref[h] / l_ref[h]


def _attn_specs(batch, seq, tq, tk):
    nq, nk = seq // tq, seq // tk
    q_spec = pl.BlockSpec((tq, WIDTH), lambda b, qi, ki: (b * nq + qi, 0))
    kv_map = lambda b, qi, ki: (b * nk + jnp.minimum(ki, (qi * tq + tq - 1) // tk), 0)
    kv_spec = pl.BlockSpec((tk, WIDTH), kv_map)
    return nq, nk, q_spec, kv_spec


def _diff_attention(lams, g_sub, q_arr, q_col, k, v, *, batch, seq, tq, tk, lam_init):
    nq, nk, _, kv_spec = _attn_specs(batch, seq, tq, tk)
    q_spec = pl.BlockSpec((tq, WIDTH), lambda b, qi, ki: (b * nq + qi, q_col))
    o_spec = pl.BlockSpec((tq, WIDTH), lambda b, qi, ki: (b * nq + qi, 0))
    small = lambda n: pl.BlockSpec((1, n), lambda b, qi, ki: (0, 0))
    return pl.pallas_call(
        functools.partial(_diff_attn_kernel, lam_init=lam_init),
        grid=(batch, nq, nk),
        in_specs=[small(DIFF_HALF)] * 4 + [small(LANES), q_spec, kv_spec, kv_spec],
        out_specs=o_spec,
        out_shape=jax.ShapeDtypeStruct((batch * seq, WIDTH), F32),
        scratch_shapes=[
            pltpu.VMEM((N_HEADS, 2 * tq, HEAD_DIM), BF16),
            pltpu.VMEM((N_HEADS, 2 * tq, 1), F32),
            pltpu.VMEM((N_HEADS, 2 * tq, 1), F32),
            pltpu.VMEM((N_HEADS, 2 * tq, HEAD_DIM), F32),
        ],
        compiler_params=_params(("arbitrary", "arbitrary", "arbitrary"), 48),
        name="diff_attention",
    )(*lams, g_sub, q_arr, k, v)


def _fox_attention(q_arr, q_col, k, v, cum, cumt, *, batch, seq, tq, tk):
    nq, nk, _, kv_spec = _attn_specs(batch, seq, tq, tk)
    q_spec = pl.BlockSpec((tq, WIDTH), lambda b, qi, ki: (b * nq + qi, q_col))
    o_spec = pl.BlockSpec((tq, WIDTH), lambda b, qi, ki: (b * nq + qi, 0))
    cq_spec = pl.BlockSpec((tq, N_HEADS), lambda b, qi, ki: (b * nq + qi, 0))
    ck_spec = pl.BlockSpec((1, N_HEADS, tk), lambda b, qi, ki: (b, 0, jnp.minimum(ki, (qi * tq + tq - 1) // tk)))
    return pl.pallas_call(
        _fox_attn_kernel,
        grid=(batch, nq, nk),
        in_specs=[q_spec, kv_spec, kv_spec, cq_spec, ck_spec],
        out_specs=o_spec,
        out_shape=jax.ShapeDtypeStruct((batch * seq, WIDTH), F32),
        scratch_shapes=[
            pltpu.VMEM((tq, WIDTH), BF16),
            pltpu.VMEM((N_HEADS, tq, 1), F32),
            pltpu.VMEM((N_HEADS, tq, 1), F32),
            pltpu.VMEM((N_HEADS, tq, HEAD_DIM), F32),
        ],
        compiler_params=_params(("arbitrary", "arbitrary", "arbitrary"), 48),
        name="fox_attention",
    )(q_arr, k, v, cum, cumt)


ROWS = 2 * N_HEADS
PAGE_COLS = PAGE_SIZE * N_HEADS


def _split3(a):
    hi = a.astype(BF16)
    r1 = a - hi.astype(F32)
    mid = r1.astype(BF16)
    lo = (r1 - mid.astype(F32)).astype(BF16)
    return hi, mid, lo


def _stack2(a):
    return jnp.concatenate([a, a], axis=0)


def _decode_kernel(pt_ref, lq1_ref, lk1_ref, lq2_ref, lk2_ref, gsub_ref,
                   qd_ref, qf_ref, kdn_ref, vdn_ref, kfn_ref, vfn_ref, lfn_ref,
                   kd_ref, vd_ref, kf_ref, vf_ref, lf_ref,
                   od_ref, of_ref,
                   qsd_ref, qsf_ref, md_ref, ld_ref, accd_ref, mf_ref, lf_acc_ref, accf_ref, carry_ref,
                   *, lam_init):
    del pt_ref
    p = pl.program_id(1)
    n_pages = pl.num_programs(1)
    scale_f = HEAD_DIM ** -0.5

    @pl.when(p == 0)
    def _():
        lane = _lane_iota((N_HEADS, HEAD_DIM))
        q = qd_ref[0] * (DIFF_HALF ** -0.5)
        qd = jnp.concatenate([jnp.where(lane < DIFF_HALF, q, 0.0), jnp.where(lane < DIFF_HALF, 0.0, q)], axis=0)
        qf = jnp.concatenate([qf_ref[0], jnp.zeros((N_HEADS, HEAD_DIM), F32)], axis=0)
        qsd_ref[...] = qd.astype(BF16)
        qsf_ref[...] = qf.astype(BF16)
        md_ref[...] = jnp.sum(qd * _stack2(kdn_ref[0]), axis=-1, keepdims=True)
        mf_ref[...] = jnp.sum(qf * _stack2(kfn_ref[0]), axis=-1, keepdims=True) * scale_f
        ld_ref[...] = jnp.ones(ld_ref.shape, F32)
        lf_acc_ref[...] = jnp.ones(lf_acc_ref.shape, F32)
        accd_ref[...] = _stack2(vdn_ref[0])
        accf_ref[...] = _stack2(vfn_ref[0])
        carry_ref[...] = _stack2(lfn_ref[0])

    r = lax.broadcasted_iota(jnp.int32, (ROWS, PAGE_COLS), 0)
    c = lax.broadcasted_iota(jnp.int32, (ROWS, PAGE_COLS), 1)
    own_head = (r & (N_HEADS - 1)) == (c & (N_HEADS - 1))

    def update(s, v_ref, m_ref, l_ref, acc_ref):
        s = jnp.where(own_head, s, NEG)
        m_prev = m_ref[...]
        m_new = jnp.maximum(m_prev, jnp.max(s, axis=-1, keepdims=True))
        alpha = jnp.exp(m_prev - m_new)
        pr = jnp.exp(s - m_new)
        l_ref[...] = alpha * l_ref[...] + jnp.sum(pr, axis=-1, keepdims=True)
        v = v_ref[...].reshape(PAGE_COLS, HEAD_DIM).astype(BF16)
        acc_ref[...] = alpha * acc_ref[...] + _dot(pr.astype(BF16), v)
        m_ref[...] = m_new

    def logits(q_ref, k_ref):
        return _dot_nt(q_ref[...], k_ref[...].reshape(PAGE_COLS, HEAD_DIM).astype(BF16))

    update(logits(qsd_ref, kd_ref), vd_ref, md_ref, ld_ref, accd_ref)

    ks = lax.broadcasted_iota(jnp.int32, (PAGE_SIZE, PAGE_COLS), 0)
    kc = lax.broadcasted_iota(jnp.int32, (PAGE_SIZE, PAGE_COLS), 1)
    later = (ks > (kc >> 3)).astype(BF16)
    page_lf = _stack2(lf_ref[...])
    bias = carry_ref[...]
    for term in _split3(page_lf):
        bias = bias + _dot(term, later)
    update(logits(qsf_ref, kf_ref) * scale_f + bias, vf_ref, mf_ref, lf_acc_ref, accf_ref)
    carry_ref[...] = carry_ref[...] + jnp.sum(page_lf, axis=-1, keepdims=True)

    @pl.when(p == n_pages - 1)
    def _():
        lam = _lambda_value(lq1_ref, lk1_ref, lq2_ref, lk2_ref, lam_init)
        o_d = accd_ref[...] / ld_ref[...]
        o = o_d[:N_HEADS] - lam * o_d[N_HEADS:]
        od_ref[0] = _sub_norm(o, gsub_ref[...], lam_init)
        of_ref[0] = (accf_ref[...] / lf_acc_ref[...])[:N_HEADS]


def _decode(page_table, lams, g_sub, qd, qf, kd_new, vd_new, kf_new, vf_new, lf_new,
            cache_kd, cache_vd, cache_kf, cache_vf, cache_lf, *, lam_init):
    n_samples, n_pages = page_table.shape
    pt_flat = page_table.reshape(-1)
    tile3 = lambda a: a.reshape(n_samples, N_HEADS, HEAD_DIM)
    tile_spec = pl.BlockSpec((1, N_HEADS, HEAD_DIM), lambda b, p, pt: (b, 0, 0))
    small = lambda n: pl.BlockSpec((1, n), lambda b, p, pt: (0, 0))
    page_of = lambda b, p, pt: pt[b * n_pages + n_pages - 1 - p]
    page_spec = pl.BlockSpec((None, PAGE_SIZE, N_HEADS, HEAD_DIM), lambda b, p, pt: (page_of(b, p, pt), 0, 0, 0))
    lf_spec = pl.BlockSpec((None, N_HEADS, PAGE_SIZE), lambda b, p, pt: (page_of(b, p, pt), 0, 0))
    grid_spec = pltpu.PrefetchScalarGridSpec(
        num_scalar_prefetch=1,
        grid=(n_samples, n_pages),
        in_specs=[small(DIFF_HALF)] * 4 + [small(LANES)] + [tile_spec] * 6
        + [pl.BlockSpec((1, N_HEADS, 1), lambda b, p, pt: (b, 0, 0))]
        + [page_spec] * 4 + [lf_spec],
        out_specs=(tile_spec, tile_spec),
        scratch_shapes=[
            pltpu.VMEM((ROWS, HEAD_DIM), BF16),
            pltpu.VMEM((ROWS, HEAD_DIM), BF16),
            pltpu.VMEM((ROWS, 1), F32),
            pltpu.VMEM((ROWS, 1), F32),
            pltpu.VMEM((ROWS, HEAD_DIM), F32),
            pltpu.VMEM((ROWS, 1), F32),
            pltpu.VMEM((ROWS, 1), F32),
            pltpu.VMEM((ROWS, HEAD_DIM), F32),
            pltpu.VMEM((ROWS, 1), F32),
        ],
    )
    od, of = pl.pallas_call(
        functools.partial(_decode_kernel, lam_init=lam_init),
        grid_spec=grid_spec,
        out_shape=(
            jax.ShapeDtypeStruct((n_samples, N_HEADS, HEAD_DIM), F32),
            jax.ShapeDtypeStruct((n_samples, N_HEADS, HEAD_DIM), F32),
        ),
        compiler_params=_params(("arbitrary", "arbitrary"), 32),
        name="decode",
    )(pt_flat, *lams, g_sub, tile3(qd), tile3(qf), tile3(kd_new), tile3(vd_new), tile3(kf_new), tile3(vf_new),
      lf_new.reshape(n_samples, N_HEADS, 1), cache_kd, cache_vd, cache_kf, cache_vf,
      jnp.swapaxes(cache_lf, 1, 2))
    return od.reshape(n_samples, WIDTH), of.reshape(n_samples, WIDTH)


def _merge_kernel(od_ref, of_ref, sgd_ref, sgf_ref, x_ref, wbd_ref, wbf_ref, wo_ref, h_ref, mg_ref):
    @pl.when(pl.program_id(1) == 0)
    def _():
        a = _dot(od_ref[...].astype(BF16), wbd_ref[...])
        b = _dot(of_ref[...].astype(BF16), wbf_ref[...])
        mg_ref[...] = (sgd_ref[...] * a + sgf_ref[...] * b).astype(BF16)

    h_ref[...] = x_ref[...] + _dot(mg_ref[...], wo_ref[...])


def _merge(od, of, internal, x, w_bd, w_bf, w_o, *, tm, tn):
    m, d = x.shape
    first_gate = 2 * WIDTH // d
    return pl.pallas_call(
        _merge_kernel,
        grid=(m // tm, d // tn),
        in_specs=[
            pl.BlockSpec((tm, WIDTH), lambda i, j: (i, 0)),
            pl.BlockSpec((tm, WIDTH), lambda i, j: (i, 0)),
            pl.BlockSpec((tm, d), lambda i, j: (i, first_gate)),
            pl.BlockSpec((tm, d), lambda i, j: (i, first_gate + 1)),
            pl.BlockSpec((tm, tn), lambda i, j: (i, j)),
            pl.BlockSpec((WIDTH, d), lambda i, j: (0, 0)),
            pl.BlockSpec((WIDTH, d), lambda i, j: (0, 0)),
            pl.BlockSpec((d, tn), lambda i, j: (0, j)),
        ],
        out_specs=pl.BlockSpec((tm, tn), lambda i, j: (i, j)),
        out_shape=jax.ShapeDtypeStruct((m, d), F32),
        scratch_shapes=[pltpu.VMEM((tm, d), BF16)],
        compiler_params=_params(("arbitrary", "arbitrary"), 48),
        name="merge",
    )(od, of, internal, internal, x, w_bd, w_bf, w_o)


def _ffn_kernel(h_ref, g_ref, wg_ref, wu_ref, wd_ref, o_ref, hn_ref):
    @pl.when(pl.program_id(1) == 0)
    def _():
        h = h_ref[...]
        ms = jnp.mean(h * h, axis=-1, keepdims=True)
        hn_ref[...] = (h * lax.rsqrt(ms + EPS) * g_ref[...]).astype(BF16)
        o_ref[...] = h

    hn = hn_ref[...]
    a = _dot(hn, wg_ref[...])
    u = _dot(hn, wu_ref[...])
    ff = (a * _sigmoid(a) * u).astype(BF16)
    o_ref[...] += _dot(ff, wd_ref[...])


def _ffn(h, g, w_gate, w_up, w_down, *, tm, tf):
    m, d = h.shape
    f = w_gate.shape[1]
    return pl.pallas_call(
        _ffn_kernel,
        grid=(m // tm, f // tf),
        in_specs=[
            pl.BlockSpec((tm, d), lambda i, j: (i, 0)),
            pl.BlockSpec((1, d), lambda i, j: (0, 0)),
            pl.BlockSpec((d, tf), lambda i, j: (0, j)),
            pl.BlockSpec((d, tf), lambda i, j: (0, j)),
            pl.BlockSpec((tf, d), lambda i, j: (j, 0)),
        ],
        out_specs=pl.BlockSpec((tm, d), lambda i, j: (i, 0)),
        out_shape=jax.ShapeDtypeStruct((m, d), F32),
        scratch_shapes=[pltpu.VMEM((tm, d), BF16)],
        compiler_params=_params(("arbitrary", "arbitrary"), 48),
        name="ffn",
    )(h, g, w_gate, w_up, w_down)


def _rope_tables(pos):
    half = DIFF_HALF // 2
    inv = ROPE_THETA ** (-jnp.arange(half, dtype=F32) / half)
    ang = pos.astype(F32)[:, None] * inv[None, :]
    cos = jnp.concatenate([jnp.cos(ang)] * 4, axis=-1)
    sin = jnp.sin(ang)
    sin_signed = jnp.concatenate([-sin, sin, -sin, sin], axis=-1)
    return cos, sin_signed


def _tile(m, pref):
    return pref if m % pref == 0 else m


def _layer(l, xp, xs, caches, page_table, weights):
    (g_norm_attn, w_in, b_f, g_q_diff, g_k_diff, g_q_fox, g_k_fox, lambda_q1, lambda_k1, lambda_q2, lambda_k2,
     g_sub, w_branch_diff, w_branch_fox, w_o, g_norm_ffn, w_ffn_gate, w_ffn_up, w_ffn_down) = [w[l] for w in weights]
    batch, seq, d = xp.shape
    n_samples, dec_seq, _ = xs.shape
    past_len = page_table.shape[1] * PAGE_SIZE
    lam_init = 0.8 - 0.6 * math.exp(-0.3 * l)

    o = [0]
    for wdt in (WIDTH, WIDTH, WIDTH, WIDTH, WIDTH, WIDTH, N_HEADS, d, d):
        o.append(o[-1] + wdt)
    col = lambda a, b: w_in[:, o[a]:o[b]]
    w_main = jnp.concatenate([col(0, 1), col(3, 4), col(7, 9), col(1, 3), col(4, 6)], axis=1).astype(BF16)
    w_f = jnp.pad(col(6, 7), ((0, 0), (0, LANES - N_HEADS))).astype(BF16)
    b_f_pad = jnp.pad(b_f, (0, LANES - N_HEADS)).reshape(1, LANES)
    two = lambda g: jnp.concatenate([g, g]).reshape(1, LANES)
    one = lambda g: g.reshape(1, -1)
    lams = tuple(one(v) for v in (lambda_q1, lambda_k1, lambda_q2, lambda_k2))
    w_bd, w_bf, w_out = w_branch_diff.astype(BF16), w_branch_fox.astype(BF16), w_o.astype(BF16)
    w_g, w_u, w_d = w_ffn_gate.astype(BF16), w_ffn_up.astype(BF16), w_ffn_down.astype(BF16)

    def project(x2d, pos, tm, rows_per_table):
        cos, sin_signed = _rope_tables(pos)
        return _proj(x2d, one(g_norm_attn), w_main, w_f, b_f_pad, two(g_q_diff), two(g_k_diff), one(g_q_fox),
                     one(g_k_fox), cos, sin_signed, tm=tm, rows_per_table=rows_per_table)

    def finish(x2d, od, of, internal, tm_merge, tm_ffn):
        h = _merge(od, of, internal, x2d, w_bd, w_bf, w_out, tm=tm_merge, tn=512)
        return _ffn(h, one(g_norm_ffn), w_g, w_u, w_d, tm=tm_ffn, tf=512)

    xp2 = xp.reshape(batch * seq, d)
    tm_p = _tile(seq, 512)
    int_p, kd_p, vd_p, kf_p, vf_p, lf_p = project(xp2, jnp.arange(seq), tm_p, seq)
    cum, cumt = _cumsum(lf_p, batch, seq)
    tq, tk = _tile(seq, 256), _tile(seq, 512)
    od_p = _diff_attention(lams, one(g_sub), int_p, 0, kd_p, vd_p, batch=batch, seq=seq, tq=tq, tk=tk,
                           lam_init=lam_init)
    of_p = _fox_attention(int_p, 1, kf_p, vf_p, cum, cumt, batch=batch, seq=seq, tq=tq, tk=tk)
    yp = finish(xp2, od_p, of_p, int_p, _tile(seq, 256), _tile(seq, 512))

    assert dec_seq == 1
    xs2 = xs.reshape(n_samples, d)
    pos_s = jnp.full((n_samples,), past_len, jnp.int32)
    int_s, kd_s, vd_s, kf_s, vf_s, lf_s = project(xs2, pos_s, n_samples, n_samples)
    od_s, of_s = _decode(page_table, lams, one(g_sub), int_s[:, :WIDTH], int_s[:, WIDTH:2 * WIDTH],
                         kd_s, vd_s, kf_s, vf_s, lf_s, *[c[l] for c in caches], lam_init=lam_init)
    ys = finish(xs2, od_s, of_s, int_s, n_samples, n_samples)

    heads = lambda a, b, t: a.reshape(b, t, N_HEADS, HEAD_DIM)
    new_p = (heads(kd_p, batch, seq), heads(vd_p, batch, seq), heads(kf_p, batch, seq), heads(vf_p, batch, seq),
             lf_p.reshape(batch, seq, N_HEADS))
    new_s = (heads(kd_s, n_samples, 1), heads(vd_s, n_samples, 1), heads(kf_s, n_samples, 1),
             heads(vf_s, n_samples, 1), lf_s.reshape(n_samples, 1, N_HEADS))
    return yp.reshape(batch, seq, d), ys.reshape(n_samples, dec_seq, d), new_p, new_s


def kernel(x_prompt, x_sample, cache_k_diff, cache_v_diff, cache_k_fox, cache_v_fox, cache_logf_fox, page_table,
           g_norm_attn, w_in, b_f, g_q_diff, g_k_diff, g_q_fox, g_k_fox, lambda_q1, lambda_k1, lambda_q2, lambda_k2,
           g_sub, w_branch_diff, w_branch_fox, w_o, g_norm_ffn, w_ffn_gate, w_ffn_up, w_ffn_down):
    weights = (g_norm_attn, w_in, b_f, g_q_diff, g_k_diff, g_q_fox, g_k_fox, lambda_q1, lambda_k1, lambda_q2,
               lambda_k2, g_sub, w_branch_diff, w_branch_fox, w_o, g_norm_ffn, w_ffn_gate, w_ffn_up, w_ffn_down)
    caches = (cache_k_diff, cache_v_diff, cache_k_fox, cache_v_fox, cache_logf_fox)
    depth = w_in.shape[0]
    xp, xs = x_prompt, x_sample
    new_p, new_s = [], []
    for l in range(depth):
        xp, xs, np_l, ns_l = _layer(l, xp, xs, caches, page_table, weights)
        new_p.append(np_l)
        new_s.append(ns_l)
    stack = lambda lst, i: jnp.stack([t[i] for t in lst], axis=0)
    return (xp, xs) + tuple(stack(new_p, i) for i in range(5)) + tuple(stack(new_s, i) for i in range(5))
```

```python
import functools
import math

import jax
import jax.numpy as jnp
from jax import lax
from jax.experimental import pallas as pl
from jax.experimental.pallas import tpu as pltpu

N_HEADS = 8
HEAD_DIM = 128
DIFF_HALF = HEAD_DIM // 2
WIDTH = N_HEADS * HEAD_DIM
ROPE_THETA = 10000.0
EPS = 1e-6
PAGE_SIZE = 128
LANES = 128
NEG = -1e30
MIB = 1024 * 1024

F32 = jnp.float32
BF16 = jnp.bfloat16
HIGHEST = lax.Precision.HIGHEST


def _params(semantics, vmem_mib):
    return pltpu.CompilerParams(dimension_semantics=semantics, vmem_limit_bytes=vmem_mib * MIB)


def _dot(a, b):
    return jnp.dot(a, b, preferred_element_type=F32)


def _dot_nt(a, b, precision=None):
    return lax.dot_general(a, b, (((1,), (1,)), ((), ())), preferred_element_type=F32, precision=precision)


def _lane_iota(shape):
    return lax.broadcasted_iota(jnp.int32, shape, len(shape) - 1)


def _rms_rope_head(a, g, cos, sin_signed, lane):
    sq = a * a
    lo = lane < DIFF_HALF
    s_lo = jnp.sum(jnp.where(lo, sq, 0.0), axis=-1, keepdims=True)
    s_hi = jnp.sum(jnp.where(lo, 0.0, sq), axis=-1, keepdims=True)
    ms = jnp.where(lo, s_lo, s_hi) * (1.0 / DIFF_HALF)
    y = a * lax.rsqrt(ms + EPS) * g
    first = (lane & (DIFF_HALF - 1)) < (DIFF_HALF // 2)
    rot = jnp.where(first, pltpu.roll(y, LANES - DIFF_HALF // 2, 1), pltpu.roll(y, DIFF_HALF // 2, 1))
    return y * cos + rot * sin_signed


def _rms_head(a, g):
    ms = jnp.mean(a * a, axis=-1, keepdims=True)
    return a * lax.rsqrt(ms + EPS) * g


def _log_sigmoid(z):
    return -(jnp.maximum(-z, 0.0) + jnp.log1p(jnp.exp(-jnp.abs(z))))


def _sigmoid(z):
    return 1.0 / (1.0 + jnp.exp(-z))


def _proj_kernel(x_ref, gn_ref, w_ref, wf_ref, bf_ref, gqd_ref, gkd_ref, gqf_ref, gkf_ref, cos_ref, sin_ref,
                 int_ref, kd_ref, vd_ref, kf_ref, vf_ref, logf_ref, xn_ref):
    j = pl.program_id(1)
    tm = x_ref.shape[0]

    @pl.when(j == 0)
    def _():
        x = x_ref[...]
        ms = jnp.mean(x * x, axis=-1, keepdims=True)
        xn_ref[...] = (x * lax.rsqrt(ms + EPS) * gn_ref[...]).astype(BF16)
        z = _dot(xn_ref[...], wf_ref[...]) + bf_ref[...]
        logf_ref[...] = _log_sigmoid(z)[:, :N_HEADS]

    lane = _lane_iota((tm, LANES))
    pair = 2 * HEAD_DIM

    def head_pairs():
        for c in range(WIDTH // pair):
            acc = _dot(xn_ref[...], w_ref[:, c * pair:(c + 1) * pair])
            for k in range(2):
                yield 2 * c + k, acc[:, k * HEAD_DIM:(k + 1) * HEAD_DIM]

    def heads(fn, out_ref):
        for h, a in head_pairs():
            out_ref[:, h * HEAD_DIM:(h + 1) * HEAD_DIM] = fn(a)

    def heads_token_major(fn, out_ref):
        for h, a in head_pairs():
            out_ref[pl.ds(h, tm, stride=N_HEADS), :] = fn(a)

    @pl.when(j == 0)
    def _():
        heads(lambda a: _rms_rope_head(a, gqd_ref[...], cos_ref[...], sin_ref[...], lane), int_ref)

    @pl.when(j == 1)
    def _():
        heads(lambda a: _rms_head(a, gqf_ref[...]), int_ref)

    @pl.when((j >= 2) & (j < 6))
    def _():
        heads(_sigmoid, int_ref)

    @pl.when(j == 6)
    def _():
        heads_token_major(lambda a: _rms_rope_head(a, gkd_ref[...], cos_ref[...], sin_ref[...], lane), kd_ref)

    @pl.when(j == 7)
    def _():
        heads_token_major(lambda a: a, vd_ref)

    @pl.when(j == 8)
    def _():
        heads_token_major(lambda a: _rms_head(a, gkf_ref[...]), kf_ref)

    @pl.when(j == 9)
    def _():
        heads_token_major(lambda a: a, vf_ref)


def _proj(x, gn, w_main, w_f, b_f, gqd, gkd, gqf, gkf, cos, sin_signed, *, tm, rows_per_table):
    m, d = x.shape
    n_int = 6
    n_tab = rows_per_table // tm
    row = lambda i, j: (i, 0)
    const = lambda i, j: (0, 0)
    kv_spec = pl.BlockSpec((tm * N_HEADS, HEAD_DIM), row)
    kv_shape = jax.ShapeDtypeStruct((m * N_HEADS, HEAD_DIM), F32)
    out_shape = (
        jax.ShapeDtypeStruct((m, n_int * WIDTH), F32),
        kv_shape, kv_shape, kv_shape, kv_shape,
        jax.ShapeDtypeStruct((m, N_HEADS), F32),
    )
    return pl.pallas_call(
        _proj_kernel,
        grid=(m // tm, n_int + 4),
        in_specs=[
            pl.BlockSpec((tm, d), row),
            pl.BlockSpec((1, d), const),
            pl.BlockSpec((d, WIDTH), lambda i, j: (0, j)),
            pl.BlockSpec((d, LANES), const),
            pl.BlockSpec((1, LANES), const),
            pl.BlockSpec((1, LANES), const),
            pl.BlockSpec((1, LANES), const),
            pl.BlockSpec((1, LANES), const),
            pl.BlockSpec((1, LANES), const),
            pl.BlockSpec((tm, LANES), lambda i, j: (i % n_tab, 0)),
            pl.BlockSpec((tm, LANES), lambda i, j: (i % n_tab, 0)),
        ],
        out_specs=(
            pl.BlockSpec((tm, WIDTH), lambda i, j: (i, jnp.minimum(j, n_int - 1))),
            kv_spec, kv_spec, kv_spec, kv_spec,
            pl.BlockSpec((tm, N_HEADS), row),
        ),
        out_shape=out_shape,
        scratch_shapes=[pltpu.VMEM((tm, d), BF16)],
        compiler_params=_params(("arbitrary", "arbitrary"), 48),
        name="proj",
    )(x, gn, w_main, w_f, b_f, gqd, gkd, gqf, gkf, cos, sin_signed)


def _pad_lanes(chunk, lane):
    out = jnp.zeros(lane.shape, F32)
    for h in range(N_HEADS):
        out = jnp.where(lane == h, chunk[:, h:h + 1], out)
    return out


def _cumsum_kernel(lf_ref, cum_ref, cumt_ref):
    s = lf_ref.shape[0]
    r = lax.broadcasted_iota(jnp.int32, (LANES, LANES), 0)
    c = lax.broadcasted_iota(jnp.int32, (LANES, LANES), 1)
    tri = (c <= r).astype(F32)
    carry = jnp.zeros((1, LANES), F32)
    for ci in range(s // LANES):
        rows = slice(ci * LANES, (ci + 1) * LANES)
        pad = _pad_lanes(lf_ref[rows, :], c)
        res = jnp.dot(tri, pad, preferred_element_type=F32, precision=HIGHEST) + carry
        cum_ref[rows, :] = res[:, :N_HEADS]
        cumt_ref[0, :, rows] = res.T[:N_HEADS, :]
        carry = res[LANES - 1:LANES, :]


def _cumsum(logf, batch, seq):
    return pl.pallas_call(
        _cumsum_kernel,
        grid=(batch,),
        in_specs=[pl.BlockSpec((seq, N_HEADS), lambda b: (b, 0))],
        out_specs=(
            pl.BlockSpec((seq, N_HEADS), lambda b: (b, 0)),
            pl.BlockSpec((1, N_HEADS, seq), lambda b: (b, 0, 0)),
        ),
        out_shape=(
            jax.ShapeDtypeStruct((batch * seq, N_HEADS), F32),
            jax.ShapeDtypeStruct((batch, N_HEADS, seq), F32),
        ),
        compiler_params=_params(("arbitrary",), 32),
        name="cumsum",
    )(logf)


def _flash_update(s, cq, v_bf, m_ref, l_ref, acc_ref, h):
    reps = s.shape[1] // LANES
    m_prev = m_ref[h]
    m_curr = jnp.max(s, axis=-1, keepdims=True)
    if cq is not None:
        m_curr = m_curr + cq
    m_new = jnp.maximum(m_prev, m_curr)
    alpha = jnp.exp(m_prev - m_new)
    shift = m_new if cq is None else m_new - cq
    p = jnp.exp(s - jnp.tile(shift, (1, reps)))
    part = p[:, :LANES]
    for j in range(1, reps):
        part = part + p[:, j * LANES:(j + 1) * LANES]
    l_ref[h] = alpha * l_ref[h] + part
    acc_ref[h] = alpha * acc_ref[h] + _dot(p.astype(BF16), v_bf)
    m_ref[h] = m_new


def _flash_result(l_ref, acc_ref, h):
    return acc_ref[h] / jnp.sum(l_ref[h], axis=-1, keepdims=True)


def _lambda_value(lq1_ref, lk1_ref, lq2_ref, lk2_ref, lam_init):
    a = jnp.sum(lq1_ref[...] * lk1_ref[...], axis=-1, keepdims=True)
    b = jnp.sum(lq2_ref[...] * lk2_ref[...], axis=-1, keepdims=True)
    return jnp.exp(a) - jnp.exp(b) + lam_init


def _sub_norm(o, g, lam_init):
    ms = jnp.mean(o * o, axis=-1, keepdims=True)
    return o * lax.rsqrt(ms + EPS) * g * (1.0 - lam_init)


def _head_rows(ref, h, n):
    return ref[pl.ds(h, n, stride=N_HEADS), :].astype(BF16)


def _causal_mask(s, qi, ki, tq, tk):
    rows = lax.broadcasted_iota(jnp.int32, s.shape, 0)
    qpos = qi * tq + jnp.where(rows >= tq, rows - tq, rows)
    kpos = ki * tk + lax.broadcasted_iota(jnp.int32, s.shape, 1)
    return jnp.where(kpos <= qpos, s, NEG)


def _diff_attn_kernel(lq1_ref, lk1_ref, lq2_ref, lk2_ref, gsub_ref, q_ref, k_ref, v_ref, o_ref,
                      qs_ref, m_ref, l_ref, acc_ref, *, lam_init):
    qi, ki = pl.program_id(1), pl.program_id(2)
    tq, tk = q_ref.shape[0], k_ref.shape[0] // N_HEADS
    last = (qi * tq + tq - 1) // tk

    @pl.when(ki == 0)
    def _():
        lane = _lane_iota((tq, LANES))
        scale = DIFF_HALF ** -0.5
        for h in range(N_HEADS):
            qh = q_ref[:, h * HEAD_DIM:(h + 1) * HEAD_DIM] * scale
            qs_ref[h, :tq, :] = jnp.where(lane < DIFF_HALF, qh, 0.0).astype(BF16)
            qs_ref[h, tq:, :] = jnp.where(lane < DIFF_HALF, 0.0, qh).astype(BF16)
        m_ref[...] = jnp.full(m_ref.shape, NEG, F32)
        l_ref[...] = jnp.zeros(l_ref.shape, F32)
        acc_ref[...] = jnp.zeros(acc_ref.shape, F32)

    def step(masked):
        for h in range(N_HEADS):
            sl = slice(h * HEAD_DIM, (h + 1) * HEAD_DIM)
            s = _dot_nt(qs_ref[h], _head_rows(k_ref, h, tk))
            if masked:
                s = _causal_mask(s, qi, ki, tq, tk)
            _flash_update(s, None, _head_rows(v_ref, h, tk), m_ref, l_ref, acc_ref, h)

    @pl.when(ki < last)
    def _():
        step(False)

    @pl.when(ki == last)
    def _():
        step(True)
        lam = _lambda_value(lq1_ref, lk1_ref, lq2_ref, lk2_ref, lam_init)
        for h in range(N_HEADS):
            o = _flash_result(l_ref, acc_ref, h)
            o = o[:tq] - lam * o[tq:]
            o_ref[:, h * HEAD_DIM:(h + 1) * HEAD_DIM] = _sub_norm(o, gsub_ref[...], lam_init)


def _fox_attn_kernel(q_ref, k_ref, v_ref, cq_ref, ckt_ref, o_ref, qs_ref, cqr_ref, m_ref, l_ref, acc_ref):
    qi, ki = pl.program_id(1), pl.program_id(2)
    tq, tk = q_ref.shape[0], k_ref.shape[0] // N_HEADS
    last = (qi * tq + tq - 1) // tk
    scale = HEAD_DIM ** -0.5

    @pl.when(ki == 0)
    def _():
        qs_ref[...] = (q_ref[...] * scale).astype(BF16)
        for h in range(N_HEADS):
            cqr_ref[h] = jnp.broadcast_to(cq_ref[:, h:h + 1], (tq, LANES))
        m_ref[...] = jnp.full(m_ref.shape, NEG, F32)
        l_ref[...] = jnp.zeros(l_ref.shape, F32)
        acc_ref[...] = jnp.zeros(acc_ref.shape, F32)

    def step(masked):
        for h in range(N_HEADS):
            sl = slice(h * HEAD_DIM, (h + 1) * HEAD_DIM)
            s = _dot_nt(qs_ref[:, sl], _head_rows(k_ref, h, tk)) - ckt_ref[0, h:h + 1, :]
            if masked:
                s = _causal_mask(s, qi, ki, tq, tk)
            _flash_update(s, cqr_ref[h], _head_rows(v_ref, h, tk), m_ref, l_ref, acc_ref, h)

    @pl.when(ki < last)
    def _():
        step(False)

    @pl.when(ki == last)
    def _():
        step(True)
        for h in range(N_HEADS):
            o_ref[:, h * HEAD_DIM:(h + 1) * HEAD_DIM] = _flash_result(l_ref, acc_ref, h)


def _attn_specs(batch, seq, tq, tk):
    nq, nk = seq // tq, seq // tk
    q_spec = pl.BlockSpec((tq, WIDTH), lambda b, qi, ki: (b * nq + qi, 0))
    kv_map = lambda b, qi, ki: (b * nk + jnp.minimum(ki, (qi * tq + tq - 1) // tk), 0)
    kv_spec = pl.BlockSpec((tk * N_HEADS, HEAD_DIM), kv_map)
    return nq, nk, q_spec, kv_spec


def _diff_attention(lams, g_sub, q_arr, q_col, k, v, *, batch, seq, tq, tk, lam_init):
    nq, nk, _, kv_spec = _attn_specs(batch, seq, tq, tk)
    q_spec = pl.BlockSpec((tq, WIDTH), lambda b, qi, ki: (b * nq + qi, q_col))
    o_spec = pl.BlockSpec((tq, WIDTH), lambda b, qi, ki: (b * nq + qi, 0))
    small = lambda n: pl.BlockSpec((1, n), lambda b, qi, ki: (0, 0))
    return pl.pallas_call(
        functools.partial(_diff_attn_kernel, lam_init=lam_init),
        grid=(batch, nq, nk),
        in_specs=[small(DIFF_HALF)] * 4 + [small(LANES), q_spec, kv_spec, kv_spec],
        out_specs=o_spec,
        out_shape=jax.ShapeDtypeStruct((batch * seq, WIDTH), F32),
        scratch_shapes=[
            pltpu.VMEM((N_HEADS, 2 * tq, HEAD_DIM), BF16),
            pltpu.VMEM((N_HEADS, 2 * tq, LANES), F32),
            pltpu.VMEM((N_HEADS, 2 * tq, LANES), F32),
            pltpu.VMEM((N_HEADS, 2 * tq, HEAD_DIM), F32),
        ],
        compiler_params=_params(("arbitrary", "arbitrary", "arbitrary"), 48),
        name="diff_attention",
    )(*lams, g_sub, q_arr, k, v)


def _fox_attention(q_arr, q_col, k, v, cum, cumt, *, batch, seq, tq, tk):
    nq, nk, _, kv_spec = _attn_specs(batch, seq, tq, tk)
    q_spec = pl.BlockSpec((tq, WIDTH), lambda b, qi, ki: (b * nq + qi, q_col))
    o_spec = pl.BlockSpec((tq, WIDTH), lambda b, qi, ki: (b * nq + qi, 0))
    cq_spec = pl.BlockSpec((tq, N_HEADS), lambda b, qi, ki: (b * nq + qi, 0))
    ck_spec = pl.BlockSpec((1, N_HEADS, tk), lambda b, qi, ki: (b, 0, jnp.minimum(ki, (qi * tq + tq - 1) // tk)))
    return pl.pallas_call(
        _fox_attn_kernel,
        grid=(batch, nq, nk),
        in_specs=[q_spec, kv_spec, kv_spec, cq_spec, ck_spec],
        out_specs=o_spec,
        out_shape=jax.ShapeDtypeStruct((batch * seq, WIDTH), F32),
        scratch_shapes=[
            pltpu.VMEM((tq, WIDTH), BF16),
            pltpu.VMEM((N_HEADS, tq, LANES), F32),
            pltpu.VMEM((N_HEADS, tq, LANES), F32),
            pltpu.VMEM((N_HEADS, tq, LANES), F32),
            pltpu.VMEM((N_HEADS, tq, HEAD_DIM), F32),
        ],
        compiler_params=_params(("arbitrary", "arbitrary", "arbitrary"), 48),
        name="fox_attention",
    )(q_arr, k, v, cum, cumt)


ROWS = 2 * N_HEADS
PAGE_COLS = PAGE_SIZE * N_HEADS


def _split3(a):
    hi = a.astype(BF16)
    r1 = a - hi.astype(F32)
    mid = r1.astype(BF16)
    lo = (r1 - mid.astype(F32)).astype(BF16)
    return hi, mid, lo


def _stack2(a):
    return jnp.concatenate([a, a], axis=0)


def _decode_kernel(pt_ref, lq1_ref, lk1_ref, lq2_ref, lk2_ref, gsub_ref,
                   qd_ref, qf_ref, kdn_ref, vdn_ref, kfn_ref, vfn_ref, lfn_ref, *refs, lam_init, group):
    del pt_ref
    kd_refs, vd_refs, kf_refs, vf_refs, lf_refs = (refs[i * group:(i + 1) * group] for i in range(5))
    od_ref, of_ref = refs[5 * group:5 * group + 2]
    (qsd_ref, qsf_ref, md_ref, ld_ref, accd_ref, mf_ref, lf_acc_ref, accf_ref, carry_ref,
     later_ref) = refs[5 * group + 2:]
    b, p = pl.program_id(0), pl.program_id(1)
    n_steps = pl.num_programs(1)
    scale_f = HEAD_DIM ** -0.5

    @pl.when((b == 0) & (p == 0))
    def _():
        ks = lax.broadcasted_iota(jnp.int32, (PAGE_SIZE, PAGE_COLS), 0)
        kc = lax.broadcasted_iota(jnp.int32, (PAGE_SIZE, PAGE_COLS), 1)
        later_ref[...] = (ks > (kc >> 3)).astype(BF16)

    @pl.when(p == 0)
    def _():
        lane = _lane_iota((N_HEADS, HEAD_DIM))
        q = qd_ref[0] * (DIFF_HALF ** -0.5)
        qd = jnp.concatenate([jnp.where(lane < DIFF_HALF, q, 0.0), jnp.where(lane < DIFF_HALF, 0.0, q)], axis=0)
        qf = jnp.concatenate([qf_ref[0] * scale_f, jnp.zeros((N_HEADS, HEAD_DIM), F32)], axis=0)
        qsd_ref[...] = qd.astype(BF16)
        qsf_ref[...] = qf.astype(BF16)
        md_ref[...] = jnp.sum(qd * _stack2(kdn_ref[0]), axis=-1, keepdims=True)
        mf_ref[...] = jnp.sum(qf * _stack2(kfn_ref[0]), axis=-1, keepdims=True)
        ld_ref[...] = jnp.ones(ld_ref.shape, F32)
        lf_acc_ref[...] = jnp.ones(lf_acc_ref.shape, F32)
        accd_ref[...] = _stack2(vdn_ref[0])
        accf_ref[...] = _stack2(vfn_ref[0])
        carry_ref[...] = _stack2(lfn_ref[0])

    r = lax.broadcasted_iota(jnp.int32, (ROWS, group * PAGE_COLS), 0)
    c = lax.broadcasted_iota(jnp.int32, (ROWS, group * PAGE_COLS), 1)
    own_head = (r & (N_HEADS - 1)) == (c & (N_HEADS - 1))
    flat = lambda ref: ref[...].reshape(PAGE_COLS, HEAD_DIM).astype(BF16)

    def logits(q_ref, k_refs):
        return jnp.concatenate([_dot_nt(q_ref[...], flat(k)) for k in k_refs], axis=1)

    def update(s, v_refs, m_ref, l_ref, acc_ref):
        s = jnp.where(own_head, s, NEG)
        m_prev = m_ref[...]
        m_new = jnp.maximum(m_prev, jnp.max(s, axis=-1, keepdims=True))
        alpha = jnp.exp(m_prev - m_new)
        pr = jnp.exp(s - m_new)
        l_ref[...] = alpha * l_ref[...] + jnp.sum(pr, axis=-1, keepdims=True)
        pr = pr.astype(BF16)
        acc = alpha * acc_ref[...]
        for g, v in enumerate(v_refs):
            acc = acc + _dot(pr[:, g * PAGE_COLS:(g + 1) * PAGE_COLS], flat(v))
        acc_ref[...] = acc
        m_ref[...] = m_new

    update(logits(qsd_ref, kd_refs), vd_refs, md_ref, ld_ref, accd_ref)

    carry = carry_ref[...]
    biases = []
    for lf in lf_refs:
        page_lf = _stack2(lf[...])
        bias = carry
        for term in _split3(page_lf):
            bias = bias + _dot(term, later_ref[...])
        biases.append(bias)
        carry = carry + jnp.sum(page_lf, axis=-1, keepdims=True)
    carry_ref[...] = carry
    update(logits(qsf_ref, kf_refs) + jnp.concatenate(biases, axis=1), vf_refs, mf_ref, lf_acc_ref, accf_ref)

    @pl.when(p == n_steps - 1)
    def _():
        lam = _lambda_value(lq1_ref, lk1_ref, lq2_ref, lk2_ref, lam_init)
        o_d = accd_ref[...] / ld_ref[...]
        o = o_d[:N_HEADS] - lam * o_d[N_HEADS:]
        od_ref[0] = _sub_norm(o, gsub_ref[...], lam_init)
        of_ref[0] = (accf_ref[...] / lf_acc_ref[...])[:N_HEADS]


def _decode(page_table, lams, g_sub, qd, qf, kd_new, vd_new, kf_new, vf_new, lf_new,
            cache_kd, cache_vd, cache_kf, cache_vf, cache_lf, *, lam_init, group):
    n_samples, n_pages = page_table.shape
    assert n_pages % group == 0
    pt_flat = page_table.reshape(-1)
    tile3 = lambda a: a.reshape(n_samples, N_HEADS, HEAD_DIM)
    tile_spec = pl.BlockSpec((1, N_HEADS, HEAD_DIM), lambda b, p, pt: (b, 0, 0))
    small = lambda n: pl.BlockSpec((1, n), lambda b, p, pt: (0, 0))

    def page_of(g):
        return lambda b, p, pt: pt[b * n_pages + n_pages - 1 - (p * group + g)]

    def page_specs():
        return [pl.BlockSpec((None, PAGE_SIZE, N_HEADS, HEAD_DIM),
                             lambda b, p, pt, f=page_of(g): (f(b, p, pt), 0, 0, 0)) for g in range(group)]

    lf_specs = [pl.BlockSpec((None, N_HEADS, PAGE_SIZE), lambda b, p, pt, f=page_of(g): (f(b, p, pt), 0, 0))
                for g in range(group)]
    grid_spec = pltpu.PrefetchScalarGridSpec(
        num_scalar_prefetch=1,
        grid=(n_samples, n_pages // group),
        in_specs=[small(DIFF_HALF)] * 4 + [small(LANES)] + [tile_spec] * 6
        + [pl.BlockSpec((1, N_HEADS, 1), lambda b, p, pt: (b, 0, 0))]
        + page_specs() + page_specs() + page_specs() + page_specs() + lf_specs,
        out_specs=(tile_spec, tile_spec),
        scratch_shapes=[
            pltpu.VMEM((ROWS, HEAD_DIM), BF16),
            pltpu.VMEM((ROWS, HEAD_DIM), BF16),
            pltpu.VMEM((ROWS, 1), F32),
            pltpu.VMEM((ROWS, 1), F32),
            pltpu.VMEM((ROWS, HEAD_DIM), F32),
            pltpu.VMEM((ROWS, 1), F32),
            pltpu.VMEM((ROWS, 1), F32),
            pltpu.VMEM((ROWS, HEAD_DIM), F32),
            pltpu.VMEM((ROWS, 1), F32),
            pltpu.VMEM((PAGE_SIZE, PAGE_COLS), BF16),
        ],
    )
    cache_lf_t = jnp.swapaxes(cache_lf, 1, 2)
    od, of = pl.pallas_call(
        functools.partial(_decode_kernel, lam_init=lam_init, group=group),
        grid_spec=grid_spec,
        out_shape=(
            jax.ShapeDtypeStruct((n_samples, N_HEADS, HEAD_DIM), F32),
            jax.ShapeDtypeStruct((n_samples, N_HEADS, HEAD_DIM), F32),
        ),
        compiler_params=_params(("arbitrary", "arbitrary"), 52),
        name="decode",
    )(pt_flat, *lams, g_sub, tile3(qd), tile3(qf), tile3(kd_new), tile3(vd_new), tile3(kf_new), tile3(vf_new),
      lf_new.reshape(n_samples, N_HEADS, 1), *([cache_kd] * group), *([cache_vd] * group), *([cache_kf] * group),
      *([cache_vf] * group), *([cache_lf_t] * group))
    return od.reshape(n_samples, WIDTH), of.reshape(n_samples, WIDTH)


def _merge_kernel(od_ref, of_ref, sgd_ref, sgf_ref, x_ref, wbd_ref, wbf_ref, wo_ref, h_ref, mg_ref, *, n_col):
    j = pl.program_id(1)

    @pl.when(j < n_col)
    def _():
        a = _dot(od_ref[...].astype(BF16), wbd_ref[...])
        b = _dot(of_ref[...].astype(BF16), wbf_ref[...])
        mg_ref[j] = (sgd_ref[...] * a + sgf_ref[...] * b).astype(BF16)

    @pl.when(j >= n_col)
    def _():
        merged = jnp.concatenate([mg_ref[c] for c in range(n_col)], axis=1)
        h_ref[...] = x_ref[...] + _dot(merged, wo_ref[...])


def _merge(od, of, internal, x, w_bd, w_bf, w_o, *, tm, tn):
    m, d = x.shape
    n_col = d // tn
    gate_d = 2 * WIDTH // tn
    first = lambda i, j: jnp.minimum(j, n_col - 1)
    second = lambda i, j: (i, jnp.maximum(j - n_col, 0))
    return pl.pallas_call(
        functools.partial(_merge_kernel, n_col=n_col),
        grid=(m // tm, 2 * n_col),
        in_specs=[
            pl.BlockSpec((tm, WIDTH), lambda i, j: (i, 0)),
            pl.BlockSpec((tm, WIDTH), lambda i, j: (i, 0)),
            pl.BlockSpec((tm, tn), lambda i, j: (i, gate_d + first(i, j))),
            pl.BlockSpec((tm, tn), lambda i, j: (i, gate_d + n_col + first(i, j))),
            pl.BlockSpec((tm, tn), second),
            pl.BlockSpec((WIDTH, tn), lambda i, j: (0, first(i, j))),
            pl.BlockSpec((WIDTH, tn), lambda i, j: (0, first(i, j))),
            pl.BlockSpec((d, tn), lambda i, j: (0, jnp.maximum(j - n_col, 0))),
        ],
        out_specs=pl.BlockSpec((tm, tn), second),
        out_shape=jax.ShapeDtypeStruct((m, d), F32),
        scratch_shapes=[pltpu.VMEM((n_col, tm, tn), BF16)],
        compiler_params=_params(("arbitrary", "arbitrary"), 48),
        name="merge",
    )(od, of, internal, internal, x, w_bd, w_bf, w_o)


def _ffn_kernel(h_ref, g_ref, wg_ref, wu_ref, wd_ref, o_ref, hn_ref):
    @pl.when(pl.program_id(1) == 0)
    def _():
        h = h_ref[...]
        ms = jnp.mean(h * h, axis=-1, keepdims=True)
        hn_ref[...] = (h * lax.rsqrt(ms + EPS) * g_ref[...]).astype(BF16)
        o_ref[...] = h

    hn = hn_ref[...]
    a = _dot(hn, wg_ref[...])
    u = _dot(hn, wu_ref[...])
    ff = (a * _sigmoid(a) * u).astype(BF16)
    o_ref[...] += _dot(ff, wd_ref[...])


def _ffn(h, g, w_gate, w_up, w_down, *, tm, tf):
    m, d = h.shape
    f = w_gate.shape[1]
    return pl.pallas_call(
        _ffn_kernel,
        grid=(m // tm, f // tf),
        in_specs=[
            pl.BlockSpec((tm, d), lambda i, j: (i, 0)),
            pl.BlockSpec((1, d), lambda i, j: (0, 0)),
            pl.BlockSpec((d, tf), lambda i, j: (0, j)),
            pl.BlockSpec((d, tf), lambda i, j: (0, j)),
            pl.BlockSpec((tf, d), lambda i, j: (j, 0)),
        ],
        out_specs=pl.BlockSpec((tm, d), lambda i, j: (i, 0)),
        out_shape=jax.ShapeDtypeStruct((m, d), F32),
        scratch_shapes=[pltpu.VMEM((tm, d), BF16)],
        compiler_params=_params(("arbitrary", "arbitrary"), 48),
        name="ffn",
    )(h, g, w_gate, w_up, w_down)


def _rope_tables(pos):
    half = DIFF_HALF // 2
    inv = ROPE_THETA ** (-jnp.arange(half, dtype=F32) / half)
    ang = pos.astype(F32)[:, None] * inv[None, :]
    cos = jnp.concatenate([jnp.cos(ang)] * 4, axis=-1)
    sin = jnp.sin(ang)
    sin_signed = jnp.concatenate([-sin, sin, -sin, sin], axis=-1)
    return cos, sin_signed


def _tile(m, pref):
    return pref if m % pref == 0 else m


def _layer(l, xp, xs, caches, page_table, weights):
    (g_norm_attn, w_in, b_f, g_q_diff, g_k_diff, g_q_fox, g_k_fox, lambda_q1, lambda_k1, lambda_q2, lambda_k2,
     g_sub, w_branch_diff, w_branch_fox, w_o, g_norm_ffn, w_ffn_gate, w_ffn_up, w_ffn_down) = [w[l] for w in weights]
    batch, seq, d = xp.shape
    n_samples, dec_seq, _ = xs.shape
    past_len = page_table.shape[1] * PAGE_SIZE
    lam_init = 0.8 - 0.6 * math.exp(-0.3 * l)

    o = [0]
    for wdt in (WIDTH, WIDTH, WIDTH, WIDTH, WIDTH, WIDTH, N_HEADS, d, d):
        o.append(o[-1] + wdt)
    col = lambda a, b: w_in[:, o[a]:o[b]]
    w_main = jnp.concatenate([col(0, 1), col(3, 4), col(7, 9), col(1, 3), col(4, 6)], axis=1).astype(BF16)
    w_f = jnp.pad(col(6, 7), ((0, 0), (0, LANES - N_HEADS))).astype(BF16)
    b_f_pad = jnp.pad(b_f, (0, LANES - N_HEADS)).reshape(1, LANES)
    two = lambda g: jnp.concatenate([g, g]).reshape(1, LANES)
    one = lambda g: g.reshape(1, -1)
    lams = tuple(one(v) for v in (lambda_q1, lambda_k1, lambda_q2, lambda_k2))
    w_bd, w_bf, w_out = w_branch_diff.astype(BF16), w_branch_fox.astype(BF16), w_o.astype(BF16)
    w_g, w_u, w_d = w_ffn_gate.astype(BF16), w_ffn_up.astype(BF16), w_ffn_down.astype(BF16)

    def project(x2d, pos, tm, rows_per_table):
        cos, sin_signed = _rope_tables(pos)
        return _proj(x2d, one(g_norm_attn), w_main, w_f, b_f_pad, two(g_q_diff), two(g_k_diff), one(g_q_fox),
                     one(g_k_fox), cos, sin_signed, tm=tm, rows_per_table=rows_per_table)

    def finish(x2d, od, of, internal, tm_merge, tm_ffn):
        h = _merge(od, of, internal, x2d, w_bd, w_bf, w_out, tm=tm_merge, tn=512)
        return _ffn(h, one(g_norm_ffn), w_g, w_u, w_d, tm=tm_ffn, tf=512)

    xp2 = xp.reshape(batch * seq, d)
    tm_p = _tile(seq, 512)
    int_p, kd_p, vd_p, kf_p, vf_p, lf_p = project(xp2, jnp.arange(seq), tm_p, seq)
    cum, cumt = _cumsum(lf_p, batch, seq)
    tq, tk = _tile(seq, 256), _tile(seq, 512)
    od_p = _diff_attention(lams, one(g_sub), int_p, 0, kd_p, vd_p, batch=batch, seq=seq, tq=tq, tk=tk,
                           lam_init=lam_init)
    of_p = _fox_attention(int_p, 1, kf_p, vf_p, cum, cumt, batch=batch, seq=seq, tq=tq, tk=tk)
    yp = finish(xp2, od_p, of_p, int_p, _tile(seq, 1024), _tile(seq, 512))

    assert dec_seq == 1
    xs2 = xs.reshape(n_samples, d)
    pos_s = jnp.full((n_samples,), past_len, jnp.int32)
    int_s, kd_s, vd_s, kf_s, vf_s, lf_s = project(xs2, pos_s, n_samples, n_samples)
    od_s, of_s = _decode(page_table, lams, one(g_sub), int_s[:, :WIDTH], int_s[:, WIDTH:2 * WIDTH],
                         kd_s, vd_s, kf_s, vf_s, lf_s, *[c[l] for c in caches], lam_init=lam_init,
                         group=math.gcd(8, page_table.shape[1]))
    ys = finish(xs2, od_s, of_s, int_s, n_samples, n_samples)

    heads = lambda a, b, t: a.reshape(b, t, N_HEADS, HEAD_DIM)
    new_p = (heads(kd_p, batch, seq), heads(vd_p, batch, seq), heads(kf_p, batch, seq), heads(vf_p, batch, seq),
             lf_p.reshape(batch, seq, N_HEADS))
    new_s = (heads(kd_s, n_samples, 1), heads(vd_s, n_samples, 1), heads(kf_s, n_samples, 1),
             heads(vf_s, n_samples, 1), lf_s.reshape(n_samples, 1, N_HEADS))
    return yp.reshape(batch, seq, d), ys.reshape(n_samples, dec_seq, d), new_p, new_s


def kernel(x_prompt, x_sample, cache_k_diff, cache_v_diff, cache_k_fox, cache_v_fox, cache_logf_fox, page_table,
           g_norm_attn, w_in, b_f, g_q_diff, g_k_diff, g_q_fox, g_k_fox, lambda_q1, lambda_k1, lambda_q2, lambda_k2,
           g_sub, w_branch_diff, w_branch_fox, w_o, g_norm_ffn, w_ffn_gate, w_ffn_up, w_ffn_down):
    weights = (g_norm_attn, w_in, b_f, g_q_diff, g_k_diff, g_q_fox, g_k_fox, lambda_q1, lambda_k1, lambda_q2,
               lambda_k2, g_sub, w_branch_diff, w_branch_fox, w_o, g_norm_ffn, w_ffn_gate, w_ffn_up, w_ffn_down)
    caches = (cache_k_diff, cache_v_diff, cache_k_fox, cache_v_fox, cache_logf_fox)
    depth = w_in.shape[0]
    xp, xs = x_prompt, x_sample
    new_p, new_s = [], []
    for l in range(depth):
        xp, xs, np_l, ns_l = _layer(l, xp, xs, caches, page_table, weights)
        new_p.append(np_l)
        new_s.append(ns_l)
    stack = lambda lst, i: jnp.stack([t[i] for t in lst], axis=0)
    return (xp, xs) + tuple(stack(new_p, i) for i in range(5)) + tuple(stack(new_s, i) for i in range(5))
```

```python
import functools
import math

import jax
import jax.numpy as jnp
from jax import lax
from jax.experimental import pallas as pl
from jax.experimental.pallas import tpu as pltpu

N_HEADS = 8
HEAD_DIM = 128
DIFF_HALF = HEAD_DIM // 2
WIDTH = N_HEADS * HEAD_DIM
ROPE_THETA = 10000.0
EPS = 1e-6
PAGE_SIZE = 128
LANES = 128
NEG = -1e30
MIB = 1024 * 1024

F32 = jnp.float32
BF16 = jnp.bfloat16
HIGHEST = lax.Precision.HIGHEST


def _params(semantics, vmem_mib):
    return pltpu.CompilerParams(dimension_semantics=semantics, vmem_limit_bytes=vmem_mib * MIB)


def _dot(a, b):
    return jnp.dot(a, b, preferred_element_type=F32)


def _dot_nt(a, b, precision=None):
    return lax.dot_general(a, b, (((1,), (1,)), ((), ())), preferred_element_type=F32, precision=precision)


def _lane_iota(shape):
    return lax.broadcasted_iota(jnp.int32, shape, len(shape) - 1)


def _rms_rope_head(a, g, cos, sin_signed, lane):
    sq = a * a
    lo = lane < DIFF_HALF
    s_lo = jnp.sum(jnp.where(lo, sq, 0.0), axis=-1, keepdims=True)
    s_hi = jnp.sum(jnp.where(lo, 0.0, sq), axis=-1, keepdims=True)
    ms = jnp.where(lo, s_lo, s_hi) * (1.0 / DIFF_HALF)
    y = a * lax.rsqrt(ms + EPS) * g
    first = (lane & (DIFF_HALF - 1)) < (DIFF_HALF // 2)
    rot = jnp.where(first, pltpu.roll(y, LANES - DIFF_HALF // 2, 1), pltpu.roll(y, DIFF_HALF // 2, 1))
    return y * cos + rot * sin_signed


def _rms_head(a, g):
    ms = jnp.mean(a * a, axis=-1, keepdims=True)
    return a * lax.rsqrt(ms + EPS) * g


def _log_sigmoid(z):
    return -(jnp.maximum(-z, 0.0) + jnp.log1p(jnp.exp(-jnp.abs(z))))


def _sigmoid(z):
    return 1.0 / (1.0 + jnp.exp(-z))


def _proj_kernel(x_ref, gn_ref, w_ref, wf_ref, bf_ref, gqd_ref, gkd_ref, gqf_ref, gkf_ref, cos_ref, sin_ref,
                 int_ref, kd_ref, vd_ref, kf_ref, vf_ref, logf_ref, xn_ref):
    j = pl.program_id(1)
    tm = x_ref.shape[0]

    @pl.when(j == 0)
    def _():
        x = x_ref[...]
        ms = jnp.mean(x * x, axis=-1, keepdims=True)
        xn_ref[...] = (x * lax.rsqrt(ms + EPS) * gn_ref[...]).astype(BF16)
        z = _dot(xn_ref[...], wf_ref[...]) + bf_ref[...]
        logf_ref[...] = _log_sigmoid(z)[:, :N_HEADS]

    lane = _lane_iota((tm, LANES))
    pair = 2 * HEAD_DIM

    def head_pairs():
        for c in range(WIDTH // pair):
            acc = _dot(xn_ref[...], w_ref[:, c * pair:(c + 1) * pair])
            for k in range(2):
                yield 2 * c + k, acc[:, k * HEAD_DIM:(k + 1) * HEAD_DIM]

    def heads(fn, out_ref):
        for h, a in head_pairs():
            out_ref[:, h * HEAD_DIM:(h + 1) * HEAD_DIM] = fn(a)

    def heads_token_major(fn, out_ref):
        for h, a in head_pairs():
            out_ref[pl.ds(h, tm, stride=N_HEADS), :] = fn(a)

    @pl.when(j == 0)
    def _():
        heads(lambda a: _rms_rope_head(a, gqd_ref[...], cos_ref[...], sin_ref[...], lane), int_ref)

    @pl.when(j == 1)
    def _():
        heads(lambda a: _rms_head(a, gqf_ref[...]), int_ref)

    @pl.when((j >= 2) & (j < 6))
    def _():
        heads(_sigmoid, int_ref)

    @pl.when(j == 6)
    def _():
        heads_token_major(lambda a: _rms_rope_head(a, gkd_ref[...], cos_ref[...], sin_ref[...], lane), kd_ref)

    @pl.when(j == 7)
    def _():
        heads_token_major(lambda a: a, vd_ref)

    @pl.when(j == 8)
    def _():
        heads_token_major(lambda a: _rms_head(a, gkf_ref[...]), kf_ref)

    @pl.when(j == 9)
    def _():
        heads_token_major(lambda a: a, vf_ref)


def _proj(x, gn, w_main, w_f, b_f, gqd, gkd, gqf, gkf, cos, sin_signed, *, tm, rows_per_table):
    m, d = x.shape
    n_int = 6
    n_tab = rows_per_table // tm
    row = lambda i, j: (i, 0)
    const = lambda i, j: (0, 0)
    kv_spec = pl.BlockSpec((tm * N_HEADS, HEAD_DIM), row)
    kv_shape = jax.ShapeDtypeStruct((m * N_HEADS, HEAD_DIM), F32)
    out_shape = (
        jax.ShapeDtypeStruct((m, n_int * WIDTH), F32),
        kv_shape, kv_shape, kv_shape, kv_shape,
        jax.ShapeDtypeStruct((m, N_HEADS), F32),
    )
    return pl.pallas_call(
        _proj_kernel,
        grid=(m // tm, n_int + 4),
        in_specs=[
            pl.BlockSpec((tm, d), row),
            pl.BlockSpec((1, d), const),
            pl.BlockSpec((d, WIDTH), lambda i, j: (0, j)),
            pl.BlockSpec((d, LANES), const),
            pl.BlockSpec((1, LANES), const),
            pl.BlockSpec((1, LANES), const),
            pl.BlockSpec((1, LANES), const),
            pl.BlockSpec((1, LANES), const),
            pl.BlockSpec((1, LANES), const),
            pl.BlockSpec((tm, LANES), lambda i, j: (i % n_tab, 0)),
            pl.BlockSpec((tm, LANES), lambda i, j: (i % n_tab, 0)),
        ],
        out_specs=(
            pl.BlockSpec((tm, WIDTH), lambda i, j: (i, jnp.minimum(j, n_int - 1))),
            kv_spec, kv_spec, kv_spec, kv_spec,
            pl.BlockSpec((tm, N_HEADS), row),
        ),
        out_shape=out_shape,
        scratch_shapes=[pltpu.VMEM((tm, d), BF16)],
        compiler_params=_params(("arbitrary", "arbitrary"), 48),
        name="proj",
    )(x, gn, w_main, w_f, b_f, gqd, gkd, gqf, gkf, cos, sin_signed)


def _pad_lanes(chunk, lane):
    out = jnp.zeros(lane.shape, F32)
    for h in range(N_HEADS):
        out = jnp.where(lane == h, chunk[:, h:h + 1], out)
    return out


def _cumsum_kernel(lf_ref, cum_ref, cumt_ref):
    s = lf_ref.shape[0]
    r = lax.broadcasted_iota(jnp.int32, (LANES, LANES), 0)
    c = lax.broadcasted_iota(jnp.int32, (LANES, LANES), 1)
    tri = (c <= r).astype(F32)
    carry = jnp.zeros((1, LANES), F32)
    for ci in range(s // LANES):
        rows = slice(ci * LANES, (ci + 1) * LANES)
        pad = _pad_lanes(lf_ref[rows, :], c)
        res = jnp.dot(tri, pad, preferred_element_type=F32, precision=HIGHEST) + carry
        cum_ref[rows, :] = res[:, :N_HEADS]
        cumt_ref[0, :, rows] = res.T[:N_HEADS, :]
        carry = res[LANES - 1:LANES, :]


def _cumsum(logf, batch, seq):
    return pl.pallas_call(
        _cumsum_kernel,
        grid=(batch,),
        in_specs=[pl.BlockSpec((seq, N_HEADS), lambda b: (b, 0))],
        out_specs=(
            pl.BlockSpec((seq, N_HEADS), lambda b: (b, 0)),
            pl.BlockSpec((1, N_HEADS, seq), lambda b: (b, 0, 0)),
        ),
        out_shape=(
            jax.ShapeDtypeStruct((batch * seq, N_HEADS), F32),
            jax.ShapeDtypeStruct((batch, N_HEADS, seq), F32),
        ),
        compiler_params=_params(("arbitrary",), 32),
        name="cumsum",
    )(logf)


ROWS = 2 * N_HEADS
PAGE_COLS = PAGE_SIZE * N_HEADS
N_DEC_SMALL = 7


def _split3(a):
    hi = a.astype(BF16)
    r1 = a - hi.astype(F32)
    mid = r1.astype(BF16)
    lo = (r1 - mid.astype(F32)).astype(BF16)
    return hi, mid, lo


def _stack2(a):
    return jnp.concatenate([a, a], axis=0)


def _lambda_value(lq1_ref, lk1_ref, lq2_ref, lk2_ref, lam_init):
    a = jnp.sum(lq1_ref[...] * lk1_ref[...], axis=-1, keepdims=True)
    b = jnp.sum(lq2_ref[...] * lk2_ref[...], axis=-1, keepdims=True)
    return jnp.exp(a) - jnp.exp(b) + lam_init


def _sub_norm(o, g, lam_init):
    ms = jnp.mean(o * o, axis=-1, keepdims=True)
    return o * lax.rsqrt(ms + EPS) * g * (1.0 - lam_init)


def _decode_step(p, n_steps, very_first, lam_refs, gsub_ref, small_refs, page_refs, out_refs, scratch_refs,
                 *, lam_init, group):
    qd_ref, qf_ref, kdn_ref, vdn_ref, kfn_ref, vfn_ref, lfn_ref = small_refs
    kd_refs, vd_refs, kf_refs, vf_refs, lf_refs = (page_refs[i * group:(i + 1) * group] for i in range(5))
    od_ref, of_ref = out_refs
    qs_ref, m_ref, l_ref, acc_ref, carry_ref, later_ref, own_ref = scratch_refs
    cols = group * PAGE_COLS

    @pl.when(very_first)
    def _():
        ks = lax.broadcasted_iota(jnp.int32, (PAGE_SIZE, PAGE_COLS), 0)
        kc = lax.broadcasted_iota(jnp.int32, (PAGE_SIZE, PAGE_COLS), 1)
        later_ref[...] = (ks > (kc >> 3)).astype(BF16)
        r = lax.broadcasted_iota(jnp.int32, (2 * ROWS, cols), 0)
        c = lax.broadcasted_iota(jnp.int32, (2 * ROWS, cols), 1)
        own_ref[...] = jnp.where((r & (N_HEADS - 1)) == (c & (N_HEADS - 1)), 0.0, NEG)

    @pl.when(p == 0)
    def _():
        lane = _lane_iota((N_HEADS, HEAD_DIM))
        q = qd_ref[0] * (DIFF_HALF ** -0.5)
        zeros = jnp.zeros((N_HEADS, HEAD_DIM), F32)
        q_d = jnp.concatenate([jnp.where(lane < DIFF_HALF, q, 0.0), jnp.where(lane < DIFF_HALF, 0.0, q)], axis=0)
        q_f = jnp.concatenate([qf_ref[0] * (HEAD_DIM ** -0.5), zeros], axis=0)
        qs_ref[:ROWS, :HEAD_DIM] = q_d.astype(BF16)
        qs_ref[:ROWS, HEAD_DIM:] = jnp.zeros((ROWS, HEAD_DIM), BF16)
        qs_ref[ROWS:, :HEAD_DIM] = jnp.zeros((ROWS, HEAD_DIM), BF16)
        qs_ref[ROWS:, HEAD_DIM:] = q_f.astype(BF16)
        m_ref[:ROWS] = jnp.sum(q_d * _stack2(kdn_ref[0]), axis=-1, keepdims=True)
        m_ref[ROWS:] = jnp.sum(q_f * _stack2(kfn_ref[0]), axis=-1, keepdims=True)
        l_ref[...] = jnp.ones(l_ref.shape, F32)
        acc_ref[...] = jnp.concatenate([_stack2(vdn_ref[0]), _stack2(vfn_ref[0])], axis=0)
        carry_ref[...] = _stack2(lfn_ref[0])

    def both(d_refs, f_refs):
        flat = lambda refs: jnp.concatenate(
            [ref[...].reshape(PAGE_COLS, HEAD_DIM).astype(BF16) for ref in refs], axis=0)
        return jnp.concatenate([flat(d_refs), flat(f_refs)], axis=1)

    pages_lf = [_stack2(lf[...]) for lf in lf_refs]
    within = _dot(jnp.concatenate([t for page_lf in pages_lf for t in _split3(page_lf)], axis=0), later_ref[...])
    carry = carry_ref[...]
    biases = []
    for g, page_lf in enumerate(pages_lf):
        w = within[3 * g * ROWS:3 * (g + 1) * ROWS]
        biases.append(carry + w[:ROWS] + w[ROWS:2 * ROWS] + w[2 * ROWS:])
        carry = carry + jnp.sum(page_lf, axis=-1, keepdims=True)
    carry_ref[...] = carry

    s = _dot_nt(qs_ref[...], both(kd_refs, kf_refs))
    s = jnp.concatenate([s[:ROWS], s[ROWS:] + jnp.concatenate(biases, axis=1)], axis=0) + own_ref[...]
    m_prev = m_ref[...]
    m_new = jnp.maximum(m_prev, jnp.max(s, axis=-1, keepdims=True))
    alpha = jnp.exp(m_prev - m_new)
    pr = jnp.exp(s - m_new)
    l_ref[...] = alpha * l_ref[...] + jnp.sum(pr, axis=-1, keepdims=True)
    pv = _dot(pr.astype(BF16), both(vd_refs, vf_refs))
    pv = jnp.concatenate([pv[:ROWS, :HEAD_DIM], pv[ROWS:, HEAD_DIM:]], axis=0)
    acc_ref[...] = alpha * acc_ref[...] + pv
    m_ref[...] = m_new

    @pl.when(p == n_steps - 1)
    def _():
        lam = _lambda_value(*lam_refs, lam_init)
        o = acc_ref[...] / l_ref[...]
        od_ref[0] = _sub_norm(o[:N_HEADS] - lam * o[N_HEADS:ROWS], gsub_ref[...], lam_init)
        of_ref[0] = o[ROWS:ROWS + N_HEADS]


def _decode_scratch(group):
    return [
        pltpu.VMEM((2 * ROWS, 2 * HEAD_DIM), BF16),
        pltpu.VMEM((2 * ROWS, 1), F32),
        pltpu.VMEM((2 * ROWS, 1), F32),
        pltpu.VMEM((2 * ROWS, HEAD_DIM), F32),
        pltpu.VMEM((ROWS, 1), F32),
        pltpu.VMEM((PAGE_SIZE, PAGE_COLS), BF16),
        pltpu.VMEM((2 * ROWS, group * PAGE_COLS), F32),
    ]


class _DecodePlan:
    def __init__(self, page_table, qd, qf, kd_new, vd_new, kf_new, vf_new, lf_new,
                 cache_kd, cache_vd, cache_kf, cache_vf, cache_lf, *, group, step_of, first_step, n_steps):
        n_samples, n_pages = page_table.shape
        assert n_pages % group == 0
        spp = n_pages // group
        assert first_step % spp == 0 and n_steps % spp == 0
        self.group, self.spp, self.first_step = group, spp, first_step
        self.pt_flat = page_table.reshape(-1)
        tile3 = lambda a: a.reshape(n_samples, N_HEADS, HEAD_DIM)
        first_sample = first_step // spp
        sample = lambda *ids: step_of(*ids) // spp
        page = lambda g: (lambda *ids_pt: ids_pt[-1][
            sample(*ids_pt[:-1]) * n_pages + n_pages - 1 - ((step_of(*ids_pt[:-1]) % spp) * group + g)])
        tile_spec = pl.BlockSpec((1, N_HEADS, HEAD_DIM), lambda *a: (sample(*a[:-1]), 0, 0))
        lfn_spec = pl.BlockSpec((1, N_HEADS, 1), lambda *a: (sample(*a[:-1]), 0, 0))
        page_specs = lambda: [pl.BlockSpec((None, PAGE_SIZE, N_HEADS, HEAD_DIM),
                                           lambda *a, f=page(g): (f(*a), 0, 0, 0)) for g in range(group)]
        lf_specs = [pl.BlockSpec((None, N_HEADS, PAGE_SIZE), lambda *a, f=page(g): (f(*a), 0, 0))
                    for g in range(group)]
        cache_lf_t = jnp.swapaxes(cache_lf, 1, 2)
        self.inputs = [tile3(qd), tile3(qf), tile3(kd_new), tile3(vd_new), tile3(kf_new), tile3(vf_new),
                       lf_new.reshape(n_samples, N_HEADS, 1)] + [cache_kd] * group + [cache_vd] * group \
            + [cache_kf] * group + [cache_vf] * group + [cache_lf_t] * group
        self.in_specs = [tile_spec] * 6 + [lfn_spec] + page_specs() + page_specs() + page_specs() + page_specs() \
            + lf_specs
        out_spec = pl.BlockSpec((1, N_HEADS, HEAD_DIM), lambda *a: (sample(*a[:-1]) - first_sample, 0, 0))
        self.out_specs = [out_spec, out_spec]
        out_shape = jax.ShapeDtypeStruct((n_steps // spp, N_HEADS, HEAD_DIM), F32)
        self.out_shapes = [out_shape, out_shape]
        self.n_in = len(self.inputs)


def _decode_kernel(pt_ref, lq1_ref, lk1_ref, lq2_ref, lk2_ref, gsub_ref, *refs, lam_init, group, spp):
    del pt_ref
    n_in = N_DEC_SMALL + 5 * group
    t = pl.program_id(0)
    _decode_step(t % spp, spp, t == 0, (lq1_ref, lk1_ref, lq2_ref, lk2_ref), gsub_ref,
                 refs[:N_DEC_SMALL], refs[N_DEC_SMALL:n_in], refs[n_in:n_in + 2], refs[n_in + 2:],
                 lam_init=lam_init, group=group)


def _decode(plan, lams, g_sub, *, n_steps, lam_init):
    small = lambda n: pl.BlockSpec((1, n), lambda t, pt: (0, 0))
    grid_spec = pltpu.PrefetchScalarGridSpec(
        num_scalar_prefetch=1,
        grid=(n_steps,),
        in_specs=[small(DIFF_HALF)] * 4 + [small(LANES)] + plan.in_specs,
        out_specs=tuple(plan.out_specs),
        scratch_shapes=_decode_scratch(plan.group),
    )
    return pl.pallas_call(
        functools.partial(_decode_kernel, lam_init=lam_init, group=plan.group, spp=plan.spp),
        grid_spec=grid_spec,
        out_shape=tuple(plan.out_shapes),
        compiler_params=_params(("arbitrary",), 52),
        name="decode",
    )(plan.pt_flat, *lams, g_sub, *plan.inputs)


def _flash_update(s, cq, v_bf, m_ref, l_ref, acc_ref, h):
    reps = s.shape[1] // LANES
    m_prev = m_ref[h]
    m_curr = jnp.max(s, axis=-1, keepdims=True)
    if cq is not None:
        m_curr = m_curr + cq
    m_new = jnp.maximum(m_prev, m_curr)
    alpha = jnp.exp(m_prev - m_new)
    shift = m_new if cq is None else m_new - cq
    p = jnp.exp(s - jnp.tile(shift, (1, reps)))
    part = p[:, :LANES]
    for j in range(1, reps):
        part = part + p[:, j * LANES:(j + 1) * LANES]
    l_ref[h] = alpha * l_ref[h] + part
    acc_ref[h] = alpha * acc_ref[h] + _dot(p.astype(BF16), v_bf)
    m_ref[h] = m_new


def _flash_result(l_ref, acc_ref, h):
    return acc_ref[h] / jnp.sum(l_ref[h], axis=-1, keepdims=True)


def _head_rows(ref, h, n):
    return ref[pl.ds(h, n, stride=N_HEADS), :].astype(BF16)


def _causal_mask(s, qi, ki, tq, tk):
    rows = lax.broadcasted_iota(jnp.int32, s.shape, 0)
    qpos = qi * tq + jnp.where(rows >= tq, rows - tq, rows)
    kpos = ki * tk + lax.broadcasted_iota(jnp.int32, s.shape, 1)
    return jnp.where(kpos <= qpos, s, NEG)


def _attn_kernel(pt_ref, lq1_ref, lk1_ref, lq2_ref, lk2_ref, gsub_ref, q_ref, k_ref, v_ref, *refs,
                 diff, lam_init, dec):
    del pt_ref
    b, qi, ki = pl.program_id(0), pl.program_id(1), pl.program_id(2)
    nq, nk = pl.num_programs(1), pl.num_programs(2)
    tq, tk = q_ref.shape[0], k_ref.shape[0] // N_HEADS
    last = (qi * tq + tq - 1) // tk
    lam_refs = (lq1_ref, lk1_ref, lq2_ref, lk2_ref)
    n_extra = 0 if diff else 2
    n_dec_in = (N_DEC_SMALL + 5 * dec[0]) if dec else 0
    extra = refs[:n_extra]
    dec_in = refs[n_extra:n_extra + n_dec_in]
    outs = refs[n_extra + n_dec_in:n_extra + n_dec_in + (3 if dec else 1)]
    scratch = refs[n_extra + n_dec_in + len(outs):]
    o_ref = outs[0]
    if diff:
        qs_ref, m_ref, l_ref, acc_ref = scratch[:4]
        dec_scratch = scratch[4:]
    else:
        cq_ref, ckt_ref = extra
        qs_ref, cqr_ref, m_ref, l_ref, acc_ref = scratch[:5]
        dec_scratch = scratch[5:]

    def decode_step():
        if not dec:
            return
        group, spp, first_step = dec
        local = (b * nq + qi) * nk + ki
        t = first_step + local
        _decode_step(t % spp, spp, local == 0, lam_refs, gsub_ref, dec_in[:N_DEC_SMALL], dec_in[N_DEC_SMALL:],
                     outs[1:], dec_scratch, lam_init=lam_init, group=group)

    @pl.when(ki == 0)
    def _():
        if diff:
            lane = _lane_iota((tq, LANES))
            scale = DIFF_HALF ** -0.5
            for h in range(N_HEADS):
                qh = q_ref[:, h * HEAD_DIM:(h + 1) * HEAD_DIM] * scale
                qs_ref[h, :tq, :] = jnp.where(lane < DIFF_HALF, qh, 0.0).astype(BF16)
                qs_ref[h, tq:, :] = jnp.where(lane < DIFF_HALF, 0.0, qh).astype(BF16)
        else:
            qs_ref[...] = (q_ref[...] * (HEAD_DIM ** -0.5)).astype(BF16)
            for h in range(N_HEADS):
                cqr_ref[h] = jnp.broadcast_to(cq_ref[:, h:h + 1], (tq, LANES))
        m_ref[...] = jnp.full(m_ref.shape, NEG, F32)
        l_ref[...] = jnp.zeros(l_ref.shape, F32)
        acc_ref[...] = jnp.zeros(acc_ref.shape, F32)

    def step(masked):
        for h in range(N_HEADS):
            kh = _head_rows(k_ref, h, tk)
            if diff:
                s = _dot_nt(qs_ref[h], kh)
                cq = None
            else:
                s = _dot_nt(qs_ref[:, h * HEAD_DIM:(h + 1) * HEAD_DIM], kh) - ckt_ref[0, h:h + 1, :]
                cq = cqr_ref[h]
            if masked:
                s = _causal_mask(s, qi, ki, tq, tk)
            _flash_update(s, cq, _head_rows(v_ref, h, tk), m_ref, l_ref, acc_ref, h)

    @pl.when(ki < last)
    def _():
        step(False)
        decode_step()

    @pl.when(ki == last)
    def _():
        step(True)
        if diff:
            lam = _lambda_value(*lam_refs, lam_init)
        for h in range(N_HEADS):
            o = _flash_result(l_ref, acc_ref, h)
            if diff:
                o = _sub_norm(o[:tq] - lam * o[tq:], gsub_ref[...], lam_init)
            o_ref[:, h * HEAD_DIM:(h + 1) * HEAD_DIM] = o
        decode_step()

    if dec:
        @pl.when(ki > last)
        def _():
            decode_step()


def _attention(lams, g_sub, q_arr, q_col, k, v, fox_bias, plan, *, diff, batch, seq, tq, tk, lam_init, name):
    nq, nk = seq // tq, seq // tk
    last = lambda qi: (qi * tq + tq - 1) // tk
    small = lambda n: pl.BlockSpec((1, n), lambda b, qi, ki, pt: (0, 0))
    q_spec = pl.BlockSpec((tq, WIDTH), lambda b, qi, ki, pt: (b * nq + qi, q_col))
    kv_spec = pl.BlockSpec((tk * N_HEADS, HEAD_DIM),
                           lambda b, qi, ki, pt: (b * nk + jnp.minimum(ki, last(qi)), 0))
    o_spec = pl.BlockSpec((tq, WIDTH), lambda b, qi, ki, pt: (b * nq + qi, 0))
    in_specs = [small(DIFF_HALF)] * 4 + [small(LANES), q_spec, kv_spec, kv_spec]
    inputs = [*lams, g_sub, q_arr, k, v]
    rows = 2 * tq if diff else tq
    scratch = [pltpu.VMEM((N_HEADS, rows, HEAD_DIM), BF16) if diff else pltpu.VMEM((tq, WIDTH), BF16)]
    if not diff:
        in_specs += [pl.BlockSpec((tq, N_HEADS), lambda b, qi, ki, pt: (b * nq + qi, 0)),
                     pl.BlockSpec((1, N_HEADS, tk), lambda b, qi, ki, pt: (b, 0, jnp.minimum(ki, last(qi))))]
        inputs += list(fox_bias)
        scratch.append(pltpu.VMEM((N_HEADS, tq, LANES), F32))
    scratch += [pltpu.VMEM((N_HEADS, rows, LANES), F32), pltpu.VMEM((N_HEADS, rows, LANES), F32),
                pltpu.VMEM((N_HEADS, rows, HEAD_DIM), F32)]
    out_specs = [o_spec]
    out_shapes = [jax.ShapeDtypeStruct((batch * seq, WIDTH), F32)]
    dec = None
    pt = jnp.zeros((1,), jnp.int32)
    if plan is not None:
        in_specs += plan.in_specs
        inputs += plan.inputs
        out_specs += plan.out_specs
        out_shapes += plan.out_shapes
        scratch += _decode_scratch(plan.group)
        dec = (plan.group, plan.spp, plan.first_step)
        pt = plan.pt_flat
    grid_spec = pltpu.PrefetchScalarGridSpec(
        num_scalar_prefetch=1, grid=(batch, nq, nk), in_specs=in_specs, out_specs=tuple(out_specs),
        scratch_shapes=scratch)
    return pl.pallas_call(
        functools.partial(_attn_kernel, diff=diff, lam_init=lam_init, dec=dec),
        grid_spec=grid_spec,
        out_shape=tuple(out_shapes),
        compiler_params=_params(("arbitrary", "arbitrary", "arbitrary"), 52),
        name=name,
    )(pt, *inputs)


def _merge_kernel(od_ref, of_ref, sgd_ref, sgf_ref, x_ref, wbd_ref, wbf_ref, wo_ref, h_ref, mg_ref, *, n_col):
    j = pl.program_id(1)

    @pl.when(j < n_col)
    def _():
        a = _dot(od_ref[...].astype(BF16), wbd_ref[...])
        b = _dot(of_ref[...].astype(BF16), wbf_ref[...])
        mg_ref[j] = (sgd_ref[...] * a + sgf_ref[...] * b).astype(BF16)

    @pl.when(j >= n_col)
    def _():
        merged = jnp.concatenate([mg_ref[c] for c in range(n_col)], axis=1)
        h_ref[...] = x_ref[...] + _dot(merged, wo_ref[...])


def _merge(od, of, internal, x, w_bd, w_bf, w_o, *, tm, tn):
    m, d = x.shape
    n_col = d // tn
    gate_d = 2 * WIDTH // tn
    first = lambda i, j: jnp.minimum(j, n_col - 1)
    second = lambda i, j: (i, jnp.maximum(j - n_col, 0))
    return pl.pallas_call(
        functools.partial(_merge_kernel, n_col=n_col),
        grid=(m // tm, 2 * n_col),
        in_specs=[
            pl.BlockSpec((tm, WIDTH), lambda i, j: (i, 0)),
            pl.BlockSpec((tm, WIDTH), lambda i, j: (i, 0)),
            pl.BlockSpec((tm, tn), lambda i, j: (i, gate_d + first(i, j))),
            pl.BlockSpec((tm, tn), lambda i, j: (i, gate_d + n_col + first(i, j))),
            pl.BlockSpec((tm, tn), second),
            pl.BlockSpec((WIDTH, tn), lambda i, j: (0, first(i, j))),
            pl.BlockSpec((WIDTH, tn), lambda i, j: (0, first(i, j))),
            pl.BlockSpec((d, tn), lambda i, j: (0, jnp.maximum(j - n_col, 0))),
        ],
        out_specs=pl.BlockSpec((tm, tn), second),
        out_shape=jax.ShapeDtypeStruct((m, d), F32),
        scratch_shapes=[pltpu.VMEM((n_col, tm, tn), BF16)],
        compiler_params=_params(("arbitrary", "arbitrary"), 48),
        name="merge",
    )(od, of, internal, internal, x, w_bd, w_bf, w_o)


def _ffn_kernel(h_ref, g_ref, wg_ref, wu_ref, wd_ref, o_ref, hn_ref):
    @pl.when(pl.program_id(1) == 0)
    def _():
        h = h_ref[...]
        ms = jnp.mean(h * h, axis=-1, keepdims=True)
        hn_ref[...] = (h * lax.rsqrt(ms + EPS) * g_ref[...]).astype(BF16)
        o_ref[...] = h

    hn = hn_ref[...]
    a = _dot(hn, wg_ref[...])
    u = _dot(hn, wu_ref[...])
    ff = (a * _sigmoid(a) * u).astype(BF16)
    o_ref[...] += _dot(ff, wd_ref[...])


def _ffn(h, g, w_gate, w_up, w_down, *, tm, tf):
    m, d = h.shape
    f = w_gate.shape[1]
    return pl.pallas_call(
        _ffn_kernel,
        grid=(m // tm, f // tf),
        in_specs=[
            pl.BlockSpec((tm, d), lambda i, j: (i, 0)),
            pl.BlockSpec((1, d), lambda i, j: (0, 0)),
            pl.BlockSpec((d, tf), lambda i, j: (0, j)),
            pl.BlockSpec((d, tf), lambda i, j: (0, j)),
            pl.BlockSpec((tf, d), lambda i, j: (j, 0)),
        ],
        out_specs=pl.BlockSpec((tm, d), lambda i, j: (i, 0)),
        out_shape=jax.ShapeDtypeStruct((m, d), F32),
        scratch_shapes=[pltpu.VMEM((tm, d), BF16)],
        compiler_params=_params(("arbitrary", "arbitrary"), 48),
        name="ffn",
    )(h, g, w_gate, w_up, w_down)


def _rope_tables(pos):
    half = DIFF_HALF // 2
    inv = ROPE_THETA ** (-jnp.arange(half, dtype=F32) / half)
    ang = pos.astype(F32)[:, None] * inv[None, :]
    cos = jnp.concatenate([jnp.cos(ang)] * 4, axis=-1)
    sin = jnp.sin(ang)
    sin_signed = jnp.concatenate([-sin, sin, -sin, sin], axis=-1)
    return cos, sin_signed


def _tile(m, pref):
    return pref if m % pref == 0 else m


def _layer(l, xp, xs, caches, page_table, weights):
    (g_norm_attn, w_in, b_f, g_q_diff, g_k_diff, g_q_fox, g_k_fox, lambda_q1, lambda_k1, lambda_q2, lambda_k2,
     g_sub, w_branch_diff, w_branch_fox, w_o, g_norm_ffn, w_ffn_gate, w_ffn_up, w_ffn_down) = [w[l] for w in weights]
    batch, seq, d = xp.shape
    n_samples, dec_seq, _ = xs.shape
    past_len = page_table.shape[1] * PAGE_SIZE
    lam_init = 0.8 - 0.6 * math.exp(-0.3 * l)

    o = [0]
    for wdt in (WIDTH, WIDTH, WIDTH, WIDTH, WIDTH, WIDTH, N_HEADS, d, d):
        o.append(o[-1] + wdt)
    col = lambda a, b: w_in[:, o[a]:o[b]]
    w_main = jnp.concatenate([col(0, 1), col(3, 4), col(7, 9), col(1, 3), col(4, 6)], axis=1).astype(BF16)
    w_f = jnp.pad(col(6, 7), ((0, 0), (0, LANES - N_HEADS))).astype(BF16)
    b_f_pad = jnp.pad(b_f, (0, LANES - N_HEADS)).reshape(1, LANES)
    two = lambda g: jnp.concatenate([g, g]).reshape(1, LANES)
    one = lambda g: g.reshape(1, -1)
    lams = tuple(one(v) for v in (lambda_q1, lambda_k1, lambda_q2, lambda_k2))
    w_bd, w_bf, w_out = w_branch_diff.astype(BF16), w_branch_fox.astype(BF16), w_o.astype(BF16)
    w_g, w_u, w_d = w_ffn_gate.astype(BF16), w_ffn_up.astype(BF16), w_ffn_down.astype(BF16)

    def project(x2d, pos, tm, rows_per_table):
        cos, sin_signed = _rope_tables(pos)
        return _proj(x2d, one(g_norm_attn), w_main, w_f, b_f_pad, two(g_q_diff), two(g_k_diff), one(g_q_fox),
                     one(g_k_fox), cos, sin_signed, tm=tm, rows_per_table=rows_per_table)

    def finish(x2d, od, of, internal, tm_merge, tm_ffn):
        h = _merge(od, of, internal, x2d, w_bd, w_bf, w_out, tm=tm_merge, tn=512)
        return _ffn(h, one(g_norm_ffn), w_g, w_u, w_d, tm=tm_ffn, tf=512)

    assert dec_seq == 1
    xp2 = xp.reshape(batch * seq, d)
    xs2 = xs.reshape(n_samples, d)
    int_p, kd_p, vd_p, kf_p, vf_p, lf_p = project(xp2, jnp.arange(seq), _tile(seq, 512), seq)
    pos_s = jnp.full((n_samples,), past_len, jnp.int32)
    int_s, kd_s, vd_s, kf_s, vf_s, lf_s = project(xs2, pos_s, n_samples, n_samples)
    cum, cumt = _cumsum(lf_p, batch, seq)

    n_pages = page_table.shape[1]
    tq = tk = _tile(seq, 256)
    n_host = batch * (seq // tq) * (seq // tk)
    group = math.gcd(4, n_pages)
    n_dec = n_samples * (n_pages // group)
    hosted = n_dec == 2 * n_host
    dec_args = (page_table, int_s[:, :WIDTH], int_s[:, WIDTH:2 * WIDTH], kd_s, vd_s, kf_s, vf_s, lf_s,
                *[c[l] for c in caches])
    nq, nk = seq // tq, seq // tk
    host_step = lambda first: (lambda b, qi, ki: first + (b * nq + qi) * nk + ki)
    plans = [_DecodePlan(*dec_args, group=group, step_of=host_step(f), first_step=f, n_steps=n_host)
             for f in (0, n_host)] if hosted else [None, None]
    attn = functools.partial(_attention, lams, one(g_sub), int_p, batch=batch, seq=seq, tq=tq, tk=tk,
                             lam_init=lam_init)
    out_d = attn(0, kd_p, vd_p, None, plans[0], diff=True, name="diff_attention")
    out_f = attn(1, kf_p, vf_p, (cum, cumt), plans[1], diff=False, name="fox_attention")
    if hosted:
        od_s = jnp.concatenate([out_d[1], out_f[1]], axis=0).reshape(n_samples, WIDTH)
        of_s = jnp.concatenate([out_d[2], out_f[2]], axis=0).reshape(n_samples, WIDTH)
    else:
        group = math.gcd(8, n_pages)
        n_dec = n_samples * (n_pages // group)
        plan = _DecodePlan(*dec_args, group=group, step_of=lambda t: t, first_step=0, n_steps=n_dec)
        od_s, of_s = (o.reshape(n_samples, WIDTH) for o in _decode(plan, lams, one(g_sub), n_steps=n_dec,
                                                                  lam_init=lam_init))
    yp = finish(xp2, out_d[0], out_f[0], int_p, _tile(seq, 1024), _tile(seq, 512))
    ys = finish(xs2, od_s, of_s, int_s, n_samples, n_samples)

    heads = lambda a, b, t: a.reshape(b, t, N_HEADS, HEAD_DIM)
    new_p = (heads(kd_p, batch, seq), heads(vd_p, batch, seq), heads(kf_p, batch, seq), heads(vf_p, batch, seq),
             lf_p.reshape(batch, seq, N_HEADS))
    new_s = (heads(kd_s, n_samples, 1), heads(vd_s, n_samples, 1), heads(kf_s, n_samples, 1),
             heads(vf_s, n_samples, 1), lf_s.reshape(n_samples, 1, N_HEADS))
    return yp.reshape(batch, seq, d), ys.reshape(n_samples, dec_seq, d), new_p, new_s


def kernel(x_prompt, x_sample, cache_k_diff, cache_v_diff, cache_k_fox, cache_v_fox, cache_logf_fox, page_table,
           g_norm_attn, w_in, b_f, g_q_diff, g_k_diff, g_q_fox, g_k_fox, lambda_q1, lambda_k1, lambda_q2, lambda_k2,
           g_sub, w_branch_diff, w_branch_fox, w_o, g_norm_ffn, w_ffn_gate, w_ffn_up, w_ffn_down):
    weights = (g_norm_attn, w_in, b_f, g_q_diff, g_k_diff, g_q_fox, g_k_fox, lambda_q1, lambda_k1, lambda_q2,
               lambda_k2, g_sub, w_branch_diff, w_branch_fox, w_o, g_norm_ffn, w_ffn_gate, w_ffn_up, w_ffn_down)
    caches = (cache_k_diff, cache_v_diff, cache_k_fox, cache_v_fox, cache_logf_fox)
    depth = w_in.shape[0]
    xp, xs = x_prompt, x_sample
    new_p, new_s = [], []
    for l in range(depth):
        xp, xs, np_l, ns_l = _layer(l, xp, xs, caches, page_table, weights)
        new_p.append(np_l)
        new_s.append(ns_l)
    stack = lambda lst, i: jnp.stack([t[i] for t in lst], axis=0)
    return (xp, xs) + tuple(stack(new_p, i) for i in range(5)) + tuple(stack(new_s, i) for i in range(5))
```

```python
import functools
import math

import jax
import jax.numpy as jnp
from jax import lax
from jax.experimental import pallas as pl
from jax.experimental.pallas import tpu as pltpu

N_HEADS = 8
HEAD_DIM = 128
DIFF_HALF = HEAD_DIM // 2
WIDTH = N_HEADS * HEAD_DIM
ROPE_THETA = 10000.0
EPS = 1e-6
PAGE_SIZE = 128
LANES = 128
NEG = -1e30
MIB = 1024 * 1024

F32 = jnp.float32
BF16 = jnp.bfloat16
HIGHEST = lax.Precision.HIGHEST


def _params(semantics, vmem_mib):
    return pltpu.CompilerParams(dimension_semantics=semantics, vmem_limit_bytes=vmem_mib * MIB)


def _dot(a, b):
    return jnp.dot(a, b, preferred_element_type=F32)


def _dot_nt(a, b, precision=None):
    return lax.dot_general(a, b, (((1,), (1,)), ((), ())), preferred_element_type=F32, precision=precision)


def _lane_iota(shape):
    return lax.broadcasted_iota(jnp.int32, shape, len(shape) - 1)


def _rms_rope_head(a, g, cos, sin_signed, lane):
    sq = a * a
    lo = lane < DIFF_HALF
    s_lo = jnp.sum(jnp.where(lo, sq, 0.0), axis=-1, keepdims=True)
    s_hi = jnp.sum(jnp.where(lo, 0.0, sq), axis=-1, keepdims=True)
    ms = jnp.where(lo, s_lo, s_hi) * (1.0 / DIFF_HALF)
    y = a * lax.rsqrt(ms + EPS) * g
    first = (lane & (DIFF_HALF - 1)) < (DIFF_HALF // 2)
    rot = jnp.where(first, pltpu.roll(y, LANES - DIFF_HALF // 2, 1), pltpu.roll(y, DIFF_HALF // 2, 1))
    return y * cos + rot * sin_signed


def _rms_head(a, g):
    ms = jnp.mean(a * a, axis=-1, keepdims=True)
    return a * lax.rsqrt(ms + EPS) * g


def _log_sigmoid(z):
    return -(jnp.maximum(-z, 0.0) + jnp.log1p(jnp.exp(-jnp.abs(z))))


def _sigmoid(z):
    return 1.0 / (1.0 + jnp.exp(-z))


def _proj_kernel(x_ref, gn_ref, w_ref, wf_ref, bf_ref, gqd_ref, gkd_ref, gqf_ref, gkf_ref, cos_ref, sin_ref,
                 int_ref, kd_ref, vd_ref, kf_ref, vf_ref, logf_ref, xn_ref):
    j = pl.program_id(1)
    tm = x_ref.shape[0]

    @pl.when(j == 0)
    def _():
        x = x_ref[...]
        ms = jnp.mean(x * x, axis=-1, keepdims=True)
        xn_ref[...] = (x * lax.rsqrt(ms + EPS) * gn_ref[...]).astype(BF16)
        z = _dot(xn_ref[...], wf_ref[...]) + bf_ref[...]
        logf_ref[...] = _log_sigmoid(z)[:, :N_HEADS]

    lane = _lane_iota((tm, LANES))
    pair = 2 * HEAD_DIM

    def head_pairs():
        for c in range(WIDTH // pair):
            acc = _dot(xn_ref[...], w_ref[:, c * pair:(c + 1) * pair])
            for k in range(2):
                yield 2 * c + k, acc[:, k * HEAD_DIM:(k + 1) * HEAD_DIM]

    def heads(fn, out_ref):
        for h, a in head_pairs():
            out_ref[:, h * HEAD_DIM:(h + 1) * HEAD_DIM] = fn(a)

    def heads_token_major(fn, out_ref):
        for h, a in head_pairs():
            out_ref[pl.ds(h, tm, stride=N_HEADS), :] = fn(a)

    @pl.when(j == 0)
    def _():
        heads(lambda a: _rms_rope_head(a, gqd_ref[...], cos_ref[...], sin_ref[...], lane), int_ref)

    @pl.when(j == 1)
    def _():
        heads(lambda a: _rms_head(a, gqf_ref[...]), int_ref)

    @pl.when((j >= 2) & (j < 6))
    def _():
        heads(_sigmoid, int_ref)

    @pl.when(j == 6)
    def _():
        heads_token_major(lambda a: _rms_rope_head(a, gkd_ref[...], cos_ref[...], sin_ref[...], lane), kd_ref)

    @pl.when(j == 7)
    def _():
        heads_token_major(lambda a: a, vd_ref)

    @pl.when(j == 8)
    def _():
        heads_token_major(lambda a: _rms_head(a, gkf_ref[...]), kf_ref)

    @pl.when(j == 9)
    def _():
        heads_token_major(lambda a: a, vf_ref)


def _proj(x, gn, w_main, w_f, b_f, gqd, gkd, gqf, gkf, cos, sin_signed, *, tm, rows_per_table):
    m, d = x.shape
    n_int = 6
    n_tab = rows_per_table // tm
    row = lambda i, j: (i, 0)
    const = lambda i, j: (0, 0)
    kv_spec = pl.BlockSpec((tm * N_HEADS, HEAD_DIM), row)
    kv_shape = jax.ShapeDtypeStruct((m * N_HEADS, HEAD_DIM), F32)
    out_shape = (
        jax.ShapeDtypeStruct((m, n_int * WIDTH), F32),
        kv_shape, kv_shape, kv_shape, kv_shape,
        jax.ShapeDtypeStruct((m, N_HEADS), F32),
    )
    return pl.pallas_call(
        _proj_kernel,
        grid=(m // tm, n_int + 4),
        in_specs=[
            pl.BlockSpec((tm, d), row),
            pl.BlockSpec((1, d), const),
            pl.BlockSpec((d, WIDTH), lambda i, j: (0, j)),
            pl.BlockSpec((d, LANES), const),
            pl.BlockSpec((1, LANES), const),
            pl.BlockSpec((1, LANES), const),
            pl.BlockSpec((1, LANES), const),
            pl.BlockSpec((1, LANES), const),
            pl.BlockSpec((1, LANES), const),
            pl.BlockSpec((tm, LANES), lambda i, j: (i % n_tab, 0)),
            pl.BlockSpec((tm, LANES), lambda i, j: (i % n_tab, 0)),
        ],
        out_specs=(
            pl.BlockSpec((tm, WIDTH), lambda i, j: (i, jnp.minimum(j, n_int - 1))),
            kv_spec, kv_spec, kv_spec, kv_spec,
            pl.BlockSpec((tm, N_HEADS), row),
        ),
        out_shape=out_shape,
        scratch_shapes=[pltpu.VMEM((tm, d), BF16)],
        compiler_params=_params(("arbitrary", "arbitrary"), 48),
        name="proj",
    )(x, gn, w_main, w_f, b_f, gqd, gkd, gqf, gkf, cos, sin_signed)


def _pad_lanes(chunk, lane):
    out = jnp.zeros(lane.shape, F32)
    for h in range(N_HEADS):
        out = jnp.where(lane == h, chunk[:, h:h + 1], out)
    return out


def _cumsum_kernel(lf_ref, cum_ref, cumt_ref):
    s = lf_ref.shape[0]
    r = lax.broadcasted_iota(jnp.int32, (LANES, LANES), 0)
    c = lax.broadcasted_iota(jnp.int32, (LANES, LANES), 1)
    tri = (c <= r).astype(F32)
    carry = jnp.zeros((1, LANES), F32)
    for ci in range(s // LANES):
        rows = slice(ci * LANES, (ci + 1) * LANES)
        pad = _pad_lanes(lf_ref[rows, :], c)
        res = jnp.dot(tri, pad, preferred_element_type=F32, precision=HIGHEST) + carry
        cum_ref[rows, :] = res[:, :N_HEADS]
        cumt_ref[0, :, rows] = res.T[:N_HEADS, :]
        carry = res[LANES - 1:LANES, :]


def _cumsum(logf, batch, seq):
    return pl.pallas_call(
        _cumsum_kernel,
        grid=(batch,),
        in_specs=[pl.BlockSpec((seq, N_HEADS), lambda b: (b, 0))],
        out_specs=(
            pl.BlockSpec((seq, N_HEADS), lambda b: (b, 0)),
            pl.BlockSpec((1, N_HEADS, seq), lambda b: (b, 0, 0)),
        ),
        out_shape=(
            jax.ShapeDtypeStruct((batch * seq, N_HEADS), F32),
            jax.ShapeDtypeStruct((batch, N_HEADS, seq), F32),
        ),
        compiler_params=_params(("arbitrary",), 32),
        name="cumsum",
    )(logf)


ROWS = 2 * N_HEADS
PAGE_COLS = PAGE_SIZE * N_HEADS
N_DEC_SMALL = 7


def _split3(a):
    hi = a.astype(BF16)
    r1 = a - hi.astype(F32)
    mid = r1.astype(BF16)
    lo = (r1 - mid.astype(F32)).astype(BF16)
    return hi, mid, lo


def _stack2(a):
    return jnp.concatenate([a, a], axis=0)


def _lambda_value(lq1_ref, lk1_ref, lq2_ref, lk2_ref, lam_init):
    a = jnp.sum(lq1_ref[...] * lk1_ref[...], axis=-1, keepdims=True)
    b = jnp.sum(lq2_ref[...] * lk2_ref[...], axis=-1, keepdims=True)
    return jnp.exp(a) - jnp.exp(b) + lam_init


def _sub_norm(o, g, lam_init):
    ms = jnp.mean(o * o, axis=-1, keepdims=True)
    return o * lax.rsqrt(ms + EPS) * g * (1.0 - lam_init)


def _decode_step(p, n_steps, very_first, lam_refs, gsub_ref, small_refs, page_refs, out_refs, scratch_refs,
                 *, lam_init, group):
    qd_ref, qf_ref, kdn_ref, vdn_ref, kfn_ref, vfn_ref, lfn_ref = small_refs
    kd_refs, vd_refs, kf_refs, vf_refs, lf_refs = (page_refs[i * group:(i + 1) * group] for i in range(5))
    od_ref, of_ref = out_refs
    qs_ref, m_ref, l_ref, acc_ref, carry_ref, later_ref, own_ref = scratch_refs
    cols = group * PAGE_COLS

    @pl.when(very_first)
    def _():
        ks = lax.broadcasted_iota(jnp.int32, (PAGE_SIZE, PAGE_COLS), 0)
        kc = lax.broadcasted_iota(jnp.int32, (PAGE_SIZE, PAGE_COLS), 1)
        later_ref[...] = (ks > (kc >> 3)).astype(BF16)
        r = lax.broadcasted_iota(jnp.int32, (2 * ROWS, cols), 0)
        c = lax.broadcasted_iota(jnp.int32, (2 * ROWS, cols), 1)
        own_ref[...] = jnp.where((r & (N_HEADS - 1)) == (c & (N_HEADS - 1)), 0.0, NEG)

    @pl.when(p == 0)
    def _():
        lane = _lane_iota((N_HEADS, HEAD_DIM))
        q = qd_ref[0] * (DIFF_HALF ** -0.5)
        zeros = jnp.zeros((N_HEADS, HEAD_DIM), F32)
        q_d = jnp.concatenate([jnp.where(lane < DIFF_HALF, q, 0.0), jnp.where(lane < DIFF_HALF, 0.0, q)], axis=0)
        q_f = jnp.concatenate([qf_ref[0] * (HEAD_DIM ** -0.5), zeros], axis=0)
        qs_ref[:ROWS, :HEAD_DIM] = q_d.astype(BF16)
        qs_ref[:ROWS, HEAD_DIM:] = jnp.zeros((ROWS, HEAD_DIM), BF16)
        qs_ref[ROWS:, :HEAD_DIM] = jnp.zeros((ROWS, HEAD_DIM), BF16)
        qs_ref[ROWS:, HEAD_DIM:] = q_f.astype(BF16)
        m_ref[:ROWS] = jnp.sum(q_d * _stack2(kdn_ref[0]), axis=-1, keepdims=True)
        m_ref[ROWS:] = jnp.sum(q_f * _stack2(kfn_ref[0]), axis=-1, keepdims=True)
        l_ref[...] = jnp.ones(l_ref.shape, F32)
        acc_ref[...] = jnp.concatenate([_stack2(vdn_ref[0]), _stack2(vfn_ref[0])], axis=0)
        carry_ref[...] = _stack2(lfn_ref[0])

    def both(d_refs, f_refs):
        flat = lambda refs: jnp.concatenate(
            [ref[...].reshape(PAGE_COLS, HEAD_DIM).astype(BF16) for ref in refs], axis=0)
        return jnp.concatenate([flat(d_refs), flat(f_refs)], axis=1)

    pages_lf = [_stack2(lf[...]) for lf in lf_refs]
    within = _dot(jnp.concatenate([t for page_lf in pages_lf for t in _split3(page_lf)], axis=0), later_ref[...])
    carry = carry_ref[...]
    biases = []
    for g, page_lf in enumerate(pages_lf):
        w = within[3 * g * ROWS:3 * (g + 1) * ROWS]
        biases.append(carry + w[:ROWS] + w[ROWS:2 * ROWS] + w[2 * ROWS:])
        carry = carry + jnp.sum(page_lf, axis=-1, keepdims=True)
    carry_ref[...] = carry

    halves = 2 if group % 2 == 0 else 1
    per = group // halves
    part = lambda refs, i: refs[i * per:(i + 1) * per]
    logits = [_dot_nt(qs_ref[...], both(part(kd_refs, i), part(kf_refs, i))) for i in range(halves)]
    m, l, acc = m_ref[...], l_ref[...], acc_ref[...]
    for i in range(halves):
        bias = jnp.concatenate(biases[i * per:(i + 1) * per], axis=1)
        s = jnp.concatenate([logits[i][:ROWS], logits[i][ROWS:] + bias], axis=0) + own_ref[:, :per * PAGE_COLS]
        m_new = jnp.maximum(m, jnp.max(s, axis=-1, keepdims=True))
        alpha = jnp.exp(m - m_new)
        pr = jnp.exp(s - m_new)
        l = alpha * l + jnp.sum(pr, axis=-1, keepdims=True)
        pv = _dot(pr.astype(BF16), both(part(vd_refs, i), part(vf_refs, i)))
        acc = alpha * acc + jnp.concatenate([pv[:ROWS, :HEAD_DIM], pv[ROWS:, HEAD_DIM:]], axis=0)
        m = m_new
    m_ref[...], l_ref[...], acc_ref[...] = m, l, acc

    @pl.when(p == n_steps - 1)
    def _():
        lam = _lambda_value(*lam_refs, lam_init)
        o = acc_ref[...] / l_ref[...]
        od_ref[0] = _sub_norm(o[:N_HEADS] - lam * o[N_HEADS:ROWS], gsub_ref[...], lam_init)
        of_ref[0] = o[ROWS:ROWS + N_HEADS]


def _decode_scratch(group):
    return [
        pltpu.VMEM((2 * ROWS, 2 * HEAD_DIM), BF16),
        pltpu.VMEM((2 * ROWS, 1), F32),
        pltpu.VMEM((2 * ROWS, 1), F32),
        pltpu.VMEM((2 * ROWS, HEAD_DIM), F32),
        pltpu.VMEM((ROWS, 1), F32),
        pltpu.VMEM((PAGE_SIZE, PAGE_COLS), BF16),
        pltpu.VMEM((2 * ROWS, group * PAGE_COLS), F32),
    ]


class _DecodePlan:
    def __init__(self, page_table, qd, qf, kd_new, vd_new, kf_new, vf_new, lf_new,
                 cache_kd, cache_vd, cache_kf, cache_vf, cache_lf, *, group, step_of, first_step, n_steps):
        n_samples, n_pages = page_table.shape
        assert n_pages % group == 0
        spp = n_pages // group
        assert first_step % spp == 0 and n_steps % spp == 0
        self.group, self.spp, self.first_step, self.n_steps = group, spp, first_step, n_steps
        self.pt_flat = page_table.reshape(-1)
        tile3 = lambda a: a.reshape(n_samples, N_HEADS, HEAD_DIM)
        first_sample = first_step // spp
        sample = lambda *ids: step_of(*ids) // spp
        page = lambda g: (lambda *ids_pt: ids_pt[-1][
            sample(*ids_pt[:-1]) * n_pages + n_pages - 1 - ((step_of(*ids_pt[:-1]) % spp) * group + g)])
        tile_spec = pl.BlockSpec((1, N_HEADS, HEAD_DIM), lambda *a: (sample(*a[:-1]), 0, 0))
        lfn_spec = pl.BlockSpec((1, N_HEADS, 1), lambda *a: (sample(*a[:-1]), 0, 0))
        page_specs = lambda: [pl.BlockSpec((None, PAGE_SIZE, N_HEADS, HEAD_DIM),
                                           lambda *a, f=page(g): (f(*a), 0, 0, 0)) for g in range(group)]
        lf_specs = [pl.BlockSpec((None, N_HEADS, PAGE_SIZE), lambda *a, f=page(g): (f(*a), 0, 0))
                    for g in range(group)]
        cache_lf_t = jnp.swapaxes(cache_lf, 1, 2)
        self.inputs = [tile3(qd), tile3(qf), tile3(kd_new), tile3(vd_new), tile3(kf_new), tile3(vf_new),
                       lf_new.reshape(n_samples, N_HEADS, 1)] + [cache_kd] * group + [cache_vd] * group \
            + [cache_kf] * group + [cache_vf] * group + [cache_lf_t] * group
        self.in_specs = [tile_spec] * 6 + [lfn_spec] + page_specs() + page_specs() + page_specs() + page_specs() \
            + lf_specs
        out_spec = pl.BlockSpec((1, N_HEADS, HEAD_DIM), lambda *a: (sample(*a[:-1]) - first_sample, 0, 0))
        self.out_specs = [out_spec, out_spec]
        out_shape = jax.ShapeDtypeStruct((n_steps // spp, N_HEADS, HEAD_DIM), F32)
        self.out_shapes = [out_shape, out_shape]
        self.n_in = len(self.inputs)


def _decode_kernel(pt_ref, lq1_ref, lk1_ref, lq2_ref, lk2_ref, gsub_ref, *refs, lam_init, group, spp):
    del pt_ref
    n_in = N_DEC_SMALL + 5 * group
    t = pl.program_id(0)
    _decode_step(t % spp, spp, t == 0, (lq1_ref, lk1_ref, lq2_ref, lk2_ref), gsub_ref,
                 refs[:N_DEC_SMALL], refs[N_DEC_SMALL:n_in], refs[n_in:n_in + 2], refs[n_in + 2:],
                 lam_init=lam_init, group=group)


def _decode(plan, lams, g_sub, *, n_steps, lam_init):
    small = lambda n: pl.BlockSpec((1, n), lambda t, pt: (0, 0))
    grid_spec = pltpu.PrefetchScalarGridSpec(
        num_scalar_prefetch=1,
        grid=(n_steps,),
        in_specs=[small(DIFF_HALF)] * 4 + [small(LANES)] + plan.in_specs,
        out_specs=tuple(plan.out_specs),
        scratch_shapes=_decode_scratch(plan.group),
    )
    return pl.pallas_call(
        functools.partial(_decode_kernel, lam_init=lam_init, group=plan.group, spp=plan.spp),
        grid_spec=grid_spec,
        out_shape=tuple(plan.out_shapes),
        compiler_params=_params(("arbitrary",), 52),
        name="decode",
    )(plan.pt_flat, *lams, g_sub, *plan.inputs)


def _flash_update(s, cq, v_bf, m_ref, l_ref, acc_ref, h):
    reps = s.shape[1] // LANES
    m_prev = m_ref[h]
    m_curr = jnp.max(s, axis=-1, keepdims=True)
    if cq is not None:
        m_curr = m_curr + cq
    m_new = jnp.maximum(m_prev, m_curr)
    alpha = jnp.exp(m_prev - m_new)
    shift = m_new if cq is None else m_new - cq
    p = jnp.exp(s - jnp.tile(shift, (1, reps)))
    part = p[:, :LANES]
    for j in range(1, reps):
        part = part + p[:, j * LANES:(j + 1) * LANES]
    l_ref[h] = alpha * l_ref[h] + part
    acc_ref[h] = alpha * acc_ref[h] + _dot(p.astype(BF16), v_bf)
    m_ref[h] = m_new


def _flash_result(l_ref, acc_ref, h):
    return acc_ref[h] / jnp.sum(l_ref[h], axis=-1, keepdims=True)


def _head_rows(ref, h, n):
    return ref[pl.ds(h, n, stride=N_HEADS), :].astype(BF16)


def _causal_mask(s, qi, ki, tq, tk):
    rows = lax.broadcasted_iota(jnp.int32, s.shape, 0)
    qpos = qi * tq + jnp.where(rows >= tq, rows - tq, rows)
    kpos = ki * tk + lax.broadcasted_iota(jnp.int32, s.shape, 1)
    return jnp.where(kpos <= qpos, s, NEG)


def _attn_kernel(pt_ref, lq1_ref, lk1_ref, lq2_ref, lk2_ref, gsub_ref, q_ref, k_ref, v_ref, *refs,
                 diff, lam_init, dec):
    del pt_ref
    b, pair, j = pl.program_id(0), pl.program_id(1), pl.program_id(2)
    n_pairs, n_j = pl.num_programs(1), pl.num_programs(2)
    tq, tk = q_ref.shape[0], k_ref.shape[0] // N_HEADS
    qi, ki = _folded(pair, j, n_j - 1)
    lam_refs = (lq1_ref, lk1_ref, lq2_ref, lk2_ref)
    n_extra = 0 if diff else 2
    n_dec_in = (N_DEC_SMALL + 5 * dec[0]) if dec else 0
    extra = refs[:n_extra]
    dec_in = refs[n_extra:n_extra + n_dec_in]
    outs = refs[n_extra + n_dec_in:n_extra + n_dec_in + (3 if dec else 1)]
    scratch = refs[n_extra + n_dec_in + len(outs):]
    o_ref = outs[0]
    if diff:
        qs_ref, m_ref, l_ref, acc_ref = scratch[:4]
        dec_scratch = scratch[4:]
    else:
        cq_ref, ckt_ref = extra
        qs_ref, cqr_ref, m_ref, l_ref, acc_ref = scratch[:5]
        dec_scratch = scratch[5:]

    def decode_step():
        if not dec:
            return
        group, spp, first_step, n_steps = dec
        local = (b * n_pairs + pair) * n_j + j

        @pl.when(local < n_steps)
        def _():
            _decode_step((first_step + local) % spp, spp, local == 0, lam_refs, gsub_ref, dec_in[:N_DEC_SMALL],
                         dec_in[N_DEC_SMALL:], outs[1:], dec_scratch, lam_init=lam_init, group=group)

    @pl.when(ki == 0)
    def _():
        if diff:
            lane = _lane_iota((tq, LANES))
            scale = DIFF_HALF ** -0.5
            for h in range(N_HEADS):
                qh = q_ref[:, h * HEAD_DIM:(h + 1) * HEAD_DIM] * scale
                qs_ref[h, :tq, :] = jnp.where(lane < DIFF_HALF, qh, 0.0).astype(BF16)
                qs_ref[h, tq:, :] = jnp.where(lane < DIFF_HALF, 0.0, qh).astype(BF16)
        else:
            qs_ref[...] = (q_ref[...] * (HEAD_DIM ** -0.5)).astype(BF16)
            for h in range(N_HEADS):
                cqr_ref[h] = jnp.broadcast_to(cq_ref[:, h:h + 1], (tq, LANES))
        m_ref[...] = jnp.full(m_ref.shape, NEG, F32)
        l_ref[...] = jnp.zeros(l_ref.shape, F32)
        acc_ref[...] = jnp.zeros(acc_ref.shape, F32)

    def step(masked):
        for h in range(N_HEADS):
            kh = _head_rows(k_ref, h, tk)
            if diff:
                s = _dot_nt(qs_ref[h], kh)
                cq = None
            else:
                s = _dot_nt(qs_ref[:, h * HEAD_DIM:(h + 1) * HEAD_DIM], kh) - ckt_ref[0, h:h + 1, :]
                cq = cqr_ref[h]
            if masked:
                s = _causal_mask(s, qi, ki, tq, tk)
            _flash_update(s, cq, _head_rows(v_ref, h, tk), m_ref, l_ref, acc_ref, h)

    @pl.when(ki < qi)
    def _():
        step(False)
        decode_step()

    @pl.when(ki == qi)
    def _():
        step(True)
        if diff:
            lam = _lambda_value(*lam_refs, lam_init)
        for h in range(N_HEADS):
            o = _flash_result(l_ref, acc_ref, h)
            if diff:
                o = _sub_norm(o[:tq] - lam * o[tq:], gsub_ref[...], lam_init)
            o_ref[:, h * HEAD_DIM:(h + 1) * HEAD_DIM] = o
        decode_step()


def _folded(pair, j, nq):
    first = j <= pair
    return jnp.where(first, pair, nq - 1 - pair), jnp.where(first, j, j - pair - 1)


def _attention(lams, g_sub, q_arr, q_col, k, v, fox_bias, plan, *, diff, batch, seq, tq, tk, lam_init, name):
    assert tq == tk and (seq // tq) % 2 == 0
    nq = seq // tq
    qrow = lambda b, r, j: b * nq + _folded(r, j, nq)[0]
    krow = lambda b, r, j: b * nq + _folded(r, j, nq)[1]
    small = lambda n: pl.BlockSpec((1, n), lambda b, r, j, pt: (0, 0))
    q_spec = pl.BlockSpec((tq, WIDTH), lambda b, r, j, pt: (qrow(b, r, j), q_col))
    kv_spec = pl.BlockSpec((tk * N_HEADS, HEAD_DIM), lambda b, r, j, pt: (krow(b, r, j), 0))
    o_spec = pl.BlockSpec((tq, WIDTH), lambda b, r, j, pt: (qrow(b, r, j), 0))
    in_specs = [small(DIFF_HALF)] * 4 + [small(LANES), q_spec, kv_spec, kv_spec]
    inputs = [*lams, g_sub, q_arr, k, v]
    rows = 2 * tq if diff else tq
    scratch = [pltpu.VMEM((N_HEADS, rows, HEAD_DIM), BF16) if diff else pltpu.VMEM((tq, WIDTH), BF16)]
    if not diff:
        in_specs += [pl.BlockSpec((tq, N_HEADS), lambda b, r, j, pt: (qrow(b, r, j), 0)),
                     pl.BlockSpec((1, N_HEADS, tk), lambda b, r, j, pt: (b, 0, _folded(r, j, nq)[1]))]
        inputs += list(fox_bias)
        scratch.append(pltpu.VMEM((N_HEADS, tq, LANES), F32))
    scratch += [pltpu.VMEM((N_HEADS, rows, LANES), F32), pltpu.VMEM((N_HEADS, rows, LANES), F32),
                pltpu.VMEM((N_HEADS, rows, HEAD_DIM), F32)]
    out_specs = [o_spec]
    out_shapes = [jax.ShapeDtypeStruct((batch * seq, WIDTH), F32)]
    dec = None
    pt = jnp.zeros((1,), jnp.int32)
    if plan is not None:
        in_specs += plan.in_specs
        inputs += plan.inputs
        out_specs += plan.out_specs
        out_shapes += plan.out_shapes
        scratch += _decode_scratch(plan.group)
        dec = (plan.group, plan.spp, plan.first_step, plan.n_steps)
        pt = plan.pt_flat
    grid_spec = pltpu.PrefetchScalarGridSpec(
        num_scalar_prefetch=1, grid=(batch, nq // 2, nq + 1), in_specs=in_specs, out_specs=tuple(out_specs),
        scratch_shapes=scratch)
    return pl.pallas_call(
        functools.partial(_attn_kernel, diff=diff, lam_init=lam_init, dec=dec),
        grid_spec=grid_spec,
        out_shape=tuple(out_shapes),
        compiler_params=_params(("arbitrary", "arbitrary", "arbitrary"), 58),
        name=name,
    )(pt, *inputs)


def _merge_kernel(od_ref, of_ref, sgd_ref, sgf_ref, x_ref, wbd_ref, wbf_ref, wo_ref, h_ref, mg_ref, *, n_col):
    j = pl.program_id(1)

    @pl.when(j < n_col)
    def _():
        a = _dot(od_ref[...].astype(BF16), wbd_ref[...])
        b = _dot(of_ref[...].astype(BF16), wbf_ref[...])
        mg_ref[j] = (sgd_ref[...] * a + sgf_ref[...] * b).astype(BF16)

    @pl.when(j >= n_col)
    def _():
        merged = jnp.concatenate([mg_ref[c] for c in range(n_col)], axis=1)
        h_ref[...] = x_ref[...] + _dot(merged, wo_ref[...])


def _merge(od, of, internal, x, w_bd, w_bf, w_o, *, tm, tn):
    m, d = x.shape
    n_col = d // tn
    gate_d = 2 * WIDTH // tn
    first = lambda i, j: jnp.minimum(j, n_col - 1)
    second = lambda i, j: (i, jnp.maximum(j - n_col, 0))
    return pl.pallas_call(
        functools.partial(_merge_kernel, n_col=n_col),
        grid=(m // tm, 2 * n_col),
        in_specs=[
            pl.BlockSpec((tm, WIDTH), lambda i, j: (i, 0)),
            pl.BlockSpec((tm, WIDTH), lambda i, j: (i, 0)),
            pl.BlockSpec((tm, tn), lambda i, j: (i, gate_d + first(i, j))),
            pl.BlockSpec((tm, tn), lambda i, j: (i, gate_d + n_col + first(i, j))),
            pl.BlockSpec((tm, tn), second),
            pl.BlockSpec((WIDTH, tn), lambda i, j: (0, first(i, j))),
            pl.BlockSpec((WIDTH, tn), lambda i, j: (0, first(i, j))),
            pl.BlockSpec((d, tn), lambda i, j: (0, jnp.maximum(j - n_col, 0))),
        ],
        out_specs=pl.BlockSpec((tm, tn), second),
        out_shape=jax.ShapeDtypeStruct((m, d), F32),
        scratch_shapes=[pltpu.VMEM((n_col, tm, tn), BF16)],
        compiler_params=_params(("arbitrary", "arbitrary"), 48),
        name="merge",
    )(od, of, internal, internal, x, w_bd, w_bf, w_o)


def _ffn_kernel(h_ref, g_ref, wg_ref, wu_ref, wd_ref, o_ref, hn_ref):
    @pl.when(pl.program_id(1) == 0)
    def _():
        h = h_ref[...]
        ms = jnp.mean(h * h, axis=-1, keepdims=True)
        hn_ref[...] = (h * lax.rsqrt(ms + EPS) * g_ref[...]).astype(BF16)
        o_ref[...] = h

    hn = hn_ref[...]
    a = _dot(hn, wg_ref[...])
    u = _dot(hn, wu_ref[...])
    ff = (a * _sigmoid(a) * u).astype(BF16)
    o_ref[...] += _dot(ff, wd_ref[...])


def _ffn(h, g, w_gate, w_up, w_down, *, tm, tf):
    m, d = h.shape
    f = w_gate.shape[1]
    return pl.pallas_call(
        _ffn_kernel,
        grid=(m // tm, f // tf),
        in_specs=[
            pl.BlockSpec((tm, d), lambda i, j: (i, 0)),
            pl.BlockSpec((1, d), lambda i, j: (0, 0)),
            pl.BlockSpec((d, tf), lambda i, j: (0, j)),
            pl.BlockSpec((d, tf), lambda i, j: (0, j)),
            pl.BlockSpec((tf, d), lambda i, j: (j, 0)),
        ],
        out_specs=pl.BlockSpec((tm, d), lambda i, j: (i, 0)),
        out_shape=jax.ShapeDtypeStruct((m, d), F32),
        scratch_shapes=[pltpu.VMEM((tm, d), BF16)],
        compiler_params=_params(("arbitrary", "arbitrary"), 48),
        name="ffn",
    )(h, g, w_gate, w_up, w_down)


def _rope_tables(pos):
    half = DIFF_HALF // 2
    inv = ROPE_THETA ** (-jnp.arange(half, dtype=F32) / half)
    ang = pos.astype(F32)[:, None] * inv[None, :]
    cos = jnp.concatenate([jnp.cos(ang)] * 4, axis=-1)
    sin = jnp.sin(ang)
    sin_signed = jnp.concatenate([-sin, sin, -sin, sin], axis=-1)
    return cos, sin_signed


def _tile(m, pref):
    return pref if m % pref == 0 else m


def _layer(l, xp, xs, caches, page_table, weights):
    (g_norm_attn, w_in, b_f, g_q_diff, g_k_diff, g_q_fox, g_k_fox, lambda_q1, lambda_k1, lambda_q2, lambda_k2,
     g_sub, w_branch_diff, w_branch_fox, w_o, g_norm_ffn, w_ffn_gate, w_ffn_up, w_ffn_down) = [w[l] for w in weights]
    batch, seq, d = xp.shape
    n_samples, dec_seq, _ = xs.shape
    past_len = page_table.shape[1] * PAGE_SIZE
    lam_init = 0.8 - 0.6 * math.exp(-0.3 * l)

    o = [0]
    for wdt in (WIDTH, WIDTH, WIDTH, WIDTH, WIDTH, WIDTH, N_HEADS, d, d):
        o.append(o[-1] + wdt)
    col = lambda a, b: w_in[:, o[a]:o[b]]
    w_main = jnp.concatenate([col(0, 1), col(3, 4), col(7, 9), col(1, 3), col(4, 6)], axis=1).astype(BF16)
    w_f = jnp.pad(col(6, 7), ((0, 0), (0, LANES - N_HEADS))).astype(BF16)
    b_f_pad = jnp.pad(b_f, (0, LANES - N_HEADS)).reshape(1, LANES)
    two = lambda g: jnp.concatenate([g, g]).reshape(1, LANES)
    one = lambda g: g.reshape(1, -1)
    lams = tuple(one(v) for v in (lambda_q1, lambda_k1, lambda_q2, lambda_k2))
    w_bd, w_bf, w_out = w_branch_diff.astype(BF16), w_branch_fox.astype(BF16), w_o.astype(BF16)
    w_g, w_u, w_d = w_ffn_gate.astype(BF16), w_ffn_up.astype(BF16), w_ffn_down.astype(BF16)

    def project(x2d, pos, tm, rows_per_table):
        cos, sin_signed = _rope_tables(pos)
        return _proj(x2d, one(g_norm_attn), w_main, w_f, b_f_pad, two(g_q_diff), two(g_k_diff), one(g_q_fox),
                     one(g_k_fox), cos, sin_signed, tm=tm, rows_per_table=rows_per_table)

    def finish(x2d, od, of, internal, tm_merge, tm_ffn):
        h = _merge(od, of, internal, x2d, w_bd, w_bf, w_out, tm=tm_merge, tn=512)
        return _ffn(h, one(g_norm_ffn), w_g, w_u, w_d, tm=tm_ffn, tf=512)

    assert dec_seq == 1
    xp2 = xp.reshape(batch * seq, d)
    xs2 = xs.reshape(n_samples, d)
    int_p, kd_p, vd_p, kf_p, vf_p, lf_p = project(xp2, jnp.arange(seq), _tile(seq, 512), seq)
    pos_s = jnp.full((n_samples,), past_len, jnp.int32)
    int_s, kd_s, vd_s, kf_s, vf_s, lf_s = project(xs2, pos_s, n_samples, n_samples)
    cum, cumt = _cumsum(lf_p, batch, seq)

    n_pages = page_table.shape[1]
    tq = tk = _tile(seq, 256)
    nq = seq // tq
    n_host = batch * (nq // 2) * (nq + 1)
    group = math.gcd(8, n_pages)
    spp = n_pages // group
    n_half = (n_samples // 2) * spp
    hosted = n_samples % 2 == 0 and n_half <= n_host
    dec_args = (page_table, int_s[:, :WIDTH], int_s[:, WIDTH:2 * WIDTH], kd_s, vd_s, kf_s, vf_s, lf_s,
                *[c[l] for c in caches])

    def host_step(first):
        return lambda b, r, j: first + jnp.minimum((b * (nq // 2) + r) * (nq + 1) + j, n_half - 1)

    plans = [_DecodePlan(*dec_args, group=group, step_of=host_step(f), first_step=f, n_steps=n_half)
             for f in (0, n_half)] if hosted else [None, None]
    attn = functools.partial(_attention, lams, one(g_sub), int_p, batch=batch, seq=seq, tq=tq, tk=tk,
                             lam_init=lam_init)
    out_d = attn(0, kd_p, vd_p, None, plans[0], diff=True, name="diff_attention")
    out_f = attn(1, kf_p, vf_p, (cum, cumt), plans[1], diff=False, name="fox_attention")
    if hosted:
        od_s = jnp.concatenate([out_d[1], out_f[1]], axis=0).reshape(n_samples, WIDTH)
        of_s = jnp.concatenate([out_d[2], out_f[2]], axis=0).reshape(n_samples, WIDTH)
    else:
        n_dec = n_samples * spp
        plan = _DecodePlan(*dec_args, group=group, step_of=lambda t: t, first_step=0, n_steps=n_dec)
        od_s, of_s = (o.reshape(n_samples, WIDTH) for o in _decode(plan, lams, one(g_sub), n_steps=n_dec,
                                                                  lam_init=lam_init))
    yp = finish(xp2, out_d[0], out_f[0], int_p, _tile(seq, 1024), _tile(seq, 512))
    ys = finish(xs2, od_s, of_s, int_s, n_samples, n_samples)

    heads = lambda a, b, t: a.reshape(b, t, N_HEADS, HEAD_DIM)
    new_p = (heads(kd_p, batch, seq), heads(vd_p, batch, seq), heads(kf_p, batch, seq), heads(vf_p, batch, seq),
             lf_p.reshape(batch, seq, N_HEADS))
    new_s = (heads(kd_s, n_samples, 1), heads(vd_s, n_samples, 1), heads(kf_s, n_samples, 1),
             heads(vf_s, n_samples, 1), lf_s.reshape(n_samples, 1, N_HEADS))
    return yp.reshape(batch, seq, d), ys.reshape(n_samples, dec_seq, d), new_p, new_s


def kernel(x_prompt, x_sample, cache_k_diff, cache_v_diff, cache_k_fox, cache_v_fox, cache_logf_fox, page_table,
           g_norm_attn, w_in, b_f, g_q_diff, g_k_diff, g_q_fox, g_k_fox, lambda_q1, lambda_k1, lambda_q2, lambda_k2,
           g_sub, w_branch_diff, w_branch_fox, w_o, g_norm_ffn, w_ffn_gate, w_ffn_up, w_ffn_down):
    weights = (g_norm_attn, w_in, b_f, g_q_diff, g_k_diff, g_q_fox, g_k_fox, lambda_q1, lambda_k1, lambda_q2,
               lambda_k2, g_sub, w_branch_diff, w_branch_fox, w_o, g_norm_ffn, w_ffn_gate, w_ffn_up, w_ffn_down)
    caches = (cache_k_diff, cache_v_diff, cache_k_fox, cache_v_fox, cache_logf_fox)
    depth = w_in.shape[0]
    xp, xs = x_prompt, x_sample
    new_p, new_s = [], []
    for l in range(depth):
        xp, xs, np_l, ns_l = _layer(l, xp, xs, caches, page_table, weights)
        new_p.append(np_l)
        new_s.append(ns_l)
    stack = lambda lst, i: jnp.stack([t[i] for t in lst], axis=0)
    return (xp, xs) + tuple(stack(new_p, i) for i in range(5)) + tuple(stack(new_s, i) for i in range(5))
```

```python
import functools
import math

import jax
import jax.numpy as jnp
from jax import lax
from jax.experimental import pallas as pl
from jax.experimental.pallas import tpu as pltpu

N_HEADS = 8
HEAD_DIM = 128
DIFF_HALF = HEAD_DIM // 2
WIDTH = N_HEADS * HEAD_DIM
ROPE_THETA = 10000.0
EPS = 1e-6
PAGE_SIZE = 128
LANES = 128
NEG = -1e30
MIB = 1024 * 1024

F32 = jnp.float32
BF16 = jnp.bfloat16
HIGHEST = lax.Precision.HIGHEST


def _params(semantics, vmem_mib):
    return pltpu.CompilerParams(dimension_semantics=semantics, vmem_limit_bytes=vmem_mib * MIB)


def _dot(a, b):
    return jnp.dot(a, b, preferred_element_type=F32)


def _dot_nt(a, b, precision=None):
    return lax.dot_general(a, b, (((1,), (1,)), ((), ())), preferred_element_type=F32, precision=precision)


def _lane_iota(shape):
    return lax.broadcasted_iota(jnp.int32, shape, len(shape) - 1)


def _rms_rope_head(a, g, cos, sin_signed, lane):
    sq = a * a
    lo = lane < DIFF_HALF
    s_lo = jnp.sum(jnp.where(lo, sq, 0.0), axis=-1, keepdims=True)
    s_hi = jnp.sum(jnp.where(lo, 0.0, sq), axis=-1, keepdims=True)
    ms = jnp.where(lo, s_lo, s_hi) * (1.0 / DIFF_HALF)
    y = a * lax.rsqrt(ms + EPS) * g
    first = (lane & (DIFF_HALF - 1)) < (DIFF_HALF // 2)
    rot = jnp.where(first, pltpu.roll(y, LANES - DIFF_HALF // 2, 1), pltpu.roll(y, DIFF_HALF // 2, 1))
    return y * cos + rot * sin_signed


def _rms_head(a, g):
    ms = jnp.mean(a * a, axis=-1, keepdims=True)
    return a * lax.rsqrt(ms + EPS) * g


def _log_sigmoid(z):
    return -(jnp.maximum(-z, 0.0) + jnp.log1p(jnp.exp(-jnp.abs(z))))


def _sigmoid(z):
    return 1.0 / (1.0 + jnp.exp(-z))


def _proj_kernel(x_ref, gn_ref, wqkv_ref, wg_ref, wf_ref, bf_ref, gqd_ref, gkd_ref, gqf_ref, gkf_ref, cos_ref, sin_ref,
                 int_ref, kd_ref, vd_ref, kf_ref, vf_ref, kdb_ref, vdb_ref, kfb_ref, vfb_ref, logf_ref, xn_ref):
    j = pl.program_id(1)
    tm = x_ref.shape[0]

    @pl.when(j == 0)
    def _():
        x = x_ref[...]
        ms = jnp.mean(x * x, axis=-1, keepdims=True)
        xn_ref[...] = (x * lax.rsqrt(ms + EPS) * gn_ref[...]).astype(BF16)
        z = _dot(xn_ref[...], wf_ref[...]) + bf_ref[...]
        logf_ref[...] = _log_sigmoid(z)[:, :N_HEADS]

    lane = _lane_iota((tm, LANES))
    pair = 2 * HEAD_DIM

    def head_pairs(w_ref):
        for c in range(WIDTH // pair):
            acc = _dot(xn_ref[...], w_ref[:, c * pair:(c + 1) * pair])
            for k in range(2):
                yield 2 * c + k, acc[:, k * HEAD_DIM:(k + 1) * HEAD_DIM]

    def heads(fn, out_ref, w_ref=wqkv_ref):
        for h, a in head_pairs(w_ref):
            out_ref[:, h * HEAD_DIM:(h + 1) * HEAD_DIM] = fn(a)

    def heads_kv(fn, out_ref, bf_out_ref):
        for h, a in head_pairs(wqkv_ref):
            val = fn(a)
            out_ref[pl.ds(h, tm, stride=N_HEADS), :] = val
            bf_out_ref[:, h * HEAD_DIM:(h + 1) * HEAD_DIM] = val.astype(BF16)

    @pl.when(j == 0)
    def _():
        heads(lambda a: _rms_rope_head(a, gqd_ref[...], cos_ref[...], sin_ref[...], lane), int_ref)

    @pl.when(j == 1)
    def _():
        heads(lambda a: _rms_head(a, gqf_ref[...]), int_ref)

    @pl.when((j >= 2) & (j < 6))
    def _():
        heads(_sigmoid, int_ref, wg_ref)

    @pl.when(j == 6)
    def _():
        heads_kv(lambda a: _rms_rope_head(a, gkd_ref[...], cos_ref[...], sin_ref[...], lane), kd_ref, kdb_ref)

    @pl.when(j == 7)
    def _():
        heads_kv(lambda a: a, vd_ref, vdb_ref)

    @pl.when(j == 8)
    def _():
        heads_kv(lambda a: _rms_head(a, gkf_ref[...]), kf_ref, kfb_ref)

    @pl.when(j == 9)
    def _():
        heads_kv(lambda a: a, vf_ref, vfb_ref)


def _proj(x, gn, w_qkv, w_g, w_f, b_f, gqd, gkd, gqf, gkf, cos, sin_signed, *, tm, rows_per_table):
    m, d = x.shape
    n_int = 6
    assert w_g.shape[1] == 4 * WIDTH

    def qkv_tile(j):
        return jnp.where(j == 0, 0, jnp.where(j <= 5, 3, jnp.where(j == 6, 1, jnp.where(j == 7, 2, j - 4))))

    n_tab = rows_per_table // tm
    row = lambda i, j: (i, 0)
    const = lambda i, j: (0, 0)
    kv_spec = pl.BlockSpec((tm * N_HEADS, HEAD_DIM), row)
    kv_shape = jax.ShapeDtypeStruct((m * N_HEADS, HEAD_DIM), F32)
    kvb_spec = pl.BlockSpec((tm, WIDTH), row)
    kvb_shape = jax.ShapeDtypeStruct((m, WIDTH), BF16)
    out_shape = (
        jax.ShapeDtypeStruct((m, n_int * WIDTH), F32),
        kv_shape, kv_shape, kv_shape, kv_shape,
        kvb_shape, kvb_shape, kvb_shape, kvb_shape,
        jax.ShapeDtypeStruct((m, N_HEADS), F32),
    )
    return pl.pallas_call(
        _proj_kernel,
        grid=(m // tm, n_int + 4),
        in_specs=[
            pl.BlockSpec((tm, d), row),
            pl.BlockSpec((1, d), const),
            pl.BlockSpec((d, WIDTH), lambda i, j: (0, qkv_tile(j))),
            pl.BlockSpec((d, WIDTH), lambda i, j: (0, jnp.clip(j - 2, 0, 3))),
            pl.BlockSpec((d, LANES), const),
            pl.BlockSpec((1, LANES), const),
            pl.BlockSpec((1, LANES), const),
            pl.BlockSpec((1, LANES), const),
            pl.BlockSpec((1, LANES), const),
            pl.BlockSpec((1, LANES), const),
            pl.BlockSpec((tm, LANES), lambda i, j: (i % n_tab, 0)),
            pl.BlockSpec((tm, LANES), lambda i, j: (i % n_tab, 0)),
        ],
        out_specs=(
            pl.BlockSpec((tm, WIDTH), lambda i, j: (i, jnp.minimum(j, n_int - 1))),
            kv_spec, kv_spec, kv_spec, kv_spec,
            kvb_spec, kvb_spec, kvb_spec, kvb_spec,
            pl.BlockSpec((tm, N_HEADS), row),
        ),
        out_shape=out_shape,
        scratch_shapes=[pltpu.VMEM((tm, d), BF16)],
        compiler_params=_params(("arbitrary", "arbitrary"), 60),
        name="proj",
    )(x, gn, w_qkv, w_g, w_f, b_f, gqd, gkd, gqf, gkf, cos, sin_signed)


def _pad_lanes(chunk, lane):
    out = jnp.zeros(lane.shape, F32)
    for h in range(N_HEADS):
        out = jnp.where(lane == h, chunk[:, h:h + 1], out)
    return out


def _cumsum_kernel(lf_ref, cum_ref, cumt_ref):
    s = lf_ref.shape[0]
    r = lax.broadcasted_iota(jnp.int32, (LANES, LANES), 0)
    c = lax.broadcasted_iota(jnp.int32, (LANES, LANES), 1)
    tri = (c <= r).astype(F32)
    carry = jnp.zeros((1, LANES), F32)
    for ci in range(s // LANES):
        rows = slice(ci * LANES, (ci + 1) * LANES)
        pad = _pad_lanes(lf_ref[rows, :], c)
        res = jnp.dot(tri, pad, preferred_element_type=F32, precision=HIGHEST) + carry
        cum_ref[rows, :] = res[:, :N_HEADS]
        cumt_ref[0, :, rows] = res.T[:N_HEADS, :]
        carry = res[LANES - 1:LANES, :]


def _cumsum(logf, batch, seq):
    return pl.pallas_call(
        _cumsum_kernel,
        grid=(batch,),
        in_specs=[pl.BlockSpec((seq, N_HEADS), lambda b: (b, 0))],
        out_specs=(
            pl.BlockSpec((seq, N_HEADS), lambda b: (b, 0)),
            pl.BlockSpec((1, N_HEADS, seq), lambda b: (b, 0, 0)),
        ),
        out_shape=(
            jax.ShapeDtypeStruct((batch * seq, N_HEADS), F32),
            jax.ShapeDtypeStruct((batch, N_HEADS, seq), F32),
        ),
        compiler_params=_params(("arbitrary",), 32),
        name="cumsum",
    )(logf)


ROWS = 2 * N_HEADS
PAGE_COLS = PAGE_SIZE * N_HEADS
N_DEC_SMALL = 7


def _split3(a):
    hi = a.astype(BF16)
    r1 = a - hi.astype(F32)
    mid = r1.astype(BF16)
    lo = (r1 - mid.astype(F32)).astype(BF16)
    return hi, mid, lo


def _stack2(a):
    return jnp.concatenate([a, a], axis=0)


def _lambda_value(lq1_ref, lk1_ref, lq2_ref, lk2_ref, lam_init):
    a = jnp.sum(lq1_ref[...] * lk1_ref[...], axis=-1, keepdims=True)
    b = jnp.sum(lq2_ref[...] * lk2_ref[...], axis=-1, keepdims=True)
    return jnp.exp(a) - jnp.exp(b) + lam_init


def _sub_norm(o, g, lam_init):
    ms = jnp.mean(o * o, axis=-1, keepdims=True)
    return o * lax.rsqrt(ms + EPS) * g * (1.0 - lam_init)


def _decode_step(p, n_steps, very_first, lam_refs, gsub_ref, small_refs, page_refs, out_refs, scratch_refs,
                 *, lam_init, group):
    qd_ref, qf_ref, kdn_ref, vdn_ref, kfn_ref, vfn_ref, lfn_ref = small_refs
    kd_refs, vd_refs, kf_refs, vf_refs, lf_refs = (page_refs[i * group:(i + 1) * group] for i in range(5))
    od_ref, of_ref = out_refs
    qs_ref, m_ref, l_ref, acc_ref, carry_ref, later_ref, own_ref = scratch_refs
    cols = group * PAGE_COLS

    @pl.when(very_first)
    def _():
        ks = lax.broadcasted_iota(jnp.int32, (PAGE_SIZE, PAGE_COLS), 0)
        kc = lax.broadcasted_iota(jnp.int32, (PAGE_SIZE, PAGE_COLS), 1)
        later_ref[...] = (ks > (kc >> 3)).astype(BF16)
        r = lax.broadcasted_iota(jnp.int32, (2 * ROWS, cols), 0)
        c = lax.broadcasted_iota(jnp.int32, (2 * ROWS, cols), 1)
        own_ref[...] = jnp.where((r & (N_HEADS - 1)) == (c & (N_HEADS - 1)), 0.0, NEG)

    @pl.when(p == 0)
    def _():
        lane = _lane_iota((N_HEADS, HEAD_DIM))
        q = qd_ref[0] * (DIFF_HALF ** -0.5)
        zeros = jnp.zeros((N_HEADS, HEAD_DIM), F32)
        q_d = jnp.concatenate([jnp.where(lane < DIFF_HALF, q, 0.0), jnp.where(lane < DIFF_HALF, 0.0, q)], axis=0)
        q_f = jnp.concatenate([qf_ref[0] * (HEAD_DIM ** -0.5), zeros], axis=0)
        qs_ref[:ROWS, :HEAD_DIM] = q_d.astype(BF16)
        qs_ref[:ROWS, HEAD_DIM:] = jnp.zeros((ROWS, HEAD_DIM), BF16)
        qs_ref[ROWS:, :HEAD_DIM] = jnp.zeros((ROWS, HEAD_DIM), BF16)
        qs_ref[ROWS:, HEAD_DIM:] = q_f.astype(BF16)
        m_ref[:ROWS] = jnp.sum(q_d * _stack2(kdn_ref[0]), axis=-1, keepdims=True)
        m_ref[ROWS:] = jnp.sum(q_f * _stack2(kfn_ref[0]), axis=-1, keepdims=True)
        l_ref[...] = jnp.ones(l_ref.shape, F32)
        acc_ref[...] = jnp.concatenate([_stack2(vdn_ref[0]), _stack2(vfn_ref[0])], axis=0)
        carry_ref[...] = _stack2(lfn_ref[0])

    def both(d_refs, f_refs):
        flat = lambda refs: jnp.concatenate(
            [ref[...].reshape(PAGE_COLS, HEAD_DIM).astype(BF16) for ref in refs], axis=0)
        return jnp.concatenate([flat(d_refs), flat(f_refs)], axis=1)

    pages_lf = [_stack2(lf[...]) for lf in lf_refs]
    within = _dot(jnp.concatenate([t for page_lf in pages_lf for t in _split3(page_lf)], axis=0), later_ref[...])
    carry = carry_ref[...]
    biases = []
    for g, page_lf in enumerate(pages_lf):
        w = within[3 * g * ROWS:3 * (g + 1) * ROWS]
        biases.append(carry + w[:ROWS] + w[ROWS:2 * ROWS] + w[2 * ROWS:])
        carry = carry + jnp.sum(page_lf, axis=-1, keepdims=True)
    carry_ref[...] = carry

    halves = 2 if group % 2 == 0 else 1
    per = group // halves
    part = lambda refs, i: refs[i * per:(i + 1) * per]
    logits = [_dot_nt(qs_ref[...], both(part(kd_refs, i), part(kf_refs, i))) for i in range(halves)]
    m, l, acc = m_ref[...], l_ref[...], acc_ref[...]
    for i in range(halves):
        bias = jnp.concatenate(biases[i * per:(i + 1) * per], axis=1)
        s = jnp.concatenate([logits[i][:ROWS], logits[i][ROWS:] + bias], axis=0) + own_ref[:, :per * PAGE_COLS]
        m_new = jnp.maximum(m, jnp.max(s, axis=-1, keepdims=True))
        alpha = jnp.exp(m - m_new)
        pr = jnp.exp(s - m_new)
        l = alpha * l + jnp.sum(pr, axis=-1, keepdims=True)
        pv = _dot(pr.astype(BF16), both(part(vd_refs, i), part(vf_refs, i)))
        acc = alpha * acc + jnp.concatenate([pv[:ROWS, :HEAD_DIM], pv[ROWS:, HEAD_DIM:]], axis=0)
        m = m_new
    m_ref[...], l_ref[...], acc_ref[...] = m, l, acc

    @pl.when(p == n_steps - 1)
    def _():
        lam = _lambda_value(*lam_refs, lam_init)
        o = acc_ref[...] / l_ref[...]
        od_ref[0] = _sub_norm(o[:N_HEADS] - lam * o[N_HEADS:ROWS], gsub_ref[...], lam_init)
        of_ref[0] = o[ROWS:ROWS + N_HEADS]


def _decode_scratch(group):
    return [
        pltpu.VMEM((2 * ROWS, 2 * HEAD_DIM), BF16),
        pltpu.VMEM((2 * ROWS, 1), F32),
        pltpu.VMEM((2 * ROWS, 1), F32),
        pltpu.VMEM((2 * ROWS, HEAD_DIM), F32),
        pltpu.VMEM((ROWS, 1), F32),
        pltpu.VMEM((PAGE_SIZE, PAGE_COLS), BF16),
        pltpu.VMEM((2 * ROWS, group * PAGE_COLS), F32),
    ]


class _DecodePlan:
    def __init__(self, page_table, qd, qf, kd_new, vd_new, kf_new, vf_new, lf_new,
                 cache_kd, cache_vd, cache_kf, cache_vf, cache_lf, *, group, step_of, first_step, n_steps):
        n_samples, n_pages = page_table.shape
        assert n_pages % group == 0
        spp = n_pages // group
        assert first_step % spp == 0 and n_steps % spp == 0
        self.group, self.spp, self.first_step, self.n_steps = group, spp, first_step, n_steps
        self.pt_flat = page_table.reshape(-1)
        tile3 = lambda a: a.reshape(n_samples, N_HEADS, HEAD_DIM)
        first_sample = first_step // spp
        sample = lambda *ids: step_of(*ids) // spp
        page = lambda g: (lambda *ids_pt: ids_pt[-1][
            sample(*ids_pt[:-1]) * n_pages + n_pages - 1 - ((step_of(*ids_pt[:-1]) % spp) * group + g)])
        tile_spec = pl.BlockSpec((1, N_HEADS, HEAD_DIM), lambda *a: (sample(*a[:-1]), 0, 0))
        lfn_spec = pl.BlockSpec((1, N_HEADS, 1), lambda *a: (sample(*a[:-1]), 0, 0))
        page_specs = lambda: [pl.BlockSpec((None, PAGE_SIZE, N_HEADS, HEAD_DIM),
                                           lambda *a, f=page(g): (f(*a), 0, 0, 0)) for g in range(group)]
        lf_specs = [pl.BlockSpec((None, N_HEADS, PAGE_SIZE), lambda *a, f=page(g): (f(*a), 0, 0))
                    for g in range(group)]
        cache_lf_t = jnp.swapaxes(cache_lf, 1, 2)
        self.inputs = [tile3(qd), tile3(qf), tile3(kd_new), tile3(vd_new), tile3(kf_new), tile3(vf_new),
                       lf_new.reshape(n_samples, N_HEADS, 1)] + [cache_kd] * group + [cache_vd] * group \
            + [cache_kf] * group + [cache_vf] * group + [cache_lf_t] * group
        self.in_specs = [tile_spec] * 6 + [lfn_spec] + page_specs() + page_specs() + page_specs() + page_specs() \
            + lf_specs
        out_spec = pl.BlockSpec((1, N_HEADS, HEAD_DIM), lambda *a: (sample(*a[:-1]) - first_sample, 0, 0))
        self.out_specs = [out_spec, out_spec]
        out_shape = jax.ShapeDtypeStruct((n_steps // spp, N_HEADS, HEAD_DIM), F32)
        self.out_shapes = [out_shape, out_shape]
        self.n_in = len(self.inputs)


def _decode_kernel(pt_ref, lq1_ref, lk1_ref, lq2_ref, lk2_ref, gsub_ref, *refs, lam_init, group, spp):
    del pt_ref
    n_in = N_DEC_SMALL + 5 * group
    t = pl.program_id(0)
    _decode_step(t % spp, spp, t == 0, (lq1_ref, lk1_ref, lq2_ref, lk2_ref), gsub_ref,
                 refs[:N_DEC_SMALL], refs[N_DEC_SMALL:n_in], refs[n_in:n_in + 2], refs[n_in + 2:],
                 lam_init=lam_init, group=group)


def _decode(plan, lams, g_sub, *, n_steps, lam_init):
    small = lambda n: pl.BlockSpec((1, n), lambda t, pt: (0, 0))
    grid_spec = pltpu.PrefetchScalarGridSpec(
        num_scalar_prefetch=1,
        grid=(n_steps,),
        in_specs=[small(DIFF_HALF)] * 4 + [small(LANES)] + plan.in_specs,
        out_specs=tuple(plan.out_specs),
        scratch_shapes=_decode_scratch(plan.group),
    )
    return pl.pallas_call(
        functools.partial(_decode_kernel, lam_init=lam_init, group=plan.group, spp=plan.spp),
        grid_spec=grid_spec,
        out_shape=tuple(plan.out_shapes),
        compiler_params=_params(("arbitrary",), 52),
        name="decode",
    )(plan.pt_flat, *lams, g_sub, *plan.inputs)


def _flash_update(s, cq, v_bf, m_ref, l_ref, acc_ref, h):
    reps = s.shape[1] // LANES
    m_prev = m_ref[h]
    m_curr = jnp.max(s, axis=-1, keepdims=True)
    if cq is not None:
        m_curr = m_curr + cq
    m_new = jnp.maximum(m_prev, m_curr)
    alpha = jnp.exp(m_prev - m_new)
    shift = m_new if cq is None else m_new - cq
    p = jnp.exp(s - jnp.tile(shift, (1, reps)))
    part = p[:, :LANES]
    for j in range(1, reps):
        part = part + p[:, j * LANES:(j + 1) * LANES]
    l_ref[h] = alpha * l_ref[h] + part
    acc_ref[h] = alpha * acc_ref[h] + _dot(p.astype(BF16), v_bf)
    m_ref[h] = m_new


def _flash_result(l_ref, acc_ref, h):
    return acc_ref[h] / jnp.sum(l_ref[h], axis=-1, keepdims=True)


def _causal_mask(s, qi, ki, tq, tk):
    rows = lax.broadcasted_iota(jnp.int32, s.shape, 0)
    qpos = qi * tq + jnp.where(rows >= tq, rows - tq, rows)
    kpos = ki * tk + lax.broadcasted_iota(jnp.int32, s.shape, 1)
    return jnp.where(kpos <= qpos, s, NEG)


def _attn_kernel(pt_ref, lq1_ref, lk1_ref, lq2_ref, lk2_ref, gsub_ref, q_ref, k_ref, v_ref, *refs,
                 diff, lam_init, dec):
    del pt_ref
    b, pair, j = pl.program_id(0), pl.program_id(1), pl.program_id(2)
    n_pairs, n_j = pl.num_programs(1), pl.num_programs(2)
    tq, tk = q_ref.shape[0], k_ref.shape[0]
    qi, ki = _folded(pair, j, n_j - 1)
    lam_refs = (lq1_ref, lk1_ref, lq2_ref, lk2_ref)
    n_extra = 0 if diff else 2
    n_dec_in = (N_DEC_SMALL + 5 * dec[0]) if dec else 0
    extra = refs[:n_extra]
    dec_in = refs[n_extra:n_extra + n_dec_in]
    outs = refs[n_extra + n_dec_in:n_extra + n_dec_in + (3 if dec else 1)]
    scratch = refs[n_extra + n_dec_in + len(outs):]
    o_ref = outs[0]
    if diff:
        qs_ref, m_ref, l_ref, acc_ref = scratch[:4]
        dec_scratch = scratch[4:]
    else:
        cq_ref, ckt_ref = extra
        qs_ref, cqr_ref, m_ref, l_ref, acc_ref = scratch[:5]
        dec_scratch = scratch[5:]

    def decode_step():
        if not dec:
            return
        group, spp, first_step, n_steps = dec
        local = (b * n_pairs + pair) * n_j + j

        @pl.when(local < n_steps)
        def _():
            _decode_step((first_step + local) % spp, spp, local == 0, lam_refs, gsub_ref, dec_in[:N_DEC_SMALL],
                         dec_in[N_DEC_SMALL:], outs[1:], dec_scratch, lam_init=lam_init, group=group)

    @pl.when(ki == 0)
    def _():
        if diff:
            lane = _lane_iota((tq, LANES))
            scale = DIFF_HALF ** -0.5
            for h in range(N_HEADS):
                qh = q_ref[:, h * HEAD_DIM:(h + 1) * HEAD_DIM] * scale
                qs_ref[h, :tq, :] = jnp.where(lane < DIFF_HALF, qh, 0.0).astype(BF16)
                qs_ref[h, tq:, :] = jnp.where(lane < DIFF_HALF, 0.0, qh).astype(BF16)
        else:
            qs_ref[...] = (q_ref[...] * (HEAD_DIM ** -0.5)).astype(BF16)
            for h in range(N_HEADS):
                cqr_ref[h] = jnp.broadcast_to(cq_ref[:, h:h + 1], (tq, LANES))
        m_ref[...] = jnp.full(m_ref.shape, NEG, F32)
        l_ref[...] = jnp.zeros(l_ref.shape, F32)
        acc_ref[...] = jnp.zeros(acc_ref.shape, F32)

    def step(masked):
        for h in range(N_HEADS):
            sl = slice(h * HEAD_DIM, (h + 1) * HEAD_DIM)
            kh = k_ref[:, sl]
            if diff:
                s = _dot_nt(qs_ref[h], kh)
                cq = None
            else:
                s = _dot_nt(qs_ref[:, sl], kh) - ckt_ref[0, h:h + 1, :]
                cq = cqr_ref[h]
            if masked:
                s = _causal_mask(s, qi, ki, tq, tk)
            _flash_update(s, cq, v_ref[:, sl], m_ref, l_ref, acc_ref, h)

    @pl.when(ki < qi)
    def _():
        step(False)
        decode_step()

    @pl.when(ki == qi)
    def _():
        step(True)
        if diff:
            lam = _lambda_value(*lam_refs, lam_init)
        for h in range(N_HEADS):
            o = _flash_result(l_ref, acc_ref, h)
            if diff:
                o = _sub_norm(o[:tq] - lam * o[tq:], gsub_ref[...], lam_init)
            o_ref[:, h * HEAD_DIM:(h + 1) * HEAD_DIM] = o.astype(o_ref.dtype)
        decode_step()


def _folded(pair, j, nq):
    first = j <= pair
    return jnp.where(first, pair, nq - 1 - pair), jnp.where(first, j, j - pair - 1)


def _attention(lams, g_sub, q_arr, q_col, k, v, fox_bias, plan, *, diff, batch, seq, tq, tk, lam_init, name):
    assert tq == tk and (seq // tq) % 2 == 0
    nq = seq // tq
    qrow = lambda b, r, j: b * nq + _folded(r, j, nq)[0]
    krow = lambda b, r, j: b * nq + _folded(r, j, nq)[1]
    small = lambda n: pl.BlockSpec((1, n), lambda b, r, j, pt: (0, 0))
    q_spec = pl.BlockSpec((tq, WIDTH), lambda b, r, j, pt: (qrow(b, r, j), q_col))
    kv_spec = pl.BlockSpec((tk, WIDTH), lambda b, r, j, pt: (krow(b, r, j), 0))
    o_spec = pl.BlockSpec((tq, WIDTH), lambda b, r, j, pt: (qrow(b, r, j), 0))
    in_specs = [small(DIFF_HALF)] * 4 + [small(LANES), q_spec, kv_spec, kv_spec]
    inputs = [*lams, g_sub, q_arr, k, v]
    rows = 2 * tq if diff else tq
    scratch = [pltpu.VMEM((N_HEADS, rows, HEAD_DIM), BF16) if diff else pltpu.VMEM((tq, WIDTH), BF16)]
    if not diff:
        in_specs += [pl.BlockSpec((tq, N_HEADS), lambda b, r, j, pt: (qrow(b, r, j), 0)),
                     pl.BlockSpec((1, N_HEADS, tk), lambda b, r, j, pt: (b, 0, _folded(r, j, nq)[1]))]
        inputs += list(fox_bias)
        scratch.append(pltpu.VMEM((N_HEADS, tq, LANES), F32))
    scratch += [pltpu.VMEM((N_HEADS, rows, LANES), F32), pltpu.VMEM((N_HEADS, rows, LANES), F32),
                pltpu.VMEM((N_HEADS, rows, HEAD_DIM), F32)]
    out_specs = [o_spec]
    out_shapes = [jax.ShapeDtypeStruct((batch * seq, WIDTH), BF16)]
    dec = None
    pt = jnp.zeros((1,), jnp.int32)
    if plan is not None:
        in_specs += plan.in_specs
        inputs += plan.inputs
        out_specs += plan.out_specs
        out_shapes += plan.out_shapes
        scratch += _decode_scratch(plan.group)
        dec = (plan.group, plan.spp, plan.first_step, plan.n_steps)
        pt = plan.pt_flat
    grid_spec = pltpu.PrefetchScalarGridSpec(
        num_scalar_prefetch=1, grid=(batch, nq // 2, nq + 1), in_specs=in_specs, out_specs=tuple(out_specs),
        scratch_shapes=scratch)
    return pl.pallas_call(
        functools.partial(_attn_kernel, diff=diff, lam_init=lam_init, dec=dec),
        grid_spec=grid_spec,
        out_shape=tuple(out_shapes),
        compiler_params=_params(("arbitrary", "arbitrary", "arbitrary"), 58),
        name=name,
    )(pt, *inputs)


def _merge_kernel(od_ref, of_ref, sgd_ref, sgf_ref, x_ref, wbd_ref, wbf_ref, wo_ref, h_ref, mg_ref, *, n_col):
    j = pl.program_id(1)

    @pl.when(j < n_col)
    def _():
        a = _dot(od_ref[...].astype(BF16), wbd_ref[...])
        b = _dot(of_ref[...].astype(BF16), wbf_ref[...])
        mg_ref[j] = (sgd_ref[...] * a + sgf_ref[...] * b).astype(BF16)

    @pl.when(j >= n_col)
    def _():
        merged = jnp.concatenate([mg_ref[c] for c in range(n_col)], axis=1)
        h_ref[...] = x_ref[...] + _dot(merged, wo_ref[...])


def _merge(od, of, internal, x, w_bd, w_bf, w_o, *, tm, tn):
    m, d = x.shape
    n_col = d // tn
    gate_d = 2 * WIDTH // tn
    first = lambda i, j: jnp.minimum(j, n_col - 1)
    second = lambda i, j: (i, jnp.maximum(j - n_col, 0))
    return pl.pallas_call(
        functools.partial(_merge_kernel, n_col=n_col),
        grid=(m // tm, 2 * n_col),
        in_specs=[
            pl.BlockSpec((tm, WIDTH), lambda i, j: (i, 0)),
            pl.BlockSpec((tm, WIDTH), lambda i, j: (i, 0)),
            pl.BlockSpec((tm, tn), lambda i, j: (i, gate_d + first(i, j))),
            pl.BlockSpec((tm, tn), lambda i, j: (i, gate_d + n_col + first(i, j))),
            pl.BlockSpec((tm, tn), second),
            pl.BlockSpec((WIDTH, tn), lambda i, j: (0, first(i, j))),
            pl.BlockSpec((WIDTH, tn), lambda i, j: (0, first(i, j))),
            pl.BlockSpec((d, tn), lambda i, j: (0, jnp.maximum(j - n_col, 0))),
        ],
        out_specs=pl.BlockSpec((tm, tn), second),
        out_shape=jax.ShapeDtypeStruct((m, d), F32),
        scratch_shapes=[pltpu.VMEM((n_col, tm, tn), BF16)],
        compiler_params=_params(("arbitrary", "arbitrary"), 48),
        name="merge",
    )(od, of, internal, internal, x, w_bd, w_bf, w_o)


def _ffn_kernel(h_ref, g_ref, wg_ref, wu_ref, wd_ref, o_ref, hn_ref):
    @pl.when(pl.program_id(1) == 0)
    def _():
        h = h_ref[...]
        ms = jnp.mean(h * h, axis=-1, keepdims=True)
        hn_ref[...] = (h * lax.rsqrt(ms + EPS) * g_ref[...]).astype(BF16)
        o_ref[...] = h

    hn = hn_ref[...]
    a = _dot(hn, wg_ref[...])
    u = _dot(hn, wu_ref[...])
    ff = (a * _sigmoid(a) * u).astype(BF16)
    o_ref[...] += _dot(ff, wd_ref[...])


def _ffn(h, g, w_gate, w_up, w_down, *, tm, tf):
    m, d = h.shape
    f = w_gate.shape[1]
    return pl.pallas_call(
        _ffn_kernel,
        grid=(m // tm, f // tf),
        in_specs=[
            pl.BlockSpec((tm, d), lambda i, j: (i, 0)),
            pl.BlockSpec((1, d), lambda i, j: (0, 0)),
            pl.BlockSpec((d, tf), lambda i, j: (0, j)),
            pl.BlockSpec((d, tf), lambda i, j: (0, j)),
            pl.BlockSpec((tf, d), lambda i, j: (j, 0)),
        ],
        out_specs=pl.BlockSpec((tm, d), lambda i, j: (i, 0)),
        out_shape=jax.ShapeDtypeStruct((m, d), F32),
        scratch_shapes=[pltpu.VMEM((tm, d), BF16)],
        compiler_params=_params(("arbitrary", "arbitrary"), 58),
        name="ffn",
    )(h, g, w_gate, w_up, w_down)


def _rope_tables(pos):
    half = DIFF_HALF // 2
    inv = ROPE_THETA ** (-jnp.arange(half, dtype=F32) / half)
    ang = pos.astype(F32)[:, None] * inv[None, :]
    cos = jnp.concatenate([jnp.cos(ang)] * 4, axis=-1)
    sin = jnp.sin(ang)
    sin_signed = jnp.concatenate([-sin, sin, -sin, sin], axis=-1)
    return cos, sin_signed


def _tile(m, pref):
    return pref if m % pref == 0 else m


def _layer(l, xp, xs, caches, page_table, weights):
    (g_norm_attn, w_in, b_f, g_q_diff, g_k_diff, g_q_fox, g_k_fox, lambda_q1, lambda_k1, lambda_q2, lambda_k2,
     g_sub, w_branch_diff, w_branch_fox, w_o, g_norm_ffn, w_ffn_gate, w_ffn_up, w_ffn_down) = [w[l] for w in weights]
    batch, seq, d = xp.shape
    n_samples, dec_seq, _ = xs.shape
    past_len = page_table.shape[1] * PAGE_SIZE
    lam_init = 0.8 - 0.6 * math.exp(-0.3 * l)

    n_qkv = 6 * WIDTH
    w_qkv = w_in[:, :n_qkv].astype(BF16)
    w_gates = w_in[:, n_qkv + N_HEADS:].astype(BF16)
    w_f = jnp.pad(w_in[:, n_qkv:n_qkv + N_HEADS], ((0, 0), (0, LANES - N_HEADS))).astype(BF16)
    b_f_pad = jnp.pad(b_f, (0, LANES - N_HEADS)).reshape(1, LANES)
    two = lambda g: jnp.concatenate([g, g]).reshape(1, LANES)
    one = lambda g: g.reshape(1, -1)
    lams = tuple(one(v) for v in (lambda_q1, lambda_k1, lambda_q2, lambda_k2))
    w_bd, w_bf, w_out = w_branch_diff.astype(BF16), w_branch_fox.astype(BF16), w_o.astype(BF16)
    w_g, w_u, w_d = w_ffn_gate.astype(BF16), w_ffn_up.astype(BF16), w_ffn_down.astype(BF16)

    def project(x2d, pos, tm, rows_per_table):
        cos, sin_signed = _rope_tables(pos)
        return _proj(x2d, one(g_norm_attn), w_qkv, w_gates, w_f, b_f_pad, two(g_q_diff), two(g_k_diff), one(g_q_fox),
                     one(g_k_fox), cos, sin_signed, tm=tm, rows_per_table=rows_per_table)

    def finish(x2d, od, of, internal, tm_merge, tm_ffn):
        h = _merge(od, of, internal, x2d, w_bd, w_bf, w_out, tm=tm_merge, tn=512)
        return _ffn(h, one(g_norm_ffn), w_g, w_u, w_d, tm=tm_ffn, tf=512)

    assert dec_seq == 1
    xp2 = xp.reshape(batch * seq, d)
    xs2 = xs.reshape(n_samples, d)
    int_p, kd_p, vd_p, kf_p, vf_p, kdb_p, vdb_p, kfb_p, vfb_p, lf_p = project(xp2, jnp.arange(seq), _tile(seq, 512), seq)
    pos_s = jnp.full((n_samples,), past_len, jnp.int32)
    int_s, kd_s, vd_s, kf_s, vf_s, _, _, _, _, lf_s = project(xs2, pos_s, n_samples, n_samples)
    cum, cumt = _cumsum(lf_p, batch, seq)

    n_pages = page_table.shape[1]
    tq = tk = _tile(seq, 256)
    nq = seq // tq
    n_host = batch * (nq // 2) * (nq + 1)
    group = math.gcd(8, n_pages)
    spp = n_pages // group
    n_half = (n_samples // 2) * spp
    hosted = n_samples % 2 == 0 and n_half <= n_host
    dec_args = (page_table, int_s[:, :WIDTH], int_s[:, WIDTH:2 * WIDTH], kd_s, vd_s, kf_s, vf_s, lf_s,
                *[c[l] for c in caches])

    def host_step(first):
        return lambda b, r, j: first + jnp.minimum((b * (nq // 2) + r) * (nq + 1) + j, n_half - 1)

    plans = [_DecodePlan(*dec_args, group=group, step_of=host_step(f), first_step=f, n_steps=n_half)
             for f in (0, n_half)] if hosted else [None, None]
    attn = functools.partial(_attention, lams, one(g_sub), int_p, batch=batch, seq=seq, tq=tq, tk=tk,
                             lam_init=lam_init)
    out_d = attn(0, kdb_p, vdb_p, None, plans[0], diff=True, name="diff_attention")
    out_f = attn(1, kfb_p, vfb_p, (cum, cumt), plans[1], diff=False, name="fox_attention")
    if hosted:
        od_s = jnp.concatenate([out_d[1], out_f[1]], axis=0).reshape(n_samples, WIDTH)
        of_s = jnp.concatenate([out_d[2], out_f[2]], axis=0).reshape(n_samples, WIDTH)
    else:
        n_dec = n_samples * spp
        plan = _DecodePlan(*dec_args, group=group, step_of=lambda t: t, first_step=0, n_steps=n_dec)
        od_s, of_s = (o.reshape(n_samples, WIDTH) for o in _decode(plan, lams, one(g_sub), n_steps=n_dec,
                                                                  lam_init=lam_init))
    yp = finish(xp2, out_d[0], out_f[0], int_p, _tile(seq, 1024), _tile(seq, 1024))
    ys = finish(xs2, od_s, of_s, int_s, n_samples, n_samples)

    heads = lambda a, b, t: a.reshape(b, t, N_HEADS, HEAD_DIM)
    new_p = (heads(kd_p, batch, seq), heads(vd_p, batch, seq), heads(kf_p, batch, seq), heads(vf_p, batch, seq),
             lf_p.reshape(batch, seq, N_HEADS))
    new_s = (heads(kd_s, n_samples, 1), heads(vd_s, n_samples, 1), heads(kf_s, n_samples, 1),
             heads(vf_s, n_samples, 1), lf_s.reshape(n_samples, 1, N_HEADS))
    return yp.reshape(batch, seq, d), ys.reshape(n_samples, dec_seq, d), new_p, new_s


def kernel(x_prompt, x_sample, cache_k_diff, cache_v_diff, cache_k_fox, cache_v_fox, cache_logf_fox, page_table,
           g_norm_attn, w_in, b_f, g_q_diff, g_k_diff, g_q_fox, g_k_fox, lambda_q1, lambda_k1, lambda_q2, lambda_k2,
           g_sub, w_branch_diff, w_branch_fox, w_o, g_norm_ffn, w_ffn_gate, w_ffn_up, w_ffn_down):
    weights = (g_norm_attn, w_in, b_f, g_q_diff, g_k_diff, g_q_fox, g_k_fox, lambda_q1, lambda_k1, lambda_q2,
               lambda_k2, g_sub, w_branch_diff, w_branch_fox, w_o, g_norm_ffn, w_ffn_gate, w_ffn_up, w_ffn_down)
    caches = (cache_k_diff, cache_v_diff, cache_k_fox, cache_v_fox, cache_logf_fox)
    depth = w_in.shape[0]
    xp, xs = x_prompt, x_sample
    new_p, new_s = [], []
    for l in range(depth):
        xp, xs, np_l, ns_l = _layer(l, xp, xs, caches, page_table, weights)
        new_p.append(np_l)
        new_s.append(ns_l)
    stack = lambda lst, i: jnp.stack([t[i] for t in lst], axis=0)
    return (xp, xs) + tuple(stack(new_p, i) for i in range(5)) + tuple(stack(new_s, i) for i in range(5))
```

```python
import functools
import math

import jax
import jax.numpy as jnp
from jax import lax
from jax.experimental import pallas as pl
from jax.experimental.pallas import tpu as pltpu

N_HEADS = 8
HEAD_DIM = 128
DIFF_HALF = HEAD_DIM // 2
WIDTH = N_HEADS * HEAD_DIM
ROPE_THETA = 10000.0
EPS = 1e-6
PAGE_SIZE = 128
LANES = 128
NEG = -1e30
MIB = 1024 * 1024

F32 = jnp.float32
BF16 = jnp.bfloat16
HIGHEST = lax.Precision.HIGHEST


def _params(semantics, vmem_mib):
    return pltpu.CompilerParams(dimension_semantics=semantics, vmem_limit_bytes=vmem_mib * MIB)


def _dot(a, b):
    return jnp.dot(a, b, preferred_element_type=F32)


def _dot_nt(a, b, precision=None):
    return lax.dot_general(a, b, (((1,), (1,)), ((), ())), preferred_element_type=F32, precision=precision)


def _lane_iota(shape):
    return lax.broadcasted_iota(jnp.int32, shape, len(shape) - 1)


def _rms_rope_head(a, g, cos, sin_signed, lane):
    sq = a * a
    lo = lane < DIFF_HALF
    s_lo = jnp.sum(jnp.where(lo, sq, 0.0), axis=-1, keepdims=True)
    s_hi = jnp.sum(jnp.where(lo, 0.0, sq), axis=-1, keepdims=True)
    ms = jnp.where(lo, s_lo, s_hi) * (1.0 / DIFF_HALF)
    y = a * lax.rsqrt(ms + EPS) * g
    first = (lane & (DIFF_HALF - 1)) < (DIFF_HALF // 2)
    rot = jnp.where(first, pltpu.roll(y, LANES - DIFF_HALF // 2, 1), pltpu.roll(y, DIFF_HALF // 2, 1))
    return y * cos + rot * sin_signed


def _rms_head(a, g):
    ms = jnp.mean(a * a, axis=-1, keepdims=True)
    return a * lax.rsqrt(ms + EPS) * g


def _log_sigmoid(z):
    return -(jnp.maximum(-z, 0.0) + jnp.log1p(jnp.exp(-jnp.abs(z))))


def _sigmoid(z):
    return 1.0 / (1.0 + jnp.exp(-z))


def _proj_kernel(x_ref, gn_ref, wqkv_ref, wg_ref, wf_ref, bf_ref, gqd_ref, gkd_ref, gqf_ref, gkf_ref, cos_ref, sin_ref,
                 q_ref, g_ref, kd_ref, vd_ref, kf_ref, vf_ref, kdb_ref, vdb_ref, kfb_ref, vfb_ref, logf_ref, xn_ref):
    j = pl.program_id(1)
    tm = x_ref.shape[0]

    @pl.when(j == 0)
    def _():
        x = x_ref[...]
        ms = jnp.mean(x * x, axis=-1, keepdims=True)
        xn_ref[...] = (x * lax.rsqrt(ms + EPS) * gn_ref[...]).astype(BF16)
        z = _dot(xn_ref[...], wf_ref[...]) + bf_ref[...]
        logf_ref[...] = _log_sigmoid(z)[:, :N_HEADS]

    lane = _lane_iota((tm, LANES))
    pair = 2 * HEAD_DIM

    def head_pairs(w_ref):
        for c in range(WIDTH // pair):
            acc = _dot(xn_ref[...], w_ref[:, c * pair:(c + 1) * pair])
            for k in range(2):
                yield 2 * c + k, acc[:, k * HEAD_DIM:(k + 1) * HEAD_DIM]

    def heads(fn, out_ref, w_ref=wqkv_ref):
        for h, a in head_pairs(w_ref):
            out_ref[:, h * HEAD_DIM:(h + 1) * HEAD_DIM] = fn(a).astype(out_ref.dtype)

    def heads_kv(fn, out_ref, bf_out_ref):
        for h, a in head_pairs(wqkv_ref):
            val = fn(a)
            out_ref[pl.ds(h, tm, stride=N_HEADS), :] = val
            bf_out_ref[:, h * HEAD_DIM:(h + 1) * HEAD_DIM] = val.astype(BF16)

    @pl.when(j == 0)
    def _():
        heads(lambda a: _rms_rope_head(a, gqd_ref[...], cos_ref[...], sin_ref[...], lane), q_ref)

    @pl.when(j == 1)
    def _():
        heads(lambda a: _rms_head(a, gqf_ref[...]) * (HEAD_DIM ** -0.5), q_ref)

    @pl.when((j >= 2) & (j < 6))
    def _():
        heads(_sigmoid, g_ref, wg_ref)

    @pl.when(j == 6)
    def _():
        heads_kv(lambda a: _rms_rope_head(a, gkd_ref[...], cos_ref[...], sin_ref[...], lane), kd_ref, kdb_ref)

    @pl.when(j == 7)
    def _():
        heads_kv(lambda a: a, vd_ref, vdb_ref)

    @pl.when(j == 8)
    def _():
        heads_kv(lambda a: _rms_head(a, gkf_ref[...]), kf_ref, kfb_ref)

    @pl.when(j == 9)
    def _():
        heads_kv(lambda a: a, vf_ref, vfb_ref)


def _proj(x, gn, w_qkv, w_g, w_f, b_f, gqd, gkd, gqf, gkf, cos, sin_signed, *, tm, rows_per_table):
    m, d = x.shape
    assert w_g.shape[1] == 4 * WIDTH

    def qkv_tile(j):
        return jnp.where(j == 0, 0, jnp.where(j <= 5, 3, jnp.where(j == 6, 1, jnp.where(j == 7, 2, j - 4))))

    n_tab = rows_per_table // tm
    row = lambda i, j: (i, 0)
    const = lambda i, j: (0, 0)
    kv_spec = pl.BlockSpec((tm * N_HEADS, HEAD_DIM), row)
    kv_shape = jax.ShapeDtypeStruct((m * N_HEADS, HEAD_DIM), F32)
    kvb_spec = pl.BlockSpec((tm, WIDTH), row)
    kvb_shape = jax.ShapeDtypeStruct((m, WIDTH), BF16)
    out_shape = (
        jax.ShapeDtypeStruct((m, 2 * WIDTH), BF16),
        jax.ShapeDtypeStruct((m, 4 * WIDTH), BF16),
        kv_shape, kv_shape, kv_shape, kv_shape,
        kvb_shape, kvb_shape, kvb_shape, kvb_shape,
        jax.ShapeDtypeStruct((m, N_HEADS), F32),
    )
    return pl.pallas_call(
        _proj_kernel,
        grid=(m // tm, 10),
        in_specs=[
            pl.BlockSpec((tm, d), row),
            pl.BlockSpec((1, d), const),
            pl.BlockSpec((d, WIDTH), lambda i, j: (0, qkv_tile(j))),
            pl.BlockSpec((d, WIDTH), lambda i, j: (0, jnp.clip(j - 2, 0, 3))),
            pl.BlockSpec((d, LANES), const),
            pl.BlockSpec((1, LANES), const),
            pl.BlockSpec((1, LANES), const),
            pl.BlockSpec((1, LANES), const),
            pl.BlockSpec((1, LANES), const),
            pl.BlockSpec((1, LANES), const),
            pl.BlockSpec((tm, LANES), lambda i, j: (i % n_tab, 0)),
            pl.BlockSpec((tm, LANES), lambda i, j: (i % n_tab, 0)),
        ],
        out_specs=(
            pl.BlockSpec((tm, WIDTH), lambda i, j: (i, jnp.minimum(j, 1))),
            pl.BlockSpec((tm, WIDTH), lambda i, j: (i, jnp.clip(j - 2, 0, 3))),
            kv_spec, kv_spec, kv_spec, kv_spec,
            kvb_spec, kvb_spec, kvb_spec, kvb_spec,
            pl.BlockSpec((tm, N_HEADS), row),
        ),
        out_shape=out_shape,
        scratch_shapes=[pltpu.VMEM((tm, d), BF16)],
        compiler_params=_params(("arbitrary", "arbitrary"), 60),
        name="proj",
    )(x, gn, w_qkv, w_g, w_f, b_f, gqd, gkd, gqf, gkf, cos, sin_signed)


def _pad_lanes(chunk, lane):
    out = jnp.zeros(lane.shape, F32)
    for h in range(N_HEADS):
        out = jnp.where(lane == h, chunk[:, h:h + 1], out)
    return out


def _cumsum_kernel(lf_ref, cum_ref, cumt_ref):
    s = lf_ref.shape[0]
    r = lax.broadcasted_iota(jnp.int32, (LANES, LANES), 0)
    c = lax.broadcasted_iota(jnp.int32, (LANES, LANES), 1)
    tri = (c <= r).astype(F32)
    carry = jnp.zeros((1, LANES), F32)
    for ci in range(s // LANES):
        rows = slice(ci * LANES, (ci + 1) * LANES)
        pad = _pad_lanes(lf_ref[rows, :], c)
        res = jnp.dot(tri, pad, preferred_element_type=F32, precision=HIGHEST) + carry
        cum_ref[rows, :] = res[:, :N_HEADS]
        cumt_ref[0, :, rows] = res.T[:N_HEADS, :]
        carry = res[LANES - 1:LANES, :]


def _cumsum(logf, batch, seq):
    return pl.pallas_call(
        _cumsum_kernel,
        grid=(batch,),
        in_specs=[pl.BlockSpec((seq, N_HEADS), lambda b: (b, 0))],
        out_specs=(
            pl.BlockSpec((seq, N_HEADS), lambda b: (b, 0)),
            pl.BlockSpec((1, N_HEADS, seq), lambda b: (b, 0, 0)),
        ),
        out_shape=(
            jax.ShapeDtypeStruct((batch * seq, N_HEADS), F32),
            jax.ShapeDtypeStruct((batch, N_HEADS, seq), F32),
        ),
        compiler_params=_params(("arbitrary",), 32),
        name="cumsum",
    )(logf)


ROWS = 2 * N_HEADS
PAGE_COLS = PAGE_SIZE * N_HEADS
N_DEC_SMALL = 7


def _split3(a):
    hi = a.astype(BF16)
    r1 = a - hi.astype(F32)
    mid = r1.astype(BF16)
    lo = (r1 - mid.astype(F32)).astype(BF16)
    return hi, mid, lo


def _stack2(a):
    return jnp.concatenate([a, a], axis=0)


def _lambda_value(lq1_ref, lk1_ref, lq2_ref, lk2_ref, lam_init):
    a = jnp.sum(lq1_ref[...] * lk1_ref[...], axis=-1, keepdims=True)
    b = jnp.sum(lq2_ref[...] * lk2_ref[...], axis=-1, keepdims=True)
    return jnp.exp(a) - jnp.exp(b) + lam_init


def _sub_norm(o, g, lam_init):
    ms = jnp.mean(o * o, axis=-1, keepdims=True)
    return o * lax.rsqrt(ms + EPS) * g * (1.0 - lam_init)


def _decode_step(p, n_steps, very_first, lam_refs, gsub_ref, small_refs, page_refs, out_refs, scratch_refs,
                 *, lam_init, group):
    qd_ref, qf_ref, kdn_ref, vdn_ref, kfn_ref, vfn_ref, lfn_ref = small_refs
    kd_refs, vd_refs, kf_refs, vf_refs, lf_refs = (page_refs[i * group:(i + 1) * group] for i in range(5))
    od_ref, of_ref = out_refs
    qs_ref, m_ref, l_ref, acc_ref, carry_ref, later_ref, own_ref = scratch_refs
    cols = group * PAGE_COLS

    @pl.when(very_first)
    def _():
        ks = lax.broadcasted_iota(jnp.int32, (PAGE_SIZE, PAGE_COLS), 0)
        kc = lax.broadcasted_iota(jnp.int32, (PAGE_SIZE, PAGE_COLS), 1)
        later_ref[...] = (ks > (kc >> 3)).astype(BF16)
        r = lax.broadcasted_iota(jnp.int32, (2 * ROWS, cols), 0)
        c = lax.broadcasted_iota(jnp.int32, (2 * ROWS, cols), 1)
        own_ref[...] = jnp.where((r & (N_HEADS - 1)) == (c & (N_HEADS - 1)), 0.0, NEG)

    @pl.when(p == 0)
    def _():
        lane = _lane_iota((N_HEADS, HEAD_DIM))
        q = qd_ref[0] * (DIFF_HALF ** -0.5)
        zeros = jnp.zeros((N_HEADS, HEAD_DIM), F32)
        q_d = jnp.concatenate([jnp.where(lane < DIFF_HALF, q, 0.0), jnp.where(lane < DIFF_HALF, 0.0, q)], axis=0)
        q_f = jnp.concatenate([qf_ref[0], zeros], axis=0)
        qs_ref[:ROWS, :HEAD_DIM] = q_d.astype(BF16)
        qs_ref[:ROWS, HEAD_DIM:] = jnp.zeros((ROWS, HEAD_DIM), BF16)
        qs_ref[ROWS:, :HEAD_DIM] = jnp.zeros((ROWS, HEAD_DIM), BF16)
        qs_ref[ROWS:, HEAD_DIM:] = q_f.astype(BF16)
        m_ref[:ROWS] = jnp.sum(q_d * _stack2(kdn_ref[0]), axis=-1, keepdims=True)
        m_ref[ROWS:] = jnp.sum(q_f * _stack2(kfn_ref[0]), axis=-1, keepdims=True)
        l_ref[...] = jnp.ones(l_ref.shape, F32)
        acc_ref[...] = jnp.concatenate([_stack2(vdn_ref[0]), _stack2(vfn_ref[0])], axis=0)
        carry_ref[...] = _stack2(lfn_ref[0])

    def both(d_refs, f_refs):
        flat = lambda refs: jnp.concatenate(
            [ref[...].reshape(PAGE_COLS, HEAD_DIM).astype(BF16) for ref in refs], axis=0)
        return jnp.concatenate([flat(d_refs), flat(f_refs)], axis=1)

    pages_lf = [_stack2(lf[...]) for lf in lf_refs]
    within = _dot(jnp.concatenate([t for page_lf in pages_lf for t in _split3(page_lf)], axis=0), later_ref[...])
    carry = carry_ref[...]
    biases = []
    for g, page_lf in enumerate(pages_lf):
        w = within[3 * g * ROWS:3 * (g + 1) * ROWS]
        biases.append(carry + w[:ROWS] + w[ROWS:2 * ROWS] + w[2 * ROWS:])
        carry = carry + jnp.sum(page_lf, axis=-1, keepdims=True)
    carry_ref[...] = carry

    halves = 2 if group % 2 == 0 else 1
    per = group // halves
    part = lambda refs, i: refs[i * per:(i + 1) * per]
    logits = [_dot_nt(qs_ref[...], both(part(kd_refs, i), part(kf_refs, i))) for i in range(halves)]
    m, l, acc = m_ref[...], l_ref[...], acc_ref[...]
    for i in range(halves):
        bias = jnp.concatenate(biases[i * per:(i + 1) * per], axis=1)
        s = jnp.concatenate([logits[i][:ROWS], logits[i][ROWS:] + bias], axis=0) + own_ref[:, :per * PAGE_COLS]
        m_new = jnp.maximum(m, jnp.max(s, axis=-1, keepdims=True))
        alpha = jnp.exp(m - m_new)
        pr = jnp.exp(s - m_new)
        l = alpha * l + jnp.sum(pr, axis=-1, keepdims=True)
        pv = _dot(pr.astype(BF16), both(part(vd_refs, i), part(vf_refs, i)))
        acc = alpha * acc + jnp.concatenate([pv[:ROWS, :HEAD_DIM], pv[ROWS:, HEAD_DIM:]], axis=0)
        m = m_new
    m_ref[...], l_ref[...], acc_ref[...] = m, l, acc

    @pl.when(p == n_steps - 1)
    def _():
        lam = _lambda_value(*lam_refs, lam_init)
        o = acc_ref[...] / l_ref[...]
        od_ref[0] = _sub_norm(o[:N_HEADS] - lam * o[N_HEADS:ROWS], gsub_ref[...], lam_init)
        of_ref[0] = o[ROWS:ROWS + N_HEADS]


def _decode_scratch(group):
    return [
        pltpu.VMEM((2 * ROWS, 2 * HEAD_DIM), BF16),
        pltpu.VMEM((2 * ROWS, 1), F32),
        pltpu.VMEM((2 * ROWS, 1), F32),
        pltpu.VMEM((2 * ROWS, HEAD_DIM), F32),
        pltpu.VMEM((ROWS, 1), F32),
        pltpu.VMEM((PAGE_SIZE, PAGE_COLS), BF16),
        pltpu.VMEM((2 * ROWS, group * PAGE_COLS), F32),
    ]


class _DecodePlan:
    def __init__(self, page_table, qd, qf, kd_new, vd_new, kf_new, vf_new, lf_new,
                 cache_kd, cache_vd, cache_kf, cache_vf, cache_lf, *, group, step_of, first_step, n_steps):
        n_samples, n_pages = page_table.shape
        assert n_pages % group == 0
        spp = n_pages // group
        assert first_step % spp == 0 and n_steps % spp == 0
        self.group, self.spp, self.first_step, self.n_steps = group, spp, first_step, n_steps
        self.pt_flat = page_table.reshape(-1)
        tile3 = lambda a: a.reshape(n_samples, N_HEADS, HEAD_DIM)
        first_sample = first_step // spp
        sample = lambda *ids: step_of(*ids) // spp
        page = lambda g: (lambda *ids_pt: ids_pt[-1][
            sample(*ids_pt[:-1]) * n_pages + n_pages - 1 - ((step_of(*ids_pt[:-1]) % spp) * group + g)])
        tile_spec = pl.BlockSpec((1, N_HEADS, HEAD_DIM), lambda *a: (sample(*a[:-1]), 0, 0))
        lfn_spec = pl.BlockSpec((1, N_HEADS, 1), lambda *a: (sample(*a[:-1]), 0, 0))
        page_specs = lambda: [pl.BlockSpec((None, PAGE_SIZE, N_HEADS, HEAD_DIM),
                                           lambda *a, f=page(g): (f(*a), 0, 0, 0)) for g in range(group)]
        lf_specs = [pl.BlockSpec((None, N_HEADS, PAGE_SIZE), lambda *a, f=page(g): (f(*a), 0, 0))
                    for g in range(group)]
        cache_lf_t = jnp.swapaxes(cache_lf, 1, 2)
        self.inputs = [tile3(qd), tile3(qf), tile3(kd_new), tile3(vd_new), tile3(kf_new), tile3(vf_new),
                       lf_new.reshape(n_samples, N_HEADS, 1)] + [cache_kd] * group + [cache_vd] * group \
            + [cache_kf] * group + [cache_vf] * group + [cache_lf_t] * group
        self.in_specs = [tile_spec] * 6 + [lfn_spec] + page_specs() + page_specs() + page_specs() + page_specs() \
            + lf_specs
        out_spec = pl.BlockSpec((1, N_HEADS, HEAD_DIM), lambda *a: (sample(*a[:-1]) - first_sample, 0, 0))
        self.out_specs = [out_spec, out_spec]
        out_shape = jax.ShapeDtypeStruct((n_steps // spp, N_HEADS, HEAD_DIM), F32)
        self.out_shapes = [out_shape, out_shape]
        self.n_in = len(self.inputs)


def _decode_kernel(pt_ref, lq1_ref, lk1_ref, lq2_ref, lk2_ref, gsub_ref, *refs, lam_init, group, spp):
    del pt_ref
    n_in = N_DEC_SMALL + 5 * group
    t = pl.program_id(0)
    _decode_step(t % spp, spp, t == 0, (lq1_ref, lk1_ref, lq2_ref, lk2_ref), gsub_ref,
                 refs[:N_DEC_SMALL], refs[N_DEC_SMALL:n_in], refs[n_in:n_in + 2], refs[n_in + 2:],
                 lam_init=lam_init, group=group)


def _decode(plan, lams, g_sub, *, n_steps, lam_init):
    small = lambda n: pl.BlockSpec((1, n), lambda t, pt: (0, 0))
    grid_spec = pltpu.PrefetchScalarGridSpec(
        num_scalar_prefetch=1,
        grid=(n_steps,),
        in_specs=[small(DIFF_HALF)] * 4 + [small(LANES)] + plan.in_specs,
        out_specs=tuple(plan.out_specs),
        scratch_shapes=_decode_scratch(plan.group),
    )
    return pl.pallas_call(
        functools.partial(_decode_kernel, lam_init=lam_init, group=plan.group, spp=plan.spp),
        grid_spec=grid_spec,
        out_shape=tuple(plan.out_shapes),
        compiler_params=_params(("arbitrary",), 52),
        name="decode",
    )(plan.pt_flat, *lams, g_sub, *plan.inputs)


def _flash_update(s, cq, v_bf, m_ref, l_ref, acc_ref, h):
    reps = s.shape[1] // LANES
    m_prev = m_ref[h]
    m_curr = jnp.max(s, axis=-1, keepdims=True)
    if cq is not None:
        m_curr = m_curr + cq
    m_new = jnp.maximum(m_prev, m_curr)
    alpha = jnp.exp(m_prev - m_new)
    shift = m_new if cq is None else m_new - cq
    p = jnp.exp(s - jnp.tile(shift, (1, reps)))
    part = p[:, :LANES]
    for j in range(1, reps):
        part = part + p[:, j * LANES:(j + 1) * LANES]
    l_ref[h] = alpha * l_ref[h] + part
    acc_ref[h] = alpha * acc_ref[h] + _dot(p.astype(BF16), v_bf)
    m_ref[h] = m_new


def _flash_result(l_ref, acc_ref, h):
    return acc_ref[h] / jnp.sum(l_ref[h], axis=-1, keepdims=True)


def _causal_mask(s, qi, ki, tq, tk):
    rows = lax.broadcasted_iota(jnp.int32, s.shape, 0)
    qpos = qi * tq + jnp.where(rows >= tq, rows - tq, rows)
    kpos = ki * tk + lax.broadcasted_iota(jnp.int32, s.shape, 1)
    return jnp.where(kpos <= qpos, s, NEG)


def _attn_kernel(pt_ref, lq1_ref, lk1_ref, lq2_ref, lk2_ref, gsub_ref, q_ref, k_ref, v_ref, *refs,
                 diff, lam_init, dec):
    del pt_ref
    b, pair, j = pl.program_id(0), pl.program_id(1), pl.program_id(2)
    n_pairs, n_j = pl.num_programs(1), pl.num_programs(2)
    tq, tk = q_ref.shape[0], k_ref.shape[0]
    qi, ki = _folded(pair, j, n_j - 1)
    lam_refs = (lq1_ref, lk1_ref, lq2_ref, lk2_ref)
    n_extra = 0 if diff else 2
    n_dec_in = (N_DEC_SMALL + 5 * dec[0]) if dec else 0
    extra = refs[:n_extra]
    dec_in = refs[n_extra:n_extra + n_dec_in]
    outs = refs[n_extra + n_dec_in:n_extra + n_dec_in + (3 if dec else 1)]
    scratch = refs[n_extra + n_dec_in + len(outs):]
    o_ref = outs[0]
    if diff:
        qs_ref, m_ref, l_ref, acc_ref = scratch[:4]
        dec_scratch = scratch[4:]
    else:
        cq_ref, ckt_ref = extra
        qs_ref, cqr_ref, m_ref, l_ref, acc_ref = scratch[:5]
        dec_scratch = scratch[5:]

    def decode_step():
        if not dec:
            return
        group, spp, first_step, n_steps = dec
        local = (b * n_pairs + pair) * n_j + j

        @pl.when(local < n_steps)
        def _():
            _decode_step((first_step + local) % spp, spp, local == 0, lam_refs, gsub_ref, dec_in[:N_DEC_SMALL],
                         dec_in[N_DEC_SMALL:], outs[1:], dec_scratch, lam_init=lam_init, group=group)

    @pl.when(ki == 0)
    def _():
        if diff:
            lane = _lane_iota((tq, LANES))
            scale = DIFF_HALF ** -0.5
            for h in range(N_HEADS):
                qh = q_ref[:, h * HEAD_DIM:(h + 1) * HEAD_DIM].astype(F32) * scale
                qs_ref[h, :tq, :] = jnp.where(lane < DIFF_HALF, qh, 0.0).astype(BF16)
                qs_ref[h, tq:, :] = jnp.where(lane < DIFF_HALF, 0.0, qh).astype(BF16)
        else:
            qs_ref[...] = q_ref[...]
            for h in range(N_HEADS):
                cqr_ref[h] = jnp.broadcast_to(cq_ref[:, h:h + 1], (tq, LANES))
        m_ref[...] = jnp.full(m_ref.shape, NEG, F32)
        l_ref[...] = jnp.zeros(l_ref.shape, F32)
        acc_ref[...] = jnp.zeros(acc_ref.shape, F32)

    def step(masked):
        for h in range(N_HEADS):
            sl = slice(h * HEAD_DIM, (h + 1) * HEAD_DIM)
            kh = k_ref[:, sl]
            if diff:
                s = _dot_nt(qs_ref[h], kh)
                cq = None
            else:
                s = _dot_nt(qs_ref[:, sl], kh) - ckt_ref[0, h:h + 1, :]
                cq = cqr_ref[h]
            if masked:
                s = _causal_mask(s, qi, ki, tq, tk)
            _flash_update(s, cq, v_ref[:, sl], m_ref, l_ref, acc_ref, h)

    @pl.when(ki < qi)
    def _():
        step(False)
        decode_step()

    @pl.when(ki == qi)
    def _():
        step(True)
        if diff:
            lam = _lambda_value(*lam_refs, lam_init)
        for h in range(N_HEADS):
            o = _flash_result(l_ref, acc_ref, h)
            if diff:
                o = _sub_norm(o[:tq] - lam * o[tq:], gsub_ref[...], lam_init)
            o_ref[:, h * HEAD_DIM:(h + 1) * HEAD_DIM] = o.astype(o_ref.dtype)
        decode_step()


def _folded(pair, j, nq):
    first = j <= pair
    return jnp.where(first, pair, nq - 1 - pair), jnp.where(first, j, j - pair - 1)


def _attention(lams, g_sub, q_arr, q_col, k, v, fox_bias, plan, *, diff, batch, seq, tq, tk, lam_init, name):
    assert tq == tk and (seq // tq) % 2 == 0
    nq = seq // tq
    qrow = lambda b, r, j: b * nq + _folded(r, j, nq)[0]
    krow = lambda b, r, j: b * nq + _folded(r, j, nq)[1]
    small = lambda n: pl.BlockSpec((1, n), lambda b, r, j, pt: (0, 0))
    q_spec = pl.BlockSpec((tq, WIDTH), lambda b, r, j, pt: (qrow(b, r, j), q_col))
    kv_spec = pl.BlockSpec((tk, WIDTH), lambda b, r, j, pt: (krow(b, r, j), 0))
    o_spec = pl.BlockSpec((tq, WIDTH), lambda b, r, j, pt: (qrow(b, r, j), 0))
    in_specs = [small(DIFF_HALF)] * 4 + [small(LANES), q_spec, kv_spec, kv_spec]
    inputs = [*lams, g_sub, q_arr, k, v]
    rows = 2 * tq if diff else tq
    scratch = [pltpu.VMEM((N_HEADS, rows, HEAD_DIM), BF16) if diff else pltpu.VMEM((tq, WIDTH), BF16)]
    if not diff:
        in_specs += [pl.BlockSpec((tq, N_HEADS), lambda b, r, j, pt: (qrow(b, r, j), 0)),
                     pl.BlockSpec((1, N_HEADS, tk), lambda b, r, j, pt: (b, 0, _folded(r, j, nq)[1]))]
        inputs += list(fox_bias)
        scratch.append(pltpu.VMEM((N_HEADS, tq, LANES), F32))
    scratch += [pltpu.VMEM((N_HEADS, rows, LANES), F32), pltpu.VMEM((N_HEADS, rows, LANES), F32),
                pltpu.VMEM((N_HEADS, rows, HEAD_DIM), F32)]
    out_specs = [o_spec]
    out_shapes = [jax.ShapeDtypeStruct((batch * seq, WIDTH), BF16)]
    dec = None
    pt = jnp.zeros((1,), jnp.int32)
    if plan is not None:
        in_specs += plan.in_specs
        inputs += plan.inputs
        out_specs += plan.out_specs
        out_shapes += plan.out_shapes
        scratch += _decode_scratch(plan.group)
        dec = (plan.group, plan.spp, plan.first_step, plan.n_steps)
        pt = plan.pt_flat
    grid_spec = pltpu.PrefetchScalarGridSpec(
        num_scalar_prefetch=1, grid=(batch, nq // 2, nq + 1), in_specs=in_specs, out_specs=tuple(out_specs),
        scratch_shapes=scratch)
    return pl.pallas_call(
        functools.partial(_attn_kernel, diff=diff, lam_init=lam_init, dec=dec),
        grid_spec=grid_spec,
        out_shape=tuple(out_shapes),
        compiler_params=_params(("arbitrary", "arbitrary", "arbitrary"), 58),
        name=name,
    )(pt, *inputs)


def _merge_kernel(od_ref, of_ref, sgd_ref, sgf_ref, x_ref, wbd_ref, wbf_ref, wo_ref, h_ref, mg_ref, *, n_col):
    j = pl.program_id(1)

    @pl.when(j < n_col)
    def _():
        a = _dot(od_ref[...].astype(BF16), wbd_ref[...])
        b = _dot(of_ref[...].astype(BF16), wbf_ref[...])
        mg_ref[j] = (sgd_ref[...] * a + sgf_ref[...] * b).astype(BF16)

    @pl.when(j >= n_col)
    def _():
        merged = jnp.concatenate([mg_ref[c] for c in range(n_col)], axis=1)
        h_ref[...] = x_ref[...] + _dot(merged, wo_ref[...])


def _merge(od, of, gates, x, w_bd, w_bf, w_o, *, tm, tn):
    m, d = x.shape
    n_col = d // tn
    first = lambda i, j: jnp.minimum(j, n_col - 1)
    second = lambda i, j: (i, jnp.maximum(j - n_col, 0))
    return pl.pallas_call(
        functools.partial(_merge_kernel, n_col=n_col),
        grid=(m // tm, 2 * n_col),
        in_specs=[
            pl.BlockSpec((tm, WIDTH), lambda i, j: (i, 0)),
            pl.BlockSpec((tm, WIDTH), lambda i, j: (i, 0)),
            pl.BlockSpec((tm, tn), lambda i, j: (i, first(i, j))),
            pl.BlockSpec((tm, tn), lambda i, j: (i, n_col + first(i, j))),
            pl.BlockSpec((tm, tn), second),
            pl.BlockSpec((WIDTH, tn), lambda i, j: (0, first(i, j))),
            pl.BlockSpec((WIDTH, tn), lambda i, j: (0, first(i, j))),
            pl.BlockSpec((d, tn), lambda i, j: (0, jnp.maximum(j - n_col, 0))),
        ],
        out_specs=pl.BlockSpec((tm, tn), second),
        out_shape=jax.ShapeDtypeStruct((m, d), F32),
        scratch_shapes=[pltpu.VMEM((n_col, tm, tn), BF16)],
        compiler_params=_params(("arbitrary", "arbitrary"), 48),
        name="merge",
    )(od, of, gates, gates, x, w_bd, w_bf, w_o)


def _ffn_kernel(h_ref, g_ref, wg_ref, wu_ref, wd_ref, o_ref, hn_ref):
    @pl.when(pl.program_id(1) == 0)
    def _():
        h = h_ref[...]
        ms = jnp.mean(h * h, axis=-1, keepdims=True)
        hn_ref[...] = (h * lax.rsqrt(ms + EPS) * g_ref[...]).astype(BF16)
        o_ref[...] = h

    hn = hn_ref[...]
    a = _dot(hn, wg_ref[...])
    u = _dot(hn, wu_ref[...])
    ff = (a * _sigmoid(a) * u).astype(BF16)
    o_ref[...] += _dot(ff, wd_ref[...])


def _ffn(h, g, w_gate, w_up, w_down, *, tm, tf):
    m, d = h.shape
    f = w_gate.shape[1]
    return pl.pallas_call(
        _ffn_kernel,
        grid=(m // tm, f // tf),
        in_specs=[
            pl.BlockSpec((tm, d), lambda i, j: (i, 0)),
            pl.BlockSpec((1, d), lambda i, j: (0, 0)),
            pl.BlockSpec((d, tf), lambda i, j: (0, j)),
            pl.BlockSpec((d, tf), lambda i, j: (0, j)),
            pl.BlockSpec((tf, d), lambda i, j: (j, 0)),
        ],
        out_specs=pl.BlockSpec((tm, d), lambda i, j: (i, 0)),
        out_shape=jax.ShapeDtypeStruct((m, d), F32),
        scratch_shapes=[pltpu.VMEM((tm, d), BF16)],
        compiler_params=_params(("arbitrary", "arbitrary"), 58),
        name="ffn",
    )(h, g, w_gate, w_up, w_down)


def _rope_tables(pos):
    half = DIFF_HALF // 2
    inv = ROPE_THETA ** (-jnp.arange(half, dtype=F32) / half)
    ang = pos.astype(F32)[:, None] * inv[None, :]
    cos = jnp.concatenate([jnp.cos(ang)] * 4, axis=-1)
    sin = jnp.sin(ang)
    sin_signed = jnp.concatenate([-sin, sin, -sin, sin], axis=-1)
    return cos, sin_signed


def _tile(m, pref):
    return pref if m % pref == 0 else m


def _layer(l, xp, xs, caches, page_table, weights):
    (g_norm_attn, w_in, b_f, g_q_diff, g_k_diff, g_q_fox, g_k_fox, lambda_q1, lambda_k1, lambda_q2, lambda_k2,
     g_sub, w_branch_diff, w_branch_fox, w_o, g_norm_ffn, w_ffn_gate, w_ffn_up, w_ffn_down) = [w[l] for w in weights]
    batch, seq, d = xp.shape
    n_samples, dec_seq, _ = xs.shape
    past_len = page_table.shape[1] * PAGE_SIZE
    lam_init = 0.8 - 0.6 * math.exp(-0.3 * l)

    n_qkv = 6 * WIDTH
    w_all = w_in.astype(BF16)
    w_gates = w_all[:, n_qkv + N_HEADS:]
    w_f = jnp.pad(w_in[:, n_qkv:n_qkv + N_HEADS], ((0, 0), (0, LANES - N_HEADS))).astype(BF16)
    b_f_pad = jnp.pad(b_f, (0, LANES - N_HEADS)).reshape(1, LANES)
    two = lambda g: jnp.concatenate([g, g]).reshape(1, LANES)
    one = lambda g: g.reshape(1, -1)
    lams = tuple(one(v) for v in (lambda_q1, lambda_k1, lambda_q2, lambda_k2))
    w_bd, w_bf, w_out = w_branch_diff.astype(BF16), w_branch_fox.astype(BF16), w_o.astype(BF16)
    w_g, w_u, w_d = w_ffn_gate.astype(BF16), w_ffn_up.astype(BF16), w_ffn_down.astype(BF16)

    def project(x2d, pos, tm, rows_per_table):
        cos, sin_signed = _rope_tables(pos)
        return _proj(x2d, one(g_norm_attn), w_all, w_gates, w_f, b_f_pad, two(g_q_diff), two(g_k_diff), one(g_q_fox),
                     one(g_k_fox), cos, sin_signed, tm=tm, rows_per_table=rows_per_table)

    def finish(x2d, od, of, gates, tm_merge, tm_ffn):
        h = _merge(od, of, gates, x2d, w_bd, w_bf, w_out, tm=tm_merge, tn=512)
        return _ffn(h, one(g_norm_ffn), w_g, w_u, w_d, tm=tm_ffn, tf=512)

    assert dec_seq == 1
    xp2 = xp.reshape(batch * seq, d)
    xs2 = xs.reshape(n_samples, d)
    q_p, g_p, kd_p, vd_p, kf_p, vf_p, kdb_p, vdb_p, kfb_p, vfb_p, lf_p = project(xp2, jnp.arange(seq), _tile(seq, 512), seq)
    pos_s = jnp.full((n_samples,), past_len, jnp.int32)
    q_s, g_s, kd_s, vd_s, kf_s, vf_s, _, _, _, _, lf_s = project(xs2, pos_s, n_samples, n_samples)
    q_s = q_s.astype(F32)
    cum, cumt = _cumsum(lf_p, batch, seq)

    n_pages = page_table.shape[1]
    tq = tk = _tile(seq, 256)
    nq = seq // tq
    n_host = batch * (nq // 2) * (nq + 1)
    group = math.gcd(8, n_pages)
    spp = n_pages // group
    n_half = (n_samples // 2) * spp
    hosted = n_samples % 2 == 0 and n_half <= n_host
    dec_args = (page_table, q_s[:, :WIDTH], q_s[:, WIDTH:], kd_s, vd_s, kf_s, vf_s, lf_s,
                *[c[l] for c in caches])

    def host_step(first):
        return lambda b, r, j: first + jnp.minimum((b * (nq // 2) + r) * (nq + 1) + j, n_half - 1)

    plans = [_DecodePlan(*dec_args, group=group, step_of=host_step(f), first_step=f, n_steps=n_half)
             for f in (0, n_half)] if hosted else [None, None]
    attn = functools.partial(_attention, lams, one(g_sub), q_p, batch=batch, seq=seq, tq=tq, tk=tk,
                             lam_init=lam_init)
    out_d = attn(0, kdb_p, vdb_p, None, plans[0], diff=True, name="diff_attention")
    out_f = attn(1, kfb_p, vfb_p, (cum, cumt), plans[1], diff=False, name="fox_attention")
    if hosted:
        od_s = jnp.concatenate([out_d[1], out_f[1]], axis=0).reshape(n_samples, WIDTH)
        of_s = jnp.concatenate([out_d[2], out_f[2]], axis=0).reshape(n_samples, WIDTH)
    else:
        n_dec = n_samples * spp
        plan = _DecodePlan(*dec_args, group=group, step_of=lambda t: t, first_step=0, n_steps=n_dec)
        od_s, of_s = (o.reshape(n_samples, WIDTH) for o in _decode(plan, lams, one(g_sub), n_steps=n_dec,
                                                                  lam_init=lam_init))
    yp = finish(xp2, out_d[0], out_f[0], g_p, _tile(seq, 1024), _tile(seq, 1024))
    ys = finish(xs2, od_s, of_s, g_s, n_samples, n_samples)

    heads = lambda a, b, t: a.reshape(b, t, N_HEADS, HEAD_DIM)
    new_p = (heads(kd_p, batch, seq), heads(vd_p, batch, seq), heads(kf_p, batch, seq), heads(vf_p, batch, seq),
             lf_p.reshape(batch, seq, N_HEADS))
    new_s = (heads(kd_s, n_samples, 1), heads(vd_s, n_samples, 1), heads(kf_s, n_samples, 1),
             heads(vf_s, n_samples, 1), lf_s.reshape(n_samples, 1, N_HEADS))
    return yp.reshape(batch, seq, d), ys.reshape(n_samples, dec_seq, d), new_p, new_s


def kernel(x_prompt, x_sample, cache_k_diff, cache_v_diff, cache_k_fox, cache_v_fox, cache_logf_fox, page_table,
           g_norm_attn, w_in, b_f, g_q_diff, g_k_diff, g_q_fox, g_k_fox, lambda_q1, lambda_k1, lambda_q2, lambda_k2,
           g_sub, w_branch_diff, w_branch_fox, w_o, g_norm_ffn, w_ffn_gate, w_ffn_up, w_ffn_down):
    weights = (g_norm_attn, w_in, b_f, g_q_diff, g_k_diff, g_q_fox, g_k_fox, lambda_q1, lambda_k1, lambda_q2,
               lambda_k2, g_sub, w_branch_diff, w_branch_fox, w_o, g_norm_ffn, w_ffn_gate, w_ffn_up, w_ffn_down)
    caches = (cache_k_diff, cache_v_diff, cache_k_fox, cache_v_fox, cache_logf_fox)
    depth = w_in.shape[0]
    xp, xs = x_prompt, x_sample
    new_p, new_s = [], []
    for l in range(depth):
        xp, xs, np_l, ns_l = _layer(l, xp, xs, caches, page_table, weights)
        new_p.append(np_l)
        new_s.append(ns_l)
    stack = lambda lst, i: jnp.stack([t[i] for t in lst], axis=0)
    return (xp, xs) + tuple(stack(new_p, i) for i in range(5)) + tuple(stack(new_s, i) for i in range(5))
```

```python
import functools
import math

import jax
import jax.numpy as jnp
from jax import lax
from jax.experimental import pallas as pl
from jax.experimental.pallas import tpu as pltpu

N_HEADS = 8
HEAD_DIM = 128
DIFF_HALF = HEAD_DIM // 2
WIDTH = N_HEADS * HEAD_DIM
ROPE_THETA = 10000.0
EPS = 1e-6
PAGE_SIZE = 128
LANES = 128
NEG = -1e30
MIB = 1024 * 1024

F32 = jnp.float32
BF16 = jnp.bfloat16
HIGHEST = lax.Precision.HIGHEST


def _params(semantics, vmem_mib):
    return pltpu.CompilerParams(dimension_semantics=semantics, vmem_limit_bytes=vmem_mib * MIB)


def _dot(a, b):
    return jnp.dot(a, b, preferred_element_type=F32)


def _dot_nt(a, b, precision=None):
    return lax.dot_general(a, b, (((1,), (1,)), ((), ())), preferred_element_type=F32, precision=precision)


def _lane_iota(shape):
    return lax.broadcasted_iota(jnp.int32, shape, len(shape) - 1)


def _rms_rope_head(a, g, cos, sin_signed, lane):
    sq = a * a
    lo = lane < DIFF_HALF
    s_lo = jnp.sum(jnp.where(lo, sq, 0.0), axis=-1, keepdims=True)
    s_hi = jnp.sum(jnp.where(lo, 0.0, sq), axis=-1, keepdims=True)
    ms = jnp.where(lo, s_lo, s_hi) * (1.0 / DIFF_HALF)
    y = a * lax.rsqrt(ms + EPS) * g
    first = (lane & (DIFF_HALF - 1)) < (DIFF_HALF // 2)
    rot = jnp.where(first, pltpu.roll(y, LANES - DIFF_HALF // 2, 1), pltpu.roll(y, DIFF_HALF // 2, 1))
    return y * cos + rot * sin_signed


def _rms_head(a, g):
    ms = jnp.mean(a * a, axis=-1, keepdims=True)
    return a * lax.rsqrt(ms + EPS) * g


def _log_sigmoid(z):
    return -(jnp.maximum(-z, 0.0) + jnp.log1p(jnp.exp(-jnp.abs(z))))


def _sigmoid(z):
    return 1.0 / (1.0 + jnp.exp(-z))


def _proj_kernel(x_ref, gn_ref, wqkv_ref, wg_ref, wf_ref, bf_ref, gqd_ref, gkd_ref, gqf_ref, gkf_ref, cos_ref, sin_ref,
                 q_ref, g_ref, kd_ref, vd_ref, kf_ref, vf_ref, kdb_ref, vdb_ref, kfb_ref, vfb_ref, logf_ref, xn_ref):
    j = pl.program_id(1)
    tm = x_ref.shape[0]

    @pl.when(j == 0)
    def _():
        x = x_ref[...]
        ms = jnp.mean(x * x, axis=-1, keepdims=True)
        xn_ref[...] = (x * lax.rsqrt(ms + EPS) * gn_ref[...]).astype(BF16)
        z = _dot(xn_ref[...], wf_ref[...]) + bf_ref[...]
        logf_ref[...] = _log_sigmoid(z)[:, :N_HEADS]

    lane = _lane_iota((tm, LANES))
    pair = 2 * HEAD_DIM

    def head_pairs(w_ref):
        for c in range(WIDTH // pair):
            acc = _dot(xn_ref[...], w_ref[:, c * pair:(c + 1) * pair])
            for k in range(2):
                yield 2 * c + k, acc[:, k * HEAD_DIM:(k + 1) * HEAD_DIM]

    def heads(fn, out_ref, w_ref=wqkv_ref):
        for h, a in head_pairs(w_ref):
            out_ref[:, h * HEAD_DIM:(h + 1) * HEAD_DIM] = fn(a).astype(out_ref.dtype)

    def heads_kv(fn, out_ref, bf_out_ref):
        for h, a in head_pairs(wqkv_ref):
            val = fn(a)
            out_ref[pl.ds(h, tm, stride=N_HEADS), :] = val
            bf_out_ref[:, h * HEAD_DIM:(h + 1) * HEAD_DIM] = val.astype(BF16)

    @pl.when(j == 0)
    def _():
        heads(lambda a: _rms_rope_head(a, gqd_ref[...], cos_ref[...], sin_ref[...], lane), q_ref)

    @pl.when(j == 1)
    def _():
        heads(lambda a: _rms_head(a, gqf_ref[...]) * (HEAD_DIM ** -0.5), q_ref)

    @pl.when((j >= 2) & (j < 6))
    def _():
        heads(_sigmoid, g_ref, wg_ref)

    @pl.when(j == 6)
    def _():
        heads_kv(lambda a: _rms_rope_head(a, gkd_ref[...], cos_ref[...], sin_ref[...], lane), kd_ref, kdb_ref)

    @pl.when(j == 7)
    def _():
        heads_kv(lambda a: a, vd_ref, vdb_ref)

    @pl.when(j == 8)
    def _():
        heads_kv(lambda a: _rms_head(a, gkf_ref[...]), kf_ref, kfb_ref)

    @pl.when(j == 9)
    def _():
        heads_kv(lambda a: a, vf_ref, vfb_ref)


def _proj(x, gn, w_qkv, w_g, w_f, b_f, gqd, gkd, gqf, gkf, cos, sin_signed, *, tm, rows_per_table):
    m, d = x.shape
    assert w_g.shape[1] == 4 * WIDTH

    def qkv_tile(j):
        return jnp.where(j == 0, 0, jnp.where(j <= 5, 3, jnp.where(j == 6, 1, jnp.where(j == 7, 2, j - 4))))

    n_tab = rows_per_table // tm
    n_i = m // tm
    row = lambda i, j: (i, 0)
    const = lambda i, j: (0, 0)

    def done_after(last_j, col, n_cols):
        def index_map(i, j):
            moved = j > last_j
            more = i + 1 < n_i
            r = jnp.where(moved & more, i + 1, i)
            c = jnp.where(moved, jnp.where(more, 0, n_cols - 1), col(j))
            return r, c
        return index_map

    zero = lambda j: 0
    kv_spec = lambda last_j: pl.BlockSpec((tm * N_HEADS, HEAD_DIM), done_after(last_j, zero, 1))
    kv_shape = jax.ShapeDtypeStruct((m * N_HEADS, HEAD_DIM), F32)
    kvb_spec = lambda last_j: pl.BlockSpec((tm, WIDTH), done_after(last_j, zero, 1))
    kvb_shape = jax.ShapeDtypeStruct((m, WIDTH), BF16)
    out_shape = (
        jax.ShapeDtypeStruct((m, 2 * WIDTH), BF16),
        jax.ShapeDtypeStruct((m, 4 * WIDTH), BF16),
        kv_shape, kv_shape, kv_shape, kv_shape,
        kvb_shape, kvb_shape, kvb_shape, kvb_shape,
        jax.ShapeDtypeStruct((m, N_HEADS), F32),
    )
    return pl.pallas_call(
        _proj_kernel,
        grid=(m // tm, 10),
        in_specs=[
            pl.BlockSpec((tm, d), row),
            pl.BlockSpec((1, d), const),
            pl.BlockSpec((d, WIDTH), lambda i, j: (0, qkv_tile(j))),
            pl.BlockSpec((d, WIDTH), lambda i, j: (0, jnp.clip(j - 2, 0, 3))),
            pl.BlockSpec((d, LANES), const),
            pl.BlockSpec((1, LANES), const),
            pl.BlockSpec((1, LANES), const),
            pl.BlockSpec((1, LANES), const),
            pl.BlockSpec((1, LANES), const),
            pl.BlockSpec((1, LANES), const),
            pl.BlockSpec((tm, LANES), lambda i, j: (i % n_tab, 0)),
            pl.BlockSpec((tm, LANES), lambda i, j: (i % n_tab, 0)),
        ],
        out_specs=(
            pl.BlockSpec((tm, WIDTH), done_after(1, lambda j: jnp.minimum(j, 1), 2)),
            pl.BlockSpec((tm, WIDTH), done_after(5, lambda j: jnp.clip(j - 2, 0, 3), 4)),
            kv_spec(6), kv_spec(7), kv_spec(8), kv_spec(9),
            kvb_spec(6), kvb_spec(7), kvb_spec(8), kvb_spec(9),
            pl.BlockSpec((tm, N_HEADS), done_after(0, zero, 1)),
        ),
        out_shape=out_shape,
        scratch_shapes=[pltpu.VMEM((tm, d), BF16)],
        compiler_params=_params(("arbitrary", "arbitrary"), 60),
        name="proj",
    )(x, gn, w_qkv, w_g, w_f, b_f, gqd, gkd, gqf, gkf, cos, sin_signed)


def _pad_lanes(chunk, lane):
    out = jnp.zeros(lane.shape, F32)
    for h in range(N_HEADS):
        out = jnp.where(lane == h, chunk[:, h:h + 1], out)
    return out


def _cumsum_kernel(lf_ref, cum_ref, cumt_ref):
    s = lf_ref.shape[0]
    r = lax.broadcasted_iota(jnp.int32, (LANES, LANES), 0)
    c = lax.broadcasted_iota(jnp.int32, (LANES, LANES), 1)
    tri = (c <= r).astype(F32)
    carry = jnp.zeros((1, LANES), F32)
    for ci in range(s // LANES):
        rows = slice(ci * LANES, (ci + 1) * LANES)
        pad = _pad_lanes(lf_ref[rows, :], c)
        res = jnp.dot(tri, pad, preferred_element_type=F32, precision=HIGHEST) + carry
        cum_ref[rows, :] = res[:, :N_HEADS]
        cumt_ref[0, :, rows] = res.T[:N_HEADS, :]
        carry = res[LANES - 1:LANES, :]


def _cumsum(logf, batch, seq):
    return pl.pallas_call(
        _cumsum_kernel,
        grid=(batch,),
        in_specs=[pl.BlockSpec((seq, N_HEADS), lambda b: (b, 0))],
        out_specs=(
            pl.BlockSpec((seq, N_HEADS), lambda b: (b, 0)),
            pl.BlockSpec((1, N_HEADS, seq), lambda b: (b, 0, 0)),
        ),
        out_shape=(
            jax.ShapeDtypeStruct((batch * seq, N_HEADS), F32),
            jax.ShapeDtypeStruct((batch, N_HEADS, seq), F32),
        ),
        compiler_params=_params(("arbitrary",), 32),
        name="cumsum",
    )(logf)


ROWS = 2 * N_HEADS
PAGE_COLS = PAGE_SIZE * N_HEADS
N_DEC_SMALL = 7


def _split3(a):
    hi = a.astype(BF16)
    r1 = a - hi.astype(F32)
    mid = r1.astype(BF16)
    lo = (r1 - mid.astype(F32)).astype(BF16)
    return hi, mid, lo


def _stack2(a):
    return jnp.concatenate([a, a], axis=0)


def _lambda_value(lq1_ref, lk1_ref, lq2_ref, lk2_ref, lam_init):
    a = jnp.sum(lq1_ref[...] * lk1_ref[...], axis=-1, keepdims=True)
    b = jnp.sum(lq2_ref[...] * lk2_ref[...], axis=-1, keepdims=True)
    return jnp.exp(a) - jnp.exp(b) + lam_init


def _sub_norm(o, g, lam_init):
    ms = jnp.mean(o * o, axis=-1, keepdims=True)
    return o * lax.rsqrt(ms + EPS) * g * (1.0 - lam_init)


def _decode_step(p, n_steps, very_first, lam_refs, gsub_ref, small_refs, page_refs, out_refs, scratch_refs,
                 *, lam_init, group):
    qd_ref, qf_ref, kdn_ref, vdn_ref, kfn_ref, vfn_ref, lfn_ref = small_refs
    kd_refs, vd_refs, kf_refs, vf_refs, lf_refs = (page_refs[i * group:(i + 1) * group] for i in range(5))
    od_ref, of_ref = out_refs
    qs_ref, m_ref, l_ref, acc_ref, carry_ref, later_ref, own_ref = scratch_refs
    cols = group * PAGE_COLS

    @pl.when(very_first)
    def _():
        ks = lax.broadcasted_iota(jnp.int32, (PAGE_SIZE, PAGE_COLS), 0)
        kc = lax.broadcasted_iota(jnp.int32, (PAGE_SIZE, PAGE_COLS), 1)
        later_ref[...] = (ks > (kc >> 3)).astype(BF16)
        r = lax.broadcasted_iota(jnp.int32, (2 * ROWS, cols), 0)
        c = lax.broadcasted_iota(jnp.int32, (2 * ROWS, cols), 1)
        own_ref[...] = jnp.where((r & (N_HEADS - 1)) == (c & (N_HEADS - 1)), 0.0, NEG)

    @pl.when(p == 0)
    def _():
        lane = _lane_iota((N_HEADS, HEAD_DIM))
        q = qd_ref[0] * (DIFF_HALF ** -0.5)
        zeros = jnp.zeros((N_HEADS, HEAD_DIM), F32)
        q_d = jnp.concatenate([jnp.where(lane < DIFF_HALF, q, 0.0), jnp.where(lane < DIFF_HALF, 0.0, q)], axis=0)
        q_f = jnp.concatenate([qf_ref[0], zeros], axis=0)
        qs_ref[:ROWS, :HEAD_DIM] = q_d.astype(BF16)
        qs_ref[:ROWS, HEAD_DIM:] = jnp.zeros((ROWS, HEAD_DIM), BF16)
        qs_ref[ROWS:, :HEAD_DIM] = jnp.zeros((ROWS, HEAD_DIM), BF16)
        qs_ref[ROWS:, HEAD_DIM:] = q_f.astype(BF16)
        m_ref[:ROWS] = jnp.sum(q_d * _stack2(kdn_ref[0]), axis=-1, keepdims=True)
        m_ref[ROWS:] = jnp.sum(q_f * _stack2(kfn_ref[0]), axis=-1, keepdims=True)
        l_ref[...] = jnp.ones(l_ref.shape, F32)
        acc_ref[...] = jnp.concatenate([_stack2(vdn_ref[0]), _stack2(vfn_ref[0])], axis=0)
        carry_ref[...] = _stack2(lfn_ref[0])

    def both(d_refs, f_refs):
        flat = lambda refs: jnp.concatenate(
            [ref[...].reshape(PAGE_COLS, HEAD_DIM).astype(BF16) for ref in refs], axis=0)
        return jnp.concatenate([flat(d_refs), flat(f_refs)], axis=1)

    pages_lf = [_stack2(lf[...]) for lf in lf_refs]
    within = _dot(jnp.concatenate([t for page_lf in pages_lf for t in _split3(page_lf)], axis=0), later_ref[...])
    carry = carry_ref[...]
    biases = []
    for g, page_lf in enumerate(pages_lf):
        w = within[3 * g * ROWS:3 * (g + 1) * ROWS]
        biases.append(carry + w[:ROWS] + w[ROWS:2 * ROWS] + w[2 * ROWS:])
        carry = carry + jnp.sum(page_lf, axis=-1, keepdims=True)
    carry_ref[...] = carry

    halves = 2 if group % 2 == 0 else 1
    per = group // halves
    part = lambda refs, i: refs[i * per:(i + 1) * per]
    logits = [_dot_nt(qs_ref[...], both(part(kd_refs, i), part(kf_refs, i))) for i in range(halves)]
    m, l, acc = m_ref[...], l_ref[...], acc_ref[...]
    for i in range(halves):
        bias = jnp.concatenate(biases[i * per:(i + 1) * per], axis=1)
        s = jnp.concatenate([logits[i][:ROWS], logits[i][ROWS:] + bias], axis=0) + own_ref[:, :per * PAGE_COLS]
        m_new = jnp.maximum(m, jnp.max(s, axis=-1, keepdims=True))
        alpha = jnp.exp(m - m_new)
        pr = jnp.exp(s - m_new)
        l = alpha * l + jnp.sum(pr, axis=-1, keepdims=True)
        pv = _dot(pr.astype(BF16), both(part(vd_refs, i), part(vf_refs, i)))
        acc = alpha * acc + jnp.concatenate([pv[:ROWS, :HEAD_DIM], pv[ROWS:, HEAD_DIM:]], axis=0)
        m = m_new
    m_ref[...], l_ref[...], acc_ref[...] = m, l, acc

    @pl.when(p == n_steps - 1)
    def _():
        lam = _lambda_value(*lam_refs, lam_init)
        o = acc_ref[...] / l_ref[...]
        od_ref[0] = _sub_norm(o[:N_HEADS] - lam * o[N_HEADS:ROWS], gsub_ref[...], lam_init)
        of_ref[0] = o[ROWS:ROWS + N_HEADS]


def _decode_scratch(group):
    return [
        pltpu.VMEM((2 * ROWS, 2 * HEAD_DIM), BF16),
        pltpu.VMEM((2 * ROWS, 1), F32),
        pltpu.VMEM((2 * ROWS, 1), F32),
        pltpu.VMEM((2 * ROWS, HEAD_DIM), F32),
        pltpu.VMEM((ROWS, 1), F32),
        pltpu.VMEM((PAGE_SIZE, PAGE_COLS), BF16),
        pltpu.VMEM((2 * ROWS, group * PAGE_COLS), F32),
    ]


class _DecodePlan:
    def __init__(self, page_table, qd, qf, kd_new, vd_new, kf_new, vf_new, lf_new,
                 cache_kd, cache_vd, cache_kf, cache_vf, cache_lf, *, group, step_of, first_step, n_steps):
        n_samples, n_pages = page_table.shape
        assert n_pages % group == 0
        spp = n_pages // group
        assert first_step % spp == 0 and n_steps % spp == 0
        self.group, self.spp, self.first_step, self.n_steps = group, spp, first_step, n_steps
        self.pt_flat = page_table.reshape(-1)
        tile3 = lambda a: a.reshape(n_samples, N_HEADS, HEAD_DIM)
        first_sample = first_step // spp
        sample = lambda *ids: step_of(*ids) // spp
        page = lambda g: (lambda *ids_pt: ids_pt[-1][
            sample(*ids_pt[:-1]) * n_pages + n_pages - 1 - ((step_of(*ids_pt[:-1]) % spp) * group + g)])
        tile_spec = pl.BlockSpec((1, N_HEADS, HEAD_DIM), lambda *a: (sample(*a[:-1]), 0, 0))
        lfn_spec = pl.BlockSpec((1, N_HEADS, 1), lambda *a: (sample(*a[:-1]), 0, 0))
        page_specs = lambda: [pl.BlockSpec((None, PAGE_SIZE, N_HEADS, HEAD_DIM),
                                           lambda *a, f=page(g): (f(*a), 0, 0, 0)) for g in range(group)]
        lf_specs = [pl.BlockSpec((None, N_HEADS, PAGE_SIZE), lambda *a, f=page(g): (f(*a), 0, 0))
                    for g in range(group)]
        cache_lf_t = jnp.swapaxes(cache_lf, 1, 2)
        self.inputs = [tile3(qd), tile3(qf), tile3(kd_new), tile3(vd_new), tile3(kf_new), tile3(vf_new),
                       lf_new.reshape(n_samples, N_HEADS, 1)] + [cache_kd] * group + [cache_vd] * group \
            + [cache_kf] * group + [cache_vf] * group + [cache_lf_t] * group
        self.in_specs = [tile_spec] * 6 + [lfn_spec] + page_specs() + page_specs() + page_specs() + page_specs() \
            + lf_specs
        out_spec = pl.BlockSpec((1, N_HEADS, HEAD_DIM), lambda *a: (sample(*a[:-1]) - first_sample, 0, 0))
        self.out_specs = [out_spec, out_spec]
        out_shape = jax.ShapeDtypeStruct((n_steps // spp, N_HEADS, HEAD_DIM), F32)
        self.out_shapes = [out_shape, out_shape]
        self.n_in = len(self.inputs)


def _decode_kernel(pt_ref, lq1_ref, lk1_ref, lq2_ref, lk2_ref, gsub_ref, *refs, lam_init, group, spp):
    del pt_ref
    n_in = N_DEC_SMALL + 5 * group
    t = pl.program_id(0)
    _decode_step(t % spp, spp, t == 0, (lq1_ref, lk1_ref, lq2_ref, lk2_ref), gsub_ref,
                 refs[:N_DEC_SMALL], refs[N_DEC_SMALL:n_in], refs[n_in:n_in + 2], refs[n_in + 2:],
                 lam_init=lam_init, group=group)


def _decode(plan, lams, g_sub, *, n_steps, lam_init):
    small = lambda n: pl.BlockSpec((1, n), lambda t, pt: (0, 0))
    grid_spec = pltpu.PrefetchScalarGridSpec(
        num_scalar_prefetch=1,
        grid=(n_steps,),
        in_specs=[small(DIFF_HALF)] * 4 + [small(LANES)] + plan.in_specs,
        out_specs=tuple(plan.out_specs),
        scratch_shapes=_decode_scratch(plan.group),
    )
    return pl.pallas_call(
        functools.partial(_decode_kernel, lam_init=lam_init, group=plan.group, spp=plan.spp),
        grid_spec=grid_spec,
        out_shape=tuple(plan.out_shapes),
        compiler_params=_params(("arbitrary",), 52),
        name="decode",
    )(plan.pt_flat, *lams, g_sub, *plan.inputs)


def _flash_update(s, cq, v_bf, m_ref, l_ref, acc_ref, h):
    reps = s.shape[1] // LANES
    m_prev = m_ref[h]
    m_curr = jnp.max(s, axis=-1, keepdims=True)
    if cq is not None:
        m_curr = m_curr + cq
    m_new = jnp.maximum(m_prev, m_curr)
    alpha = jnp.exp(m_prev - m_new)
    shift = m_new if cq is None else m_new - cq
    p = jnp.exp(s - jnp.tile(shift, (1, reps)))
    part = p[:, :LANES]
    for j in range(1, reps):
        part = part + p[:, j * LANES:(j + 1) * LANES]
    l_ref[h] = alpha * l_ref[h] + part
    acc_ref[h] = alpha * acc_ref[h] + _dot(p.astype(BF16), v_bf)
    m_ref[h] = m_new


def _flash_result(l_ref, acc_ref, h):
    return acc_ref[h] / jnp.sum(l_ref[h], axis=-1, keepdims=True)


def _causal_mask(s, qi, ki, tq, tk):
    rows = lax.broadcasted_iota(jnp.int32, s.shape, 0)
    qpos = qi * tq + jnp.where(rows >= tq, rows - tq, rows)
    kpos = ki * tk + lax.broadcasted_iota(jnp.int32, s.shape, 1)
    return jnp.where(kpos <= qpos, s, NEG)


def _attn_kernel(pt_ref, lq1_ref, lk1_ref, lq2_ref, lk2_ref, gsub_ref, q_ref, k_ref, v_ref, *refs,
                 diff, lam_init, dec):
    del pt_ref
    b, pair, j = pl.program_id(0), pl.program_id(1), pl.program_id(2)
    n_pairs, n_j = pl.num_programs(1), pl.num_programs(2)
    tq, tk = q_ref.shape[0], k_ref.shape[0]
    qi, ki = _folded(pair, j, n_j - 1)
    lam_refs = (lq1_ref, lk1_ref, lq2_ref, lk2_ref)
    n_extra = 0 if diff else 2
    n_dec_in = (N_DEC_SMALL + 5 * dec[0]) if dec else 0
    extra = refs[:n_extra]
    dec_in = refs[n_extra:n_extra + n_dec_in]
    outs = refs[n_extra + n_dec_in:n_extra + n_dec_in + (3 if dec else 1)]
    scratch = refs[n_extra + n_dec_in + len(outs):]
    o_ref = outs[0]
    if diff:
        qs_ref, m_ref, l_ref, acc_ref = scratch[:4]
        dec_scratch = scratch[4:]
    else:
        cq_ref, ckt_ref = extra
        qs_ref, cqr_ref, m_ref, l_ref, acc_ref = scratch[:5]
        dec_scratch = scratch[5:]

    def decode_step():
        if not dec:
            return
        group, spp, first_step, n_steps = dec
        local = (b * n_pairs + pair) * n_j + j

        @pl.when(local < n_steps)
        def _():
            _decode_step((first_step + local) % spp, spp, local == 0, lam_refs, gsub_ref, dec_in[:N_DEC_SMALL],
                         dec_in[N_DEC_SMALL:], outs[1:], dec_scratch, lam_init=lam_init, group=group)

    @pl.when(ki == 0)
    def _():
        if diff:
            lane = _lane_iota((tq, LANES))
            scale = DIFF_HALF ** -0.5
            for h in range(N_HEADS):
                qh = q_ref[:, h * HEAD_DIM:(h + 1) * HEAD_DIM].astype(F32) * scale
                qs_ref[h, :tq, :] = jnp.where(lane < DIFF_HALF, qh, 0.0).astype(BF16)
                qs_ref[h, tq:, :] = jnp.where(lane < DIFF_HALF, 0.0, qh).astype(BF16)
        else:
            qs_ref[...] = q_ref[...]
            for h in range(N_HEADS):
                cqr_ref[h] = jnp.broadcast_to(cq_ref[:, h:h + 1], (tq, LANES))
        m_ref[...] = jnp.full(m_ref.shape, NEG, F32)
        l_ref[...] = jnp.zeros(l_ref.shape, F32)
        acc_ref[...] = jnp.zeros(acc_ref.shape, F32)

    def step(masked):
        for h in range(N_HEADS):
            sl = slice(h * HEAD_DIM, (h + 1) * HEAD_DIM)
            kh = k_ref[:, sl]
            if diff:
                s = _dot_nt(qs_ref[h], kh)
                cq = None
            else:
                s = _dot_nt(qs_ref[:, sl], kh) - ckt_ref[0, h:h + 1, :]
                cq = cqr_ref[h]
            if masked:
                s = _causal_mask(s, qi, ki, tq, tk)
            _flash_update(s, cq, v_ref[:, sl], m_ref, l_ref, acc_ref, h)

    @pl.when(ki < qi)
    def _():
        step(False)
        decode_step()

    @pl.when(ki == qi)
    def _():
        step(True)
        if diff:
            lam = _lambda_value(*lam_refs, lam_init)
        for h in range(N_HEADS):
            o = _flash_result(l_ref, acc_ref, h)
            if diff:
                o = _sub_norm(o[:tq] - lam * o[tq:], gsub_ref[...], lam_init)
            o_ref[:, h * HEAD_DIM:(h + 1) * HEAD_DIM] = o.astype(o_ref.dtype)
        decode_step()


def _folded(pair, j, nq):
    first = j <= pair
    return jnp.where(first, pair, nq - 1 - pair), jnp.where(first, j, j - pair - 1)


def _attention(lams, g_sub, q_arr, q_col, k, v, fox_bias, plan, *, diff, batch, seq, tq, tk, lam_init, name):
    assert tq == tk and (seq // tq) % 2 == 0
    nq = seq // tq
    qrow = lambda b, r, j: b * nq + _folded(r, j, nq)[0]
    krow = lambda b, r, j: b * nq + _folded(r, j, nq)[1]
    small = lambda n: pl.BlockSpec((1, n), lambda b, r, j, pt: (0, 0))
    q_spec = pl.BlockSpec((tq, WIDTH), lambda b, r, j, pt: (qrow(b, r, j), q_col))
    kv_spec = pl.BlockSpec((tk, WIDTH), lambda b, r, j, pt: (krow(b, r, j), 0))
    o_spec = pl.BlockSpec((tq, WIDTH), lambda b, r, j, pt: (qrow(b, r, j), 0))
    in_specs = [small(DIFF_HALF)] * 4 + [small(LANES), q_spec, kv_spec, kv_spec]
    inputs = [*lams, g_sub, q_arr, k, v]
    rows = 2 * tq if diff else tq
    scratch = [pltpu.VMEM((N_HEADS, rows, HEAD_DIM), BF16) if diff else pltpu.VMEM((tq, WIDTH), BF16)]
    if not diff:
        in_specs += [pl.BlockSpec((tq, N_HEADS), lambda b, r, j, pt: (qrow(b, r, j), 0)),
                     pl.BlockSpec((1, N_HEADS, tk), lambda b, r, j, pt: (b, 0, _folded(r, j, nq)[1]))]
        inputs += list(fox_bias)
        scratch.append(pltpu.VMEM((N_HEADS, tq, LANES), F32))
    scratch += [pltpu.VMEM((N_HEADS, rows, LANES), F32), pltpu.VMEM((N_HEADS, rows, LANES), F32),
                pltpu.VMEM((N_HEADS, rows, HEAD_DIM), F32)]
    out_specs = [o_spec]
    out_shapes = [jax.ShapeDtypeStruct((batch * seq, WIDTH), BF16)]
    dec = None
    pt = jnp.zeros((1,), jnp.int32)
    if plan is not None:
        in_specs += plan.in_specs
        inputs += plan.inputs
        out_specs += plan.out_specs
        out_shapes += plan.out_shapes
        scratch += _decode_scratch(plan.group)
        dec = (plan.group, plan.spp, plan.first_step, plan.n_steps)
        pt = plan.pt_flat
    grid_spec = pltpu.PrefetchScalarGridSpec(
        num_scalar_prefetch=1, grid=(batch, nq // 2, nq + 1), in_specs=in_specs, out_specs=tuple(out_specs),
        scratch_shapes=scratch)
    return pl.pallas_call(
        functools.partial(_attn_kernel, diff=diff, lam_init=lam_init, dec=dec),
        grid_spec=grid_spec,
        out_shape=tuple(out_shapes),
        compiler_params=_params(("arbitrary", "arbitrary", "arbitrary"), 58),
        name=name,
    )(pt, *inputs)


def _merge_kernel(od_ref, of_ref, sgd_ref, sgf_ref, x_ref, wbd_ref, wbf_ref, wo_ref, h_ref, mg_ref, *, n_col):
    j = pl.program_id(1)

    @pl.when(j < n_col)
    def _():
        a = _dot(od_ref[...].astype(BF16), wbd_ref[...])
        b = _dot(of_ref[...].astype(BF16), wbf_ref[...])
        mg_ref[j] = (sgd_ref[...] * a + sgf_ref[...] * b).astype(BF16)

    @pl.when(j >= n_col)
    def _():
        merged = jnp.concatenate([mg_ref[c] for c in range(n_col)], axis=1)
        h_ref[...] = x_ref[...] + _dot(merged, wo_ref[...])


def _merge(od, of, gates, x, w_bd, w_bf, w_o, *, tm, tn):
    m, d = x.shape
    n_col = d // tn
    first = lambda i, j: jnp.minimum(j, n_col - 1)
    second = lambda i, j: (i, jnp.maximum(j - n_col, 0))
    return pl.pallas_call(
        functools.partial(_merge_kernel, n_col=n_col),
        grid=(m // tm, 2 * n_col),
        in_specs=[
            pl.BlockSpec((tm, WIDTH), lambda i, j: (i, 0)),
            pl.BlockSpec((tm, WIDTH), lambda i, j: (i, 0)),
            pl.BlockSpec((tm, tn), lambda i, j: (i, first(i, j))),
            pl.BlockSpec((tm, tn), lambda i, j: (i, n_col + first(i, j))),
            pl.BlockSpec((tm, tn), second),
            pl.BlockSpec((WIDTH, tn), lambda i, j: (0, first(i, j))),
            pl.BlockSpec((WIDTH, tn), lambda i, j: (0, first(i, j))),
            pl.BlockSpec((d, tn), lambda i, j: (0, jnp.maximum(j - n_col, 0))),
        ],
        out_specs=pl.BlockSpec((tm, tn), second),
        out_shape=jax.ShapeDtypeStruct((m, d), F32),
        scratch_shapes=[pltpu.VMEM((n_col, tm, tn), BF16)],
        compiler_params=_params(("arbitrary", "arbitrary"), 48),
        name="merge",
    )(od, of, gates, gates, x, w_bd, w_bf, w_o)


def _ffn_kernel(h_ref, g_ref, wg_ref, wu_ref, wd_ref, o_ref, hn_ref):
    @pl.when(pl.program_id(1) == 0)
    def _():
        h = h_ref[...]
        ms = jnp.mean(h * h, axis=-1, keepdims=True)
        hn_ref[...] = (h * lax.rsqrt(ms + EPS) * g_ref[...]).astype(BF16)
        o_ref[...] = h

    hn = hn_ref[...]
    a = _dot(hn, wg_ref[...])
    u = _dot(hn, wu_ref[...])
    ff = (a * _sigmoid(a) * u).astype(BF16)
    o_ref[...] += _dot(ff, wd_ref[...])


def _ffn(h, g, w_gate, w_up, w_down, *, tm, tf):
    m, d = h.shape
    f = w_gate.shape[1]
    return pl.pallas_call(
        _ffn_kernel,
        grid=(m // tm, f // tf),
        in_specs=[
            pl.BlockSpec((tm, d), lambda i, j: (i, 0)),
            pl.BlockSpec((1, d), lambda i, j: (0, 0)),
            pl.BlockSpec((d, tf), lambda i, j: (0, j)),
            pl.BlockSpec((d, tf), lambda i, j: (0, j)),
            pl.BlockSpec((tf, d), lambda i, j: (j, 0)),
        ],
        out_specs=pl.BlockSpec((tm, d), lambda i, j: (i, 0)),
        out_shape=jax.ShapeDtypeStruct((m, d), F32),
        scratch_shapes=[pltpu.VMEM((tm, d), BF16)],
        compiler_params=_params(("arbitrary", "arbitrary"), 58),
        name="ffn",
    )(h, g, w_gate, w_up, w_down)


def _rope_tables(pos):
    half = DIFF_HALF // 2
    inv = ROPE_THETA ** (-jnp.arange(half, dtype=F32) / half)
    ang = pos.astype(F32)[:, None] * inv[None, :]
    cos = jnp.concatenate([jnp.cos(ang)] * 4, axis=-1)
    sin = jnp.sin(ang)
    sin_signed = jnp.concatenate([-sin, sin, -sin, sin], axis=-1)
    return cos, sin_signed


def _tile(m, pref):
    return pref if m % pref == 0 else m


def _layer(l, xp, xs, caches, page_table, weights):
    (g_norm_attn, w_in, b_f, g_q_diff, g_k_diff, g_q_fox, g_k_fox, lambda_q1, lambda_k1, lambda_q2, lambda_k2,
     g_sub, w_branch_diff, w_branch_fox, w_o, g_norm_ffn, w_ffn_gate, w_ffn_up, w_ffn_down) = [w[l] for w in weights]
    batch, seq, d = xp.shape
    n_samples, dec_seq, _ = xs.shape
    past_len = page_table.shape[1] * PAGE_SIZE
    lam_init = 0.8 - 0.6 * math.exp(-0.3 * l)

    n_qkv = 6 * WIDTH
    w_all = w_in.astype(BF16)
    w_gates = w_all[:, n_qkv + N_HEADS:]
    w_f = jnp.pad(w_in[:, n_qkv:n_qkv + N_HEADS], ((0, 0), (0, LANES - N_HEADS))).astype(BF16)
    b_f_pad = jnp.pad(b_f, (0, LANES - N_HEADS)).reshape(1, LANES)
    two = lambda g: jnp.concatenate([g, g]).reshape(1, LANES)
    one = lambda g: g.reshape(1, -1)
    lams = tuple(one(v) for v in (lambda_q1, lambda_k1, lambda_q2, lambda_k2))
    w_bd, w_bf, w_out = w_branch_diff.astype(BF16), w_branch_fox.astype(BF16), w_o.astype(BF16)
    w_g, w_u, w_d = w_ffn_gate.astype(BF16), w_ffn_up.astype(BF16), w_ffn_down.astype(BF16)

    def project(x2d, pos, tm, rows_per_table):
        cos, sin_signed = _rope_tables(pos)
        return _proj(x2d, one(g_norm_attn), w_all, w_gates, w_f, b_f_pad, two(g_q_diff), two(g_k_diff), one(g_q_fox),
                     one(g_k_fox), cos, sin_signed, tm=tm, rows_per_table=rows_per_table)

    def finish(x2d, od, of, gates, tm_merge, tm_ffn):
        h = _merge(od, of, gates, x2d, w_bd, w_bf, w_out, tm=tm_merge, tn=512)
        return _ffn(h, one(g_norm_ffn), w_g, w_u, w_d, tm=tm_ffn, tf=512)

    assert dec_seq == 1
    xp2 = xp.reshape(batch * seq, d)
    xs2 = xs.reshape(n_samples, d)
    q_p, g_p, kd_p, vd_p, kf_p, vf_p, kdb_p, vdb_p, kfb_p, vfb_p, lf_p = project(xp2, jnp.arange(seq), _tile(seq, 512), seq)
    pos_s = jnp.full((n_samples,), past_len, jnp.int32)
    q_s, g_s, kd_s, vd_s, kf_s, vf_s, _, _, _, _, lf_s = project(xs2, pos_s, n_samples, n_samples)
    q_s = q_s.astype(F32)
    cum, cumt = _cumsum(lf_p, batch, seq)

    n_pages = page_table.shape[1]
    tq = tk = _tile(seq, 256)
    nq = seq // tq
    n_host = batch * (nq // 2) * (nq + 1)
    group = math.gcd(8, n_pages)
    spp = n_pages // group
    n_half = (n_samples // 2) * spp
    hosted = n_samples % 2 == 0 and n_half <= n_host
    dec_args = (page_table, q_s[:, :WIDTH], q_s[:, WIDTH:], kd_s, vd_s, kf_s, vf_s, lf_s,
                *[c[l] for c in caches])

    def host_step(first):
        return lambda b, r, j: first + jnp.minimum((b * (nq // 2) + r) * (nq + 1) + j, n_half - 1)

    plans = [_DecodePlan(*dec_args, group=group, step_of=host_step(f), first_step=f, n_steps=n_half)
             for f in (0, n_half)] if hosted else [None, None]
    attn = functools.partial(_attention, lams, one(g_sub), q_p, batch=batch, seq=seq, tq=tq, tk=tk,
                             lam_init=lam_init)
    out_d = attn(0, kdb_p, vdb_p, None, plans[0], diff=True, name="diff_attention")
    out_f = attn(1, kfb_p, vfb_p, (cum, cumt), plans[1], diff=False, name="fox_attention")
    if hosted:
        od_s = jnp.concatenate([out_d[1], out_f[1]], axis=0).reshape(n_samples, WIDTH)
        of_s = jnp.concatenate([out_d[2], out_f[2]], axis=0).reshape(n_samples, WIDTH)
    else:
        n_dec = n_samples * spp
        plan = _DecodePlan(*dec_args, group=group, step_of=lambda t: t, first_step=0, n_steps=n_dec)
        od_s, of_s = (o.reshape(n_samples, WIDTH) for o in _decode(plan, lams, one(g_sub), n_steps=n_dec,
                                                                  lam_init=lam_init))
    yp = finish(xp2, out_d[0], out_f[0], g_p, _tile(seq, 1024), _tile(seq, 1024))
    ys = finish(xs2, od_s, of_s, g_s, n_samples, n_samples)

    heads = lambda a, b, t: a.reshape(b, t, N_HEADS, HEAD_DIM)
    new_p = (heads(kd_p, batch, seq), heads(vd_p, batch, seq), heads(kf_p, batch, seq), heads(vf_p, batch, seq),
             lf_p.reshape(batch, seq, N_HEADS))
    new_s = (heads(kd_s, n_samples, 1), heads(vd_s, n_samples, 1), heads(kf_s, n_samples, 1),
             heads(vf_s, n_samples, 1), lf_s.reshape(n_samples, 1, N_HEADS))
    return yp.reshape(batch, seq, d), ys.reshape(n_samples, dec_seq, d), new_p, new_s


def kernel(x_prompt, x_sample, cache_k_diff, cache_v_diff, cache_k_fox, cache_v_fox, cache_logf_fox, page_table,
           g_norm_attn, w_in, b_f, g_q_diff, g_k_diff, g_q_fox, g_k_fox, lambda_q1, lambda_k1, lambda_q2, lambda_k2,
           g_sub, w_branch_diff, w_branch_fox, w_o, g_norm_ffn, w_ffn_gate, w_ffn_up, w_ffn_down):
    weights = (g_norm_attn, w_in, b_f, g_q_diff, g_k_diff, g_q_fox, g_k_fox, lambda_q1, lambda_k1, lambda_q2,
               lambda_k2, g_sub, w_branch_diff, w_branch_fox, w_o, g_norm_ffn, w_ffn_gate, w_ffn_up, w_ffn_down)
    caches = (cache_k_diff, cache_v_diff, cache_k_fox, cache_v_fox, cache_logf_fox)
    depth = w_in.shape[0]
    xp, xs = x_prompt, x_sample
    new_p, new_s = [], []
    for l in range(depth):
        xp, xs, np_l, ns_l = _layer(l, xp, xs, caches, page_table, weights)
        new_p.append(np_l)
        new_s.append(ns_l)
    stack = lambda lst, i: jnp.stack([t[i] for t in lst], axis=0)
    return (xp, xs) + tuple(stack(new_p, i) for i in range(5)) + tuple(stack(new_s, i) for i in range(5))
```

```python
import functools
import math

import jax
import jax.numpy as jnp
from jax import lax
from jax.experimental import pallas as pl
from jax.experimental.pallas import tpu as pltpu

N_HEADS = 8
HEAD_DIM = 128
DIFF_HALF = HEAD_DIM // 2
WIDTH = N_HEADS * HEAD_DIM
ROPE_THETA = 10000.0
EPS = 1e-6
PAGE_SIZE = 128
LANES = 128
NEG = -1e30
MIB = 1024 * 1024

F32 = jnp.float32
BF16 = jnp.bfloat16
HIGHEST = lax.Precision.HIGHEST


def _params(semantics, vmem_mib):
    return pltpu.CompilerParams(dimension_semantics=semantics, vmem_limit_bytes=vmem_mib * MIB)


def _dot(a, b):
    return jnp.dot(a, b, preferred_element_type=F32)


def _dot_nt(a, b, precision=None):
    return lax.dot_general(a, b, (((1,), (1,)), ((), ())), preferred_element_type=F32, precision=precision)


def _lane_iota(shape):
    return lax.broadcasted_iota(jnp.int32, shape, len(shape) - 1)


def _rms_rope_head(a, g, cos, sin_signed, lane):
    sq = a * a
    lo = lane < DIFF_HALF
    s_lo = jnp.sum(jnp.where(lo, sq, 0.0), axis=-1, keepdims=True)
    s_hi = jnp.sum(jnp.where(lo, 0.0, sq), axis=-1, keepdims=True)
    ms = jnp.where(lo, s_lo, s_hi) * (1.0 / DIFF_HALF)
    y = a * lax.rsqrt(ms + EPS) * g
    first = (lane & (DIFF_HALF - 1)) < (DIFF_HALF // 2)
    rot = jnp.where(first, pltpu.roll(y, LANES - DIFF_HALF // 2, 1), pltpu.roll(y, DIFF_HALF // 2, 1))
    return y * cos + rot * sin_signed


def _rms_head(a, g):
    ms = jnp.mean(a * a, axis=-1, keepdims=True)
    return a * lax.rsqrt(ms + EPS) * g


def _log_sigmoid(z):
    return -(jnp.maximum(-z, 0.0) + jnp.log1p(jnp.exp(-jnp.abs(z))))


def _sigmoid(z):
    return 1.0 / (1.0 + jnp.exp(-z))


def _proj_kernel(x_ref, gn_ref, wqkv_ref, wg_ref, wf_ref, bf_ref, gqd_ref, gkd_ref, gqf_ref, gkf_ref, cos_ref, sin_ref,
                 q_ref, g_ref, kd_ref, vd_ref, kf_ref, vf_ref, kdb_ref, vdb_ref, kfb_ref, vfb_ref, logf_ref, xn_ref):
    j = pl.program_id(1)
    tm = x_ref.shape[0]

    @pl.when(j == 0)
    def _():
        x = x_ref[...]
        ms = jnp.mean(x * x, axis=-1, keepdims=True)
        xn_ref[...] = (x * lax.rsqrt(ms + EPS) * gn_ref[...]).astype(BF16)
        z = _dot(xn_ref[...], wf_ref[...]) + bf_ref[...]
        logf_ref[...] = _log_sigmoid(z)[:, :N_HEADS]

    lane = _lane_iota((tm, LANES))
    pair = 2 * HEAD_DIM

    def head_pairs(w_ref):
        for c in range(WIDTH // pair):
            acc = _dot(xn_ref[...], w_ref[:, c * pair:(c + 1) * pair])
            for k in range(2):
                yield 2 * c + k, acc[:, k * HEAD_DIM:(k + 1) * HEAD_DIM]

    def heads(fn, out_ref, w_ref=wqkv_ref):
        for h, a in head_pairs(w_ref):
            out_ref[:, h * HEAD_DIM:(h + 1) * HEAD_DIM] = fn(a).astype(out_ref.dtype)

    def heads_kv(fn, out_ref, bf_out_ref):
        for h, a in head_pairs(wqkv_ref):
            val = fn(a)
            out_ref[pl.ds(h, tm, stride=N_HEADS), :] = val
            bf_out_ref[:, h * HEAD_DIM:(h + 1) * HEAD_DIM] = val.astype(BF16)

    @pl.when(j == 0)
    def _():
        heads(lambda a: _rms_rope_head(a, gqd_ref[...], cos_ref[...], sin_ref[...], lane), q_ref)

    @pl.when(j == 1)
    def _():
        heads(lambda a: _rms_head(a, gqf_ref[...]) * (HEAD_DIM ** -0.5), q_ref)

    @pl.when((j >= 2) & (j < 6))
    def _():
        heads(_sigmoid, g_ref, wg_ref)

    @pl.when(j == 6)
    def _():
        heads_kv(lambda a: _rms_rope_head(a, gkd_ref[...], cos_ref[...], sin_ref[...], lane), kd_ref, kdb_ref)

    @pl.when(j == 7)
    def _():
        heads_kv(lambda a: a, vd_ref, vdb_ref)

    @pl.when(j == 8)
    def _():
        heads_kv(lambda a: _rms_head(a, gkf_ref[...]), kf_ref, kfb_ref)

    @pl.when(j == 9)
    def _():
        heads_kv(lambda a: a, vf_ref, vfb_ref)


def _proj(x, gn, w_qkv, w_g, w_f, b_f, gqd, gkd, gqf, gkf, cos, sin_signed, *, tm, rows_per_table):
    m, d = x.shape
    assert w_g.shape[1] == 4 * WIDTH

    def qkv_tile(j):
        return jnp.where(j == 0, 0, jnp.where(j <= 5, 3, jnp.where(j == 6, 1, jnp.where(j == 7, 2, j - 4))))

    n_tab = rows_per_table // tm
    row = lambda i, j: (i, 0)
    const = lambda i, j: (0, 0)
    kv_spec = pl.BlockSpec((tm * N_HEADS, HEAD_DIM), row)
    kv_shape = jax.ShapeDtypeStruct((m * N_HEADS, HEAD_DIM), F32)
    kvb_spec = pl.BlockSpec((tm, WIDTH), row)
    kvb_shape = jax.ShapeDtypeStruct((m, WIDTH), BF16)
    out_shape = (
        jax.ShapeDtypeStruct((m, 2 * WIDTH), BF16),
        jax.ShapeDtypeStruct((m, 4 * WIDTH), BF16),
        kv_shape, kv_shape, kv_shape, kv_shape,
        kvb_shape, kvb_shape, kvb_shape, kvb_shape,
        jax.ShapeDtypeStruct((m, N_HEADS), F32),
    )
    return pl.pallas_call(
        _proj_kernel,
        grid=(m // tm, 10),
        in_specs=[
            pl.BlockSpec((tm, d), row),
            pl.BlockSpec((1, d), const),
            pl.BlockSpec((d, WIDTH), lambda i, j: (0, qkv_tile(j))),
            pl.BlockSpec((d, WIDTH), lambda i, j: (0, jnp.clip(j - 2, 0, 3))),
            pl.BlockSpec((d, LANES), const),
            pl.BlockSpec((1, LANES), const),
            pl.BlockSpec((1, LANES), const),
            pl.BlockSpec((1, LANES), const),
            pl.BlockSpec((1, LANES), const),
            pl.BlockSpec((1, LANES), const),
            pl.BlockSpec((tm, LANES), lambda i, j: (i % n_tab, 0)),
            pl.BlockSpec((tm, LANES), lambda i, j: (i % n_tab, 0)),
        ],
        out_specs=(
            pl.BlockSpec((tm, WIDTH), lambda i, j: (i, jnp.minimum(j, 1))),
            pl.BlockSpec((tm, WIDTH), lambda i, j: (i, jnp.clip(j - 2, 0, 3))),
            kv_spec, kv_spec, kv_spec, kv_spec,
            kvb_spec, kvb_spec, kvb_spec, kvb_spec,
            pl.BlockSpec((tm, N_HEADS), row),
        ),
        out_shape=out_shape,
        scratch_shapes=[pltpu.VMEM((tm, d), BF16)],
        compiler_params=_params(("arbitrary", "arbitrary"), 60),
        name="proj",
    )(x, gn, w_qkv, w_g, w_f, b_f, gqd, gkd, gqf, gkf, cos, sin_signed)


def _pad_lanes(chunk, lane):
    out = jnp.zeros(lane.shape, F32)
    for h in range(N_HEADS):
        out = jnp.where(lane == h, chunk[:, h:h + 1], out)
    return out


def _cumsum_kernel(lf_ref, cum_ref, cumt_ref):
    s = lf_ref.shape[0]
    r = lax.broadcasted_iota(jnp.int32, (LANES, LANES), 0)
    c = lax.broadcasted_iota(jnp.int32, (LANES, LANES), 1)
    tri = (c <= r).astype(F32)
    carry = jnp.zeros((1, LANES), F32)
    for ci in range(s // LANES):
        rows = slice(ci * LANES, (ci + 1) * LANES)
        pad = _pad_lanes(lf_ref[rows, :], c)
        res = jnp.dot(tri, pad, preferred_element_type=F32, precision=HIGHEST) + carry
        cum_ref[rows, :] = res[:, :N_HEADS]
        cumt_ref[0, :, rows] = res.T[:N_HEADS, :]
        carry = res[LANES - 1:LANES, :]


def _cumsum(logf, batch, seq):
    return pl.pallas_call(
        _cumsum_kernel,
        grid=(batch,),
        in_specs=[pl.BlockSpec((seq, N_HEADS), lambda b: (b, 0))],
        out_specs=(
            pl.BlockSpec((seq, N_HEADS), lambda b: (b, 0)),
            pl.BlockSpec((1, N_HEADS, seq), lambda b: (b, 0, 0)),
        ),
        out_shape=(
            jax.ShapeDtypeStruct((batch * seq, N_HEADS), F32),
            jax.ShapeDtypeStruct((batch, N_HEADS, seq), F32),
        ),
        compiler_params=_params(("arbitrary",), 32),
        name="cumsum",
    )(logf)


ROWS = 2 * N_HEADS
PAGE_COLS = PAGE_SIZE * N_HEADS
N_DEC_SMALL = 7


def _split3(a):
    hi = a.astype(BF16)
    r1 = a - hi.astype(F32)
    mid = r1.astype(BF16)
    lo = (r1 - mid.astype(F32)).astype(BF16)
    return hi, mid, lo


def _stack2(a):
    return jnp.concatenate([a, a], axis=0)


def _lambda_value(lq1_ref, lk1_ref, lq2_ref, lk2_ref, lam_init):
    a = jnp.sum(lq1_ref[...] * lk1_ref[...], axis=-1, keepdims=True)
    b = jnp.sum(lq2_ref[...] * lk2_ref[...], axis=-1, keepdims=True)
    return jnp.exp(a) - jnp.exp(b) + lam_init


def _sub_norm(o, g, lam_init):
    ms = jnp.mean(o * o, axis=-1, keepdims=True)
    return o * lax.rsqrt(ms + EPS) * g * (1.0 - lam_init)


def _decode_step(p, n_steps, very_first, lam_refs, gsub_ref, small_refs, page_refs, out_refs, scratch_refs,
                 *, lam_init, group):
    qd_ref, qf_ref, kdn_ref, vdn_ref, kfn_ref, vfn_ref, lfn_ref = small_refs
    kd_refs, vd_refs, kf_refs, vf_refs, lf_refs = (page_refs[i * group:(i + 1) * group] for i in range(5))
    od_ref, of_ref = out_refs
    qs_ref, m_ref, l_ref, acc_ref, carry_ref, later_ref, own_ref = scratch_refs
    cols = group * PAGE_COLS

    @pl.when(very_first)
    def _():
        ks = lax.broadcasted_iota(jnp.int32, (PAGE_SIZE, PAGE_COLS), 0)
        kc = lax.broadcasted_iota(jnp.int32, (PAGE_SIZE, PAGE_COLS), 1)
        later_ref[...] = (ks > (kc >> 3)).astype(BF16)
        r = lax.broadcasted_iota(jnp.int32, (2 * ROWS, cols), 0)
        c = lax.broadcasted_iota(jnp.int32, (2 * ROWS, cols), 1)
        own_ref[...] = jnp.where((r & (N_HEADS - 1)) == (c & (N_HEADS - 1)), 0.0, NEG)

    @pl.when(p == 0)
    def _():
        lane = _lane_iota((N_HEADS, HEAD_DIM))
        q = qd_ref[0] * (DIFF_HALF ** -0.5)
        zeros = jnp.zeros((N_HEADS, HEAD_DIM), F32)
        q_d = jnp.concatenate([jnp.where(lane < DIFF_HALF, q, 0.0), jnp.where(lane < DIFF_HALF, 0.0, q)], axis=0)
        q_f = jnp.concatenate([qf_ref[0], zeros], axis=0)
        qs_ref[:ROWS, :HEAD_DIM] = q_d.astype(BF16)
        qs_ref[:ROWS, HEAD_DIM:] = jnp.zeros((ROWS, HEAD_DIM), BF16)
        qs_ref[ROWS:, :HEAD_DIM] = jnp.zeros((ROWS, HEAD_DIM), BF16)
        qs_ref[ROWS:, HEAD_DIM:] = q_f.astype(BF16)
        m_ref[:ROWS] = jnp.sum(q_d * _stack2(kdn_ref[0]), axis=-1, keepdims=True)
        m_ref[ROWS:] = jnp.sum(q_f * _stack2(kfn_ref[0]), axis=-1, keepdims=True)
        l_ref[...] = jnp.ones(l_ref.shape, F32)
        acc_ref[...] = jnp.concatenate([_stack2(vdn_ref[0]), _stack2(vfn_ref[0])], axis=0)
        carry_ref[...] = _stack2(lfn_ref[0])

    def both(d_refs, f_refs):
        flat = lambda refs: jnp.concatenate(
            [ref[...].reshape(PAGE_COLS, HEAD_DIM).astype(BF16) for ref in refs], axis=0)
        return jnp.concatenate([flat(d_refs), flat(f_refs)], axis=1)

    pages_lf = [_stack2(lf[...]) for lf in lf_refs]
    within = _dot(jnp.concatenate([t for page_lf in pages_lf for t in _split3(page_lf)], axis=0), later_ref[...])
    carry = carry_ref[...]
    biases = []
    for g, page_lf in enumerate(pages_lf):
        w = within[3 * g * ROWS:3 * (g + 1) * ROWS]
        biases.append(carry + w[:ROWS] + w[ROWS:2 * ROWS] + w[2 * ROWS:])
        carry = carry + jnp.sum(page_lf, axis=-1, keepdims=True)
    carry_ref[...] = carry

    halves = 2 if group % 2 == 0 else 1
    per = group // halves
    part = lambda refs, i: refs[i * per:(i + 1) * per]
    logits = [_dot_nt(qs_ref[...], both(part(kd_refs, i), part(kf_refs, i))) for i in range(halves)]
    m, l, acc = m_ref[...], l_ref[...], acc_ref[...]
    for i in range(halves):
        bias = jnp.concatenate(biases[i * per:(i + 1) * per], axis=1)
        s = jnp.concatenate([logits[i][:ROWS], logits[i][ROWS:] + bias], axis=0) + own_ref[:, :per * PAGE_COLS]
        m_new = jnp.maximum(m, jnp.max(s, axis=-1, keepdims=True))
        alpha = jnp.exp(m - m_new)
        pr = jnp.exp(s - m_new)
        l = alpha * l + jnp.sum(pr, axis=-1, keepdims=True)
        pv = _dot(pr.astype(BF16), both(part(vd_refs, i), part(vf_refs, i)))
        acc = alpha * acc + jnp.concatenate([pv[:ROWS, :HEAD_DIM], pv[ROWS:, HEAD_DIM:]], axis=0)
        m = m_new
    m_ref[...], l_ref[...], acc_ref[...] = m, l, acc

    @pl.when(p == n_steps - 1)
    def _():
        lam = _lambda_value(*lam_refs, lam_init)
        o = acc_ref[...] / l_ref[...]
        od_ref[0] = _sub_norm(o[:N_HEADS] - lam * o[N_HEADS:ROWS], gsub_ref[...], lam_init)
        of_ref[0] = o[ROWS:ROWS + N_HEADS]


def _decode_scratch(group):
    return [
        pltpu.VMEM((2 * ROWS, 2 * HEAD_DIM), BF16),
        pltpu.VMEM((2 * ROWS, 1), F32),
        pltpu.VMEM((2 * ROWS, 1), F32),
        pltpu.VMEM((2 * ROWS, HEAD_DIM), F32),
        pltpu.VMEM((ROWS, 1), F32),
        pltpu.VMEM((PAGE_SIZE, PAGE_COLS), BF16),
        pltpu.VMEM((2 * ROWS, group * PAGE_COLS), F32),
    ]


class _DecodePlan:
    def __init__(self, page_table, qd, qf, kd_new, vd_new, kf_new, vf_new, lf_new,
                 cache_kd, cache_vd, cache_kf, cache_vf, cache_lf, *, group, step_of, first_step, n_steps):
        n_samples, n_pages = page_table.shape
        assert n_pages % group == 0
        spp = n_pages // group
        assert first_step % spp == 0 and n_steps % spp == 0
        self.group, self.spp, self.first_step, self.n_steps = group, spp, first_step, n_steps
        self.pt_flat = page_table.reshape(-1)
        tile3 = lambda a: a.reshape(n_samples, N_HEADS, HEAD_DIM)
        first_sample = first_step // spp
        sample = lambda *ids: step_of(*ids) // spp
        page = lambda g: (lambda *ids_pt: ids_pt[-1][
            sample(*ids_pt[:-1]) * n_pages + n_pages - 1 - ((step_of(*ids_pt[:-1]) % spp) * group + g)])
        tile_spec = pl.BlockSpec((1, N_HEADS, HEAD_DIM), lambda *a: (sample(*a[:-1]), 0, 0))
        lfn_spec = pl.BlockSpec((1, N_HEADS, 1), lambda *a: (sample(*a[:-1]), 0, 0))
        page_specs = lambda: [pl.BlockSpec((None, PAGE_SIZE, N_HEADS, HEAD_DIM),
                                           lambda *a, f=page(g): (f(*a), 0, 0, 0)) for g in range(group)]
        lf_specs = [pl.BlockSpec((None, N_HEADS, PAGE_SIZE), lambda *a, f=page(g): (f(*a), 0, 0))
                    for g in range(group)]
        cache_lf_t = jnp.swapaxes(cache_lf, 1, 2)
        self.inputs = [tile3(qd), tile3(qf), tile3(kd_new), tile3(vd_new), tile3(kf_new), tile3(vf_new),
                       lf_new.reshape(n_samples, N_HEADS, 1)] + [cache_kd] * group + [cache_vd] * group \
            + [cache_kf] * group + [cache_vf] * group + [cache_lf_t] * group
        self.in_specs = [tile_spec] * 6 + [lfn_spec] + page_specs() + page_specs() + page_specs() + page_specs() \
            + lf_specs
        out_spec = pl.BlockSpec((1, N_HEADS, HEAD_DIM), lambda *a: (sample(*a[:-1]) - first_sample, 0, 0))
        self.out_specs = [out_spec, out_spec]
        out_shape = jax.ShapeDtypeStruct((n_steps // spp, N_HEADS, HEAD_DIM), F32)
        self.out_shapes = [out_shape, out_shape]
        self.n_in = len(self.inputs)


def _decode_kernel(pt_ref, lq1_ref, lk1_ref, lq2_ref, lk2_ref, gsub_ref, *refs, lam_init, group, spp):
    del pt_ref
    n_in = N_DEC_SMALL + 5 * group
    t = pl.program_id(0)
    _decode_step(t % spp, spp, t == 0, (lq1_ref, lk1_ref, lq2_ref, lk2_ref), gsub_ref,
                 refs[:N_DEC_SMALL], refs[N_DEC_SMALL:n_in], refs[n_in:n_in + 2], refs[n_in + 2:],
                 lam_init=lam_init, group=group)


def _decode(plan, lams, g_sub, *, n_steps, lam_init):
    small = lambda n: pl.BlockSpec((1, n), lambda t, pt: (0, 0))
    grid_spec = pltpu.PrefetchScalarGridSpec(
        num_scalar_prefetch=1,
        grid=(n_steps,),
        in_specs=[small(DIFF_HALF)] * 4 + [small(LANES)] + plan.in_specs,
        out_specs=tuple(plan.out_specs),
        scratch_shapes=_decode_scratch(plan.group),
    )
    return pl.pallas_call(
        functools.partial(_decode_kernel, lam_init=lam_init, group=plan.group, spp=plan.spp),
        grid_spec=grid_spec,
        out_shape=tuple(plan.out_shapes),
        compiler_params=_params(("arbitrary",), 52),
        name="decode",
    )(plan.pt_flat, *lams, g_sub, *plan.inputs)


def _flash_update(s, cq, v_bf, m_ref, l_ref, acc_ref, h):
    reps = s.shape[1] // LANES
    m_prev = m_ref[h]
    m_curr = jnp.max(s, axis=-1, keepdims=True)
    if cq is not None:
        m_curr = m_curr + cq
    m_new = jnp.maximum(m_prev, m_curr)
    alpha = jnp.exp(m_prev - m_new)
    shift = m_new if cq is None else m_new - cq
    p = jnp.exp(s - jnp.tile(shift, (1, reps)))
    part = p[:, :LANES]
    for j in range(1, reps):
        part = part + p[:, j * LANES:(j + 1) * LANES]
    l_ref[h] = alpha * l_ref[h] + part
    acc_ref[h] = alpha * acc_ref[h] + _dot(p.astype(BF16), v_bf)
    m_ref[h] = m_new


def _flash_result(l_ref, acc_ref, h):
    return acc_ref[h] / jnp.sum(l_ref[h], axis=-1, keepdims=True)


def _causal_mask(s, qi, ki, tq, tk):
    rows = lax.broadcasted_iota(jnp.int32, s.shape, 0)
    qpos = qi * tq + jnp.where(rows >= tq, rows - tq, rows)
    kpos = ki * tk + lax.broadcasted_iota(jnp.int32, s.shape, 1)
    return jnp.where(kpos <= qpos, s, NEG)


def _attn_kernel(pt_ref, lq1_ref, lk1_ref, lq2_ref, lk2_ref, gsub_ref, q_ref, k_ref, v_ref, *refs,
                 diff, lam_init, dec):
    del pt_ref
    b, pair, j = pl.program_id(0), pl.program_id(1), pl.program_id(2)
    n_pairs, n_j = pl.num_programs(1), pl.num_programs(2)
    tq, tk = q_ref.shape[0], k_ref.shape[0]
    qi, ki = _folded(pair, j, n_j - 1)
    lam_refs = (lq1_ref, lk1_ref, lq2_ref, lk2_ref)
    n_extra = 0 if diff else 2
    n_dec_in = (N_DEC_SMALL + 5 * dec[0]) if dec else 0
    extra = refs[:n_extra]
    dec_in = refs[n_extra:n_extra + n_dec_in]
    outs = refs[n_extra + n_dec_in:n_extra + n_dec_in + (3 if dec else 1)]
    scratch = refs[n_extra + n_dec_in + len(outs):]
    o_ref = outs[0]
    if diff:
        qs_ref, m_ref, l_ref, acc_ref = scratch[:4]
        dec_scratch = scratch[4:]
    else:
        cq_ref, ckt_ref = extra
        qs_ref, cqr_ref, m_ref, l_ref, acc_ref = scratch[:5]
        dec_scratch = scratch[5:]

    def decode_step():
        if not dec:
            return
        group, spp, first_step, n_steps = dec
        local = (b * n_pairs + pair) * n_j + j

        @pl.when(local < n_steps)
        def _():
            _decode_step((first_step + local) % spp, spp, local == 0, lam_refs, gsub_ref, dec_in[:N_DEC_SMALL],
                         dec_in[N_DEC_SMALL:], outs[1:], dec_scratch, lam_init=lam_init, group=group)

    @pl.when(ki == 0)
    def _():
        if diff:
            lane = _lane_iota((tq, LANES))
            scale = DIFF_HALF ** -0.5
            for h in range(N_HEADS):
                qh = q_ref[:, h * HEAD_DIM:(h + 1) * HEAD_DIM].astype(F32) * scale
                qs_ref[h, :tq, :] = jnp.where(lane < DIFF_HALF, qh, 0.0).astype(BF16)
                qs_ref[h, tq:, :] = jnp.where(lane < DIFF_HALF, 0.0, qh).astype(BF16)
        else:
            qs_ref[...] = q_ref[...]
            for h in range(N_HEADS):
                cqr_ref[h] = jnp.broadcast_to(cq_ref[:, h:h + 1], (tq, LANES))
        m_ref[...] = jnp.full(m_ref.shape, NEG, F32)
        l_ref[...] = jnp.zeros(l_ref.shape, F32)
        acc_ref[...] = jnp.zeros(acc_ref.shape, F32)

    def step(masked):
        for h in range(N_HEADS):
            sl = slice(h * HEAD_DIM, (h + 1) * HEAD_DIM)
            kh = k_ref[:, sl]
            if diff:
                s = _dot_nt(qs_ref[h], kh)
                cq = None
            else:
                s = _dot_nt(qs_ref[:, sl], kh) - ckt_ref[0, h:h + 1, :]
                cq = cqr_ref[h]
            if masked:
                s = _causal_mask(s, qi, ki, tq, tk)
            _flash_update(s, cq, v_ref[:, sl], m_ref, l_ref, acc_ref, h)

    @pl.when(ki < qi)
    def _():
        step(False)
        decode_step()

    @pl.when(ki == qi)
    def _():
        step(True)
        if diff:
            lam = _lambda_value(*lam_refs, lam_init)
        for h in range(N_HEADS):
            o = _flash_result(l_ref, acc_ref, h)
            if diff:
                o = _sub_norm(o[:tq] - lam * o[tq:], gsub_ref[...], lam_init)
            o_ref[:, h * HEAD_DIM:(h + 1) * HEAD_DIM] = o.astype(o_ref.dtype)
        decode_step()


def _folded(pair, j, nq):
    first = j <= pair
    return jnp.where(first, pair, nq - 1 - pair), jnp.where(first, j, j - pair - 1)


def _attention(lams, g_sub, q_arr, q_col, k, v, fox_bias, plan, *, diff, batch, seq, tq, tk, lam_init, name):
    assert tq == tk and (seq // tq) % 2 == 0
    nq = seq // tq
    qrow = lambda b, r, j: b * nq + _folded(r, j, nq)[0]
    krow = lambda b, r, j: b * nq + _folded(r, j, nq)[1]
    small = lambda n: pl.BlockSpec((1, n), lambda b, r, j, pt: (0, 0))
    q_spec = pl.BlockSpec((tq, WIDTH), lambda b, r, j, pt: (qrow(b, r, j), q_col))
    kv_spec = pl.BlockSpec((tk, WIDTH), lambda b, r, j, pt: (krow(b, r, j), 0))
    o_spec = pl.BlockSpec((tq, WIDTH), lambda b, r, j, pt: (qrow(b, r, j), 0))
    in_specs = [small(DIFF_HALF)] * 4 + [small(LANES), q_spec, kv_spec, kv_spec]
    inputs = [*lams, g_sub, q_arr, k, v]
    rows = 2 * tq if diff else tq
    scratch = [pltpu.VMEM((N_HEADS, rows, HEAD_DIM), BF16) if diff else pltpu.VMEM((tq, WIDTH), BF16)]
    if not diff:
        in_specs += [pl.BlockSpec((tq, N_HEADS), lambda b, r, j, pt: (qrow(b, r, j), 0)),
                     pl.BlockSpec((1, N_HEADS, tk), lambda b, r, j, pt: (b, 0, _folded(r, j, nq)[1]))]
        inputs += list(fox_bias)
        scratch.append(pltpu.VMEM((N_HEADS, tq, LANES), F32))
    scratch += [pltpu.VMEM((N_HEADS, rows, LANES), F32), pltpu.VMEM((N_HEADS, rows, LANES), F32),
                pltpu.VMEM((N_HEADS, rows, HEAD_DIM), F32)]
    out_specs = [o_spec]
    out_shapes = [jax.ShapeDtypeStruct((batch * seq, WIDTH), BF16)]
    dec = None
    pt = jnp.zeros((1,), jnp.int32)
    if plan is not None:
        in_specs += plan.in_specs
        inputs += plan.inputs
        out_specs += plan.out_specs
        out_shapes += plan.out_shapes
        scratch += _decode_scratch(plan.group)
        dec = (plan.group, plan.spp, plan.first_step, plan.n_steps)
        pt = plan.pt_flat
    grid_spec = pltpu.PrefetchScalarGridSpec(
        num_scalar_prefetch=1, grid=(batch, nq // 2, nq + 1), in_specs=in_specs, out_specs=tuple(out_specs),
        scratch_shapes=scratch)
    return pl.pallas_call(
        functools.partial(_attn_kernel, diff=diff, lam_init=lam_init, dec=dec),
        grid_spec=grid_spec,
        out_shape=tuple(out_shapes),
        compiler_params=_params(("arbitrary", "arbitrary", "arbitrary"), 58),
        name=name,
    )(pt, *inputs)


def _merge_kernel(od_ref, of_ref, sgd_ref, sgf_ref, x_ref, wbd_ref, wbf_ref, wo_ref, h_ref, mg_ref, *, n_col):
    j = pl.program_id(1)

    @pl.when(j < n_col)
    def _():
        a = _dot(od_ref[...].astype(BF16), wbd_ref[...])
        b = _dot(of_ref[...].astype(BF16), wbf_ref[...])
        mg_ref[j] = (sgd_ref[...] * a + sgf_ref[...] * b).astype(BF16)

    @pl.when(j >= n_col)
    def _():
        merged = jnp.concatenate([mg_ref[c] for c in range(n_col)], axis=1)
        h_ref[...] = x_ref[...] + _dot(merged, wo_ref[...])


def _merge(od, of, gates, x, w_bd, w_bf, w_o, *, tm, tn):
    m, d = x.shape
    n_col = d // tn
    first = lambda i, j: jnp.minimum(j, n_col - 1)
    second = lambda i, j: (i, jnp.maximum(j - n_col, 0))
    return pl.pallas_call(
        functools.partial(_merge_kernel, n_col=n_col),
        grid=(m // tm, 2 * n_col),
        in_specs=[
            pl.BlockSpec((tm, WIDTH), lambda i, j: (i, 0)),
            pl.BlockSpec((tm, WIDTH), lambda i, j: (i, 0)),
            pl.BlockSpec((tm, tn), lambda i, j: (i, first(i, j))),
            pl.BlockSpec((tm, tn), lambda i, j: (i, n_col + first(i, j))),
            pl.BlockSpec((tm, tn), second),
            pl.BlockSpec((WIDTH, tn), lambda i, j: (0, first(i, j))),
            pl.BlockSpec((WIDTH, tn), lambda i, j: (0, first(i, j))),
            pl.BlockSpec((d, tn), lambda i, j: (0, jnp.maximum(j - n_col, 0))),
        ],
        out_specs=pl.BlockSpec((tm, tn), second),
        out_shape=jax.ShapeDtypeStruct((m, d), F32),
        scratch_shapes=[pltpu.VMEM((n_col, tm, tn), BF16)],
        compiler_params=_params(("arbitrary", "arbitrary"), 58),
        name="merge",
    )(od, of, gates, gates, x, w_bd, w_bf, w_o)


def _ffn_kernel(h_ref, g_ref, wg_ref, wu_ref, wd_ref, o_ref, hn_ref):
    @pl.when(pl.program_id(1) == 0)
    def _():
        h = h_ref[...]
        ms = jnp.mean(h * h, axis=-1, keepdims=True)
        hn_ref[...] = (h * lax.rsqrt(ms + EPS) * g_ref[...]).astype(BF16)
        o_ref[...] = h

    hn = hn_ref[...]
    a = _dot(hn, wg_ref[...])
    u = _dot(hn, wu_ref[...])
    ff = (a * _sigmoid(a) * u).astype(BF16)
    o_ref[...] += _dot(ff, wd_ref[...])


def _ffn(h, g, w_gate, w_up, w_down, *, tm, tf):
    m, d = h.shape
    f = w_gate.shape[1]
    return pl.pallas_call(
        _ffn_kernel,
        grid=(m // tm, f // tf),
        in_specs=[
            pl.BlockSpec((tm, d), lambda i, j: (i, 0)),
            pl.BlockSpec((1, d), lambda i, j: (0, 0)),
            pl.BlockSpec((d, tf), lambda i, j: (0, j)),
            pl.BlockSpec((d, tf), lambda i, j: (0, j)),
            pl.BlockSpec((tf, d), lambda i, j: (j, 0)),
        ],
        out_specs=pl.BlockSpec((tm, d), lambda i, j: (i, 0)),
        out_shape=jax.ShapeDtypeStruct((m, d), F32),
        scratch_shapes=[pltpu.VMEM((tm, d), BF16)],
        compiler_params=_params(("arbitrary", "arbitrary"), 58),
        name="ffn",
    )(h, g, w_gate, w_up, w_down)


def _rope_tables(pos):
    half = DIFF_HALF // 2
    inv = ROPE_THETA ** (-jnp.arange(half, dtype=F32) / half)
    ang = pos.astype(F32)[:, None] * inv[None, :]
    cos = jnp.concatenate([jnp.cos(ang)] * 4, axis=-1)
    sin = jnp.sin(ang)
    sin_signed = jnp.concatenate([-sin, sin, -sin, sin], axis=-1)
    return cos, sin_signed


def _tile(m, pref):
    return pref if m % pref == 0 else m


def _layer(l, xp, xs, caches, page_table, weights):
    (g_norm_attn, w_in, b_f, g_q_diff, g_k_diff, g_q_fox, g_k_fox, lambda_q1, lambda_k1, lambda_q2, lambda_k2,
     g_sub, w_branch_diff, w_branch_fox, w_o, g_norm_ffn, w_ffn_gate, w_ffn_up, w_ffn_down) = [w[l] for w in weights]
    batch, seq, d = xp.shape
    n_samples, dec_seq, _ = xs.shape
    past_len = page_table.shape[1] * PAGE_SIZE
    lam_init = 0.8 - 0.6 * math.exp(-0.3 * l)

    n_qkv = 6 * WIDTH
    w_all = w_in.astype(BF16)
    w_gates = w_all[:, n_qkv + N_HEADS:]
    w_f = jnp.pad(w_in[:, n_qkv:n_qkv + N_HEADS], ((0, 0), (0, LANES - N_HEADS))).astype(BF16)
    b_f_pad = jnp.pad(b_f, (0, LANES - N_HEADS)).reshape(1, LANES)
    two = lambda g: jnp.concatenate([g, g]).reshape(1, LANES)
    one = lambda g: g.reshape(1, -1)
    lams = tuple(one(v) for v in (lambda_q1, lambda_k1, lambda_q2, lambda_k2))
    w_bd, w_bf, w_out = w_branch_diff.astype(BF16), w_branch_fox.astype(BF16), w_o.astype(BF16)
    w_g, w_u, w_d = w_ffn_gate.astype(BF16), w_ffn_up.astype(BF16), w_ffn_down.astype(BF16)

    def project(x2d, pos, tm, rows_per_table):
        cos, sin_signed = _rope_tables(pos)
        return _proj(x2d, one(g_norm_attn), w_all, w_gates, w_f, b_f_pad, two(g_q_diff), two(g_k_diff), one(g_q_fox),
                     one(g_k_fox), cos, sin_signed, tm=tm, rows_per_table=rows_per_table)

    def finish(x2d, od, of, gates, tm_merge, tm_ffn):
        h = _merge(od, of, gates, x2d, w_bd, w_bf, w_out, tm=tm_merge, tn=1024)
        return _ffn(h, one(g_norm_ffn), w_g, w_u, w_d, tm=tm_ffn, tf=512)

    assert dec_seq == 1
    xp2 = xp.reshape(batch * seq, d)
    xs2 = xs.reshape(n_samples, d)
    q_p, g_p, kd_p, vd_p, kf_p, vf_p, kdb_p, vdb_p, kfb_p, vfb_p, lf_p = project(xp2, jnp.arange(seq), _tile(seq, 512), seq)
    pos_s = jnp.full((n_samples,), past_len, jnp.int32)
    q_s, g_s, kd_s, vd_s, kf_s, vf_s, _, _, _, _, lf_s = project(xs2, pos_s, n_samples, n_samples)
    q_s = q_s.astype(F32)
    cum, cumt = _cumsum(lf_p, batch, seq)

    n_pages = page_table.shape[1]
    tq = tk = _tile(seq, 256)
    nq = seq // tq
    n_host = batch * (nq // 2) * (nq + 1)
    group = math.gcd(8, n_pages)
    spp = n_pages // group
    n_half = (n_samples // 2) * spp
    hosted = n_samples % 2 == 0 and n_half <= n_host
    dec_args = (page_table, q_s[:, :WIDTH], q_s[:, WIDTH:], kd_s, vd_s, kf_s, vf_s, lf_s,
                *[c[l] for c in caches])

    def host_step(first):
        return lambda b, r, j: first + jnp.minimum((b * (nq // 2) + r) * (nq + 1) + j, n_half - 1)

    plans = [_DecodePlan(*dec_args, group=group, step_of=host_step(f), first_step=f, n_steps=n_half)
             for f in (0, n_half)] if hosted else [None, None]
    attn = functools.partial(_attention, lams, one(g_sub), q_p, batch=batch, seq=seq, tq=tq, tk=tk,
                             lam_init=lam_init)
    out_d = attn(0, kdb_p, vdb_p, None, plans[0], diff=True, name="diff_attention")
    out_f = attn(1, kfb_p, vfb_p, (cum, cumt), plans[1], diff=False, name="fox_attention")
    if hosted:
        od_s = jnp.concatenate([out_d[1], out_f[1]], axis=0).reshape(n_samples, WIDTH)
        of_s = jnp.concatenate([out_d[2], out_f[2]], axis=0).reshape(n_samples, WIDTH)
    else:
        n_dec = n_samples * spp
        plan = _DecodePlan(*dec_args, group=group, step_of=lambda t: t, first_step=0, n_steps=n_dec)
        od_s, of_s = (o.reshape(n_samples, WIDTH) for o in _decode(plan, lams, one(g_sub), n_steps=n_dec,
                                                                  lam_init=lam_init))
    yp = finish(xp2, out_d[0], out_f[0], g_p, _tile(seq, 1024), _tile(seq, 1024))
    ys = finish(xs2, od_s, of_s, g_s, n_samples, n_samples)

    heads = lambda a, b, t: a.reshape(b, t, N_HEADS, HEAD_DIM)
    new_p = (heads(kd_p, batch, seq), heads(vd_p, batch, seq), heads(kf_p, batch, seq), heads(vf_p, batch, seq),
             lf_p.reshape(batch, seq, N_HEADS))
    new_s = (heads(kd_s, n_samples, 1), heads(vd_s, n_samples, 1), heads(kf_s, n_samples, 1),
             heads(vf_s, n_samples, 1), lf_s.reshape(n_samples, 1, N_HEADS))
    return yp.reshape(batch, seq, d), ys.reshape(n_samples, dec_seq, d), new_p, new_s


def kernel(x_prompt, x_sample, cache_k_diff, cache_v_diff, cache_k_fox, cache_v_fox, cache_logf_fox, page_table,
           g_norm_attn, w_in, b_f, g_q_diff, g_k_diff, g_q_fox, g_k_fox, lambda_q1, lambda_k1, lambda_q2, lambda_k2,
           g_sub, w_branch_diff, w_branch_fox, w_o, g_norm_ffn, w_ffn_gate, w_ffn_up, w_ffn_down):
    weights = (g_norm_attn, w_in, b_f, g_q_diff, g_k_diff, g_q_fox, g_k_fox, lambda_q1, lambda_k1, lambda_q2,
               lambda_k2, g_sub, w_branch_diff, w_branch_fox, w_o, g_norm_ffn, w_ffn_gate, w_ffn_up, w_ffn_down)
    caches = (cache_k_diff, cache_v_diff, cache_k_fox, cache_v_fox, cache_logf_fox)
    depth = w_in.shape[0]
    xp, xs = x_prompt, x_sample
    new_p, new_s = [], []
    for l in range(depth):
        xp, xs, np_l, ns_l = _layer(l, xp, xs, caches, page_table, weights)
        new_p.append(np_l)
        new_s.append(ns_l)
    stack = lambda lst, i: jnp.stack([t[i] for t in lst], axis=0)
    return (xp, xs) + tuple(stack(new_p, i) for i in range(5)) + tuple(stack(new_s, i) for i in range(5))
```

```python
import functools
import math

import jax
import jax.numpy as jnp
from jax import lax
from jax.experimental import pallas as pl
from jax.experimental.pallas import tpu as pltpu

N_HEADS = 8
HEAD_DIM = 128
DIFF_HALF = HEAD_DIM // 2
WIDTH = N_HEADS * HEAD_DIM
ROPE_THETA = 10000.0
EPS = 1e-6
PAGE_SIZE = 128
LANES = 128
NEG = -1e30
MIB = 1024 * 1024

PROJ_ROWS = 512
PROJ_VMEM_MIB = 60
ATTN_BLOCK = 256
DECODE_PAGES = 8
ATTN_VMEM_MIB = 58
DECODE_VMEM_MIB = 52
MERGE_ROWS, MERGE_COLS = 1024, 512
MERGE_VMEM_MIB = 48
FFN_ROWS, FFN_COLS = 1024, 512
FFN_VMEM_MIB = 58
CUMSUM_VMEM_MIB = 32

F32 = jnp.float32
BF16 = jnp.bfloat16
HIGHEST = lax.Precision.HIGHEST


def _params(semantics, vmem_mib):
    return pltpu.CompilerParams(dimension_semantics=semantics, vmem_limit_bytes=vmem_mib * MIB)


def _dot(a, b):
    return jnp.dot(a, b, preferred_element_type=F32)


def _dot_nt(a, b, precision=None):
    return lax.dot_general(a, b, (((1,), (1,)), ((), ())), preferred_element_type=F32, precision=precision)


def _lane_iota(shape):
    return lax.broadcasted_iota(jnp.int32, shape, len(shape) - 1)


def _rms_rope_head(a, g, cos, sin_signed, lane):
    sq = a * a
    lo = lane < DIFF_HALF
    s_lo = jnp.sum(jnp.where(lo, sq, 0.0), axis=-1, keepdims=True)
    s_hi = jnp.sum(jnp.where(lo, 0.0, sq), axis=-1, keepdims=True)
    ms = jnp.where(lo, s_lo, s_hi) * (1.0 / DIFF_HALF)
    y = a * lax.rsqrt(ms + EPS) * g
    first = (lane & (DIFF_HALF - 1)) < (DIFF_HALF // 2)
    rot = jnp.where(first, pltpu.roll(y, LANES - DIFF_HALF // 2, 1), pltpu.roll(y, DIFF_HALF // 2, 1))
    return y * cos + rot * sin_signed


def _rms_head(a, g):
    ms = jnp.mean(a * a, axis=-1, keepdims=True)
    return a * lax.rsqrt(ms + EPS) * g


def _log_sigmoid(z):
    return -(jnp.maximum(-z, 0.0) + jnp.log1p(jnp.exp(-jnp.abs(z))))


def _sigmoid(z):
    return 1.0 / (1.0 + jnp.exp(-z))


def _proj_kernel(x_ref, gn_ref, wqkv_ref, wg_ref, wf_ref, bf_ref, gqd_ref, gkd_ref, gqf_ref, gkf_ref, cos_ref, sin_ref,
                 q_ref, g_ref, kd_ref, vd_ref, kf_ref, vf_ref, kdb_ref, vdb_ref, kfb_ref, vfb_ref, logf_ref, xn_ref):
    j = pl.program_id(1)
    tm = x_ref.shape[0]

    @pl.when(j == 0)
    def _():
        x = x_ref[...]
        ms = jnp.mean(x * x, axis=-1, keepdims=True)
        xn_ref[...] = (x * lax.rsqrt(ms + EPS) * gn_ref[...]).astype(BF16)
        z = _dot(xn_ref[...], wf_ref[...]) + bf_ref[...]
        logf_ref[...] = _log_sigmoid(z)[:, :N_HEADS]

    lane = _lane_iota((tm, LANES))
    pair = 2 * HEAD_DIM

    def head_pairs(w_ref):
        for c in range(WIDTH // pair):
            acc = _dot(xn_ref[...], w_ref[:, c * pair:(c + 1) * pair])
            for k in range(2):
                yield 2 * c + k, acc[:, k * HEAD_DIM:(k + 1) * HEAD_DIM]

    def heads(fn, out_ref, w_ref=wqkv_ref):
        for h, a in head_pairs(w_ref):
            out_ref[:, h * HEAD_DIM:(h + 1) * HEAD_DIM] = fn(a).astype(out_ref.dtype)

    def heads_kv(fn, out_ref, bf_out_ref):
        for h, a in head_pairs(wqkv_ref):
            val = fn(a)
            out_ref[pl.ds(h, tm, stride=N_HEADS), :] = val
            bf_out_ref[:, h * HEAD_DIM:(h + 1) * HEAD_DIM] = val.astype(BF16)

    @pl.when(j == 0)
    def _():
        heads(lambda a: _rms_rope_head(a, gqd_ref[...], cos_ref[...], sin_ref[...], lane), q_ref)

    @pl.when(j == 1)
    def _():
        heads(lambda a: _rms_head(a, gqf_ref[...]) * (HEAD_DIM ** -0.5), q_ref)

    @pl.when((j >= 2) & (j < 6))
    def _():
        heads(_sigmoid, g_ref, wg_ref)

    @pl.when(j == 6)
    def _():
        heads_kv(lambda a: _rms_rope_head(a, gkd_ref[...], cos_ref[...], sin_ref[...], lane), kd_ref, kdb_ref)

    @pl.when(j == 7)
    def _():
        heads_kv(lambda a: a, vd_ref, vdb_ref)

    @pl.when(j == 8)
    def _():
        heads_kv(lambda a: _rms_head(a, gkf_ref[...]), kf_ref, kfb_ref)

    @pl.when(j == 9)
    def _():
        heads_kv(lambda a: a, vf_ref, vfb_ref)


def _proj(x, gn, w_qkv, w_g, w_f, b_f, gqd, gkd, gqf, gkf, cos, sin_signed, *, tm, rows_per_table):
    m, d = x.shape
    assert w_g.shape[1] == 4 * WIDTH

    def qkv_tile(j):
        return jnp.where(j == 0, 0, jnp.where(j <= 5, 3, jnp.where(j == 6, 1, jnp.where(j == 7, 2, j - 4))))

    n_tab = rows_per_table // tm
    row = lambda i, j: (i, 0)
    const = lambda i, j: (0, 0)
    kv_spec = pl.BlockSpec((tm * N_HEADS, HEAD_DIM), row)
    kv_shape = jax.ShapeDtypeStruct((m * N_HEADS, HEAD_DIM), F32)
    kvb_spec = pl.BlockSpec((tm, WIDTH), row)
    kvb_shape = jax.ShapeDtypeStruct((m, WIDTH), BF16)
    out_shape = (
        jax.ShapeDtypeStruct((m, 2 * WIDTH), BF16),
        jax.ShapeDtypeStruct((m, 4 * WIDTH), BF16),
        kv_shape, kv_shape, kv_shape, kv_shape,
        kvb_shape, kvb_shape, kvb_shape, kvb_shape,
        jax.ShapeDtypeStruct((m, N_HEADS), F32),
    )
    return pl.pallas_call(
        _proj_kernel,
        grid=(m // tm, 10),
        in_specs=[
            pl.BlockSpec((tm, d), row),
            pl.BlockSpec((1, d), const),
            pl.BlockSpec((d, WIDTH), lambda i, j: (0, qkv_tile(j))),
            pl.BlockSpec((d, WIDTH), lambda i, j: (0, jnp.clip(j - 2, 0, 3))),
            pl.BlockSpec((d, LANES), const),
            pl.BlockSpec((1, LANES), const),
            pl.BlockSpec((1, LANES), const),
            pl.BlockSpec((1, LANES), const),
            pl.BlockSpec((1, LANES), const),
            pl.BlockSpec((1, LANES), const),
            pl.BlockSpec((tm, LANES), lambda i, j: (i % n_tab, 0)),
            pl.BlockSpec((tm, LANES), lambda i, j: (i % n_tab, 0)),
        ],
        out_specs=(
            pl.BlockSpec((tm, WIDTH), lambda i, j: (i, jnp.minimum(j, 1))),
            pl.BlockSpec((tm, WIDTH), lambda i, j: (i, jnp.clip(j - 2, 0, 3))),
            kv_spec, kv_spec, kv_spec, kv_spec,
            kvb_spec, kvb_spec, kvb_spec, kvb_spec,
            pl.BlockSpec((tm, N_HEADS), row),
        ),
        out_shape=out_shape,
        scratch_shapes=[pltpu.VMEM((tm, d), BF16)],
        compiler_params=_params(("arbitrary", "arbitrary"), PROJ_VMEM_MIB),
        name="proj",
    )(x, gn, w_qkv, w_g, w_f, b_f, gqd, gkd, gqf, gkf, cos, sin_signed)


def _pad_lanes(chunk, lane):
    out = jnp.zeros(lane.shape, F32)
    for h in range(N_HEADS):
        out = jnp.where(lane == h, chunk[:, h:h + 1], out)
    return out


def _cumsum_kernel(lf_ref, cum_ref, cumt_ref):
    s = lf_ref.shape[0]
    r = lax.broadcasted_iota(jnp.int32, (LANES, LANES), 0)
    c = lax.broadcasted_iota(jnp.int32, (LANES, LANES), 1)
    tri = (c <= r).astype(F32)
    carry = jnp.zeros((1, LANES), F32)
    for ci in range(s // LANES):
        rows = slice(ci * LANES, (ci + 1) * LANES)
        pad = _pad_lanes(lf_ref[rows, :], c)
        res = jnp.dot(tri, pad, preferred_element_type=F32, precision=HIGHEST) + carry
        cum_ref[rows, :] = res[:, :N_HEADS]
        cumt_ref[0, :, rows] = res.T[:N_HEADS, :]
        carry = res[LANES - 1:LANES, :]


def _cumsum(logf, batch, seq):
    return pl.pallas_call(
        _cumsum_kernel,
        grid=(batch,),
        in_specs=[pl.BlockSpec((seq, N_HEADS), lambda b: (b, 0))],
        out_specs=(
            pl.BlockSpec((seq, N_HEADS), lambda b: (b, 0)),
            pl.BlockSpec((1, N_HEADS, seq), lambda b: (b, 0, 0)),
        ),
        out_shape=(
            jax.ShapeDtypeStruct((batch * seq, N_HEADS), F32),
            jax.ShapeDtypeStruct((batch, N_HEADS, seq), F32),
        ),
        compiler_params=_params(("arbitrary",), CUMSUM_VMEM_MIB),
        name="cumsum",
    )(logf)


ROWS = 2 * N_HEADS
PAGE_COLS = PAGE_SIZE * N_HEADS
N_DEC_SMALL = 7


def _split3(a):
    hi = a.astype(BF16)
    r1 = a - hi.astype(F32)
    mid = r1.astype(BF16)
    lo = (r1 - mid.astype(F32)).astype(BF16)
    return hi, mid, lo


def _stack2(a):
    return jnp.concatenate([a, a], axis=0)


def _lambda_value(lq1_ref, lk1_ref, lq2_ref, lk2_ref, lam_init):
    a = jnp.sum(lq1_ref[...] * lk1_ref[...], axis=-1, keepdims=True)
    b = jnp.sum(lq2_ref[...] * lk2_ref[...], axis=-1, keepdims=True)
    return jnp.exp(a) - jnp.exp(b) + lam_init


def _sub_norm(o, g, lam_init):
    ms = jnp.mean(o * o, axis=-1, keepdims=True)
    return o * lax.rsqrt(ms + EPS) * g * (1.0 - lam_init)


def _decode_step(p, n_steps, very_first, lam_refs, gsub_ref, small_refs, page_refs, out_refs, scratch_refs,
                 *, lam_init, group):
    qd_ref, qf_ref, kdn_ref, vdn_ref, kfn_ref, vfn_ref, lfn_ref = small_refs
    kd_refs, vd_refs, kf_refs, vf_refs, lf_refs = (page_refs[i * group:(i + 1) * group] for i in range(5))
    od_ref, of_ref = out_refs
    qs_ref, m_ref, l_ref, acc_ref, carry_ref, later_ref, own_ref = scratch_refs

    @pl.when(very_first)
    def _():
        ks = lax.broadcasted_iota(jnp.int32, (PAGE_SIZE, PAGE_COLS), 0)
        kc = lax.broadcasted_iota(jnp.int32, (PAGE_SIZE, PAGE_COLS), 1)
        later_ref[...] = (ks > (kc >> 3)).astype(BF16)
        r = lax.broadcasted_iota(jnp.int32, own_ref.shape, 0)
        c = lax.broadcasted_iota(jnp.int32, own_ref.shape, 1)
        own_ref[...] = jnp.where((r & (N_HEADS - 1)) == (c & (N_HEADS - 1)), 0.0, NEG)

    @pl.when(p == 0)
    def _():
        lane = _lane_iota((N_HEADS, HEAD_DIM))
        q = qd_ref[0] * (DIFF_HALF ** -0.5)
        zeros = jnp.zeros((N_HEADS, HEAD_DIM), F32)
        q_d = jnp.concatenate([jnp.where(lane < DIFF_HALF, q, 0.0), jnp.where(lane < DIFF_HALF, 0.0, q)], axis=0)
        q_f = jnp.concatenate([qf_ref[0], zeros], axis=0)
        qs_ref[:ROWS, :HEAD_DIM] = q_d.astype(BF16)
        qs_ref[:ROWS, HEAD_DIM:] = jnp.zeros((ROWS, HEAD_DIM), BF16)
        qs_ref[ROWS:, :HEAD_DIM] = jnp.zeros((ROWS, HEAD_DIM), BF16)
        qs_ref[ROWS:, HEAD_DIM:] = q_f.astype(BF16)
        m_ref[:ROWS] = jnp.sum(q_d * _stack2(kdn_ref[0]), axis=-1, keepdims=True)
        m_ref[ROWS:] = jnp.sum(q_f * _stack2(kfn_ref[0]), axis=-1, keepdims=True)
        l_ref[...] = jnp.ones(l_ref.shape, F32)
        acc_ref[...] = jnp.concatenate([_stack2(vdn_ref[0]), _stack2(vfn_ref[0])], axis=0)
        carry_ref[...] = _stack2(lfn_ref[0])

    def both(d_refs, f_refs):
        flat = lambda refs: jnp.concatenate(
            [ref[...].reshape(PAGE_COLS, HEAD_DIM).astype(BF16) for ref in refs], axis=0)
        return jnp.concatenate([flat(d_refs), flat(f_refs)], axis=1)

    pages_lf = [_stack2(lf[...]) for lf in lf_refs]
    within = _dot(jnp.concatenate([t for page_lf in pages_lf for t in _split3(page_lf)], axis=0), later_ref[...])
    carry = carry_ref[...]
    biases = []
    for g, page_lf in enumerate(pages_lf):
        w = within[3 * g * ROWS:3 * (g + 1) * ROWS]
        biases.append(carry + w[:ROWS] + w[ROWS:2 * ROWS] + w[2 * ROWS:])
        carry = carry + jnp.sum(page_lf, axis=-1, keepdims=True)
    carry_ref[...] = carry

    halves = _decode_halves(group)
    per = group // halves
    part = lambda refs, i: refs[i * per:(i + 1) * per]
    logits = [_dot_nt(qs_ref[...], both(part(kd_refs, i), part(kf_refs, i))) for i in range(halves)]
    m, l, acc = m_ref[...], l_ref[...], acc_ref[...]
    for i in range(halves):
        bias = jnp.concatenate(biases[i * per:(i + 1) * per], axis=1)
        s = jnp.concatenate([logits[i][:ROWS], logits[i][ROWS:] + bias], axis=0) + own_ref[...]
        m_new = jnp.maximum(m, jnp.max(s, axis=-1, keepdims=True))
        alpha = jnp.exp(m - m_new)
        pr = jnp.exp(s - m_new)
        l = alpha * l + jnp.sum(pr, axis=-1, keepdims=True)
        pv = _dot(pr.astype(BF16), both(part(vd_refs, i), part(vf_refs, i)))
        acc = alpha * acc + jnp.concatenate([pv[:ROWS, :HEAD_DIM], pv[ROWS:, HEAD_DIM:]], axis=0)
        m = m_new
    m_ref[...], l_ref[...], acc_ref[...] = m, l, acc

    @pl.when(p == n_steps - 1)
    def _():
        lam = _lambda_value(*lam_refs, lam_init)
        o = acc_ref[...] / l_ref[...]
        od_ref[0] = _sub_norm(o[:N_HEADS] - lam * o[N_HEADS:ROWS], gsub_ref[...], lam_init)
        of_ref[0] = o[ROWS:ROWS + N_HEADS]


def _decode_halves(group):
    return 2 if group % 2 == 0 else 1


def _decode_scratch(group):
    return [
        pltpu.VMEM((2 * ROWS, 2 * HEAD_DIM), BF16),
        pltpu.VMEM((2 * ROWS, 1), F32),
        pltpu.VMEM((2 * ROWS, 1), F32),
        pltpu.VMEM((2 * ROWS, HEAD_DIM), F32),
        pltpu.VMEM((ROWS, 1), F32),
        pltpu.VMEM((PAGE_SIZE, PAGE_COLS), BF16),
        pltpu.VMEM((2 * ROWS, group // _decode_halves(group) * PAGE_COLS), F32),
    ]


class _DecodePlan:
    def __init__(self, page_table, qd, qf, kd_new, vd_new, kf_new, vf_new, lf_new,
                 cache_kd, cache_vd, cache_kf, cache_vf, cache_lf, *, group, step_of, first_step, n_steps):
        n_samples, n_pages = page_table.shape
        assert n_pages % group == 0
        spp = n_pages // group
        assert first_step % spp == 0 and n_steps % spp == 0
        self.group, self.spp, self.first_step, self.n_steps = group, spp, first_step, n_steps
        self.pt_flat = page_table.reshape(-1)
        tile3 = lambda a: a.reshape(n_samples, N_HEADS, HEAD_DIM)
        first_sample = first_step // spp
        sample = lambda *ids: step_of(*ids) // spp
        page = lambda g: (lambda *ids_pt: ids_pt[-1][
            sample(*ids_pt[:-1]) * n_pages + n_pages - 1 - ((step_of(*ids_pt[:-1]) % spp) * group + g)])
        tile_spec = pl.BlockSpec((1, N_HEADS, HEAD_DIM), lambda *a: (sample(*a[:-1]), 0, 0))
        lfn_spec = pl.BlockSpec((1, N_HEADS, 1), lambda *a: (sample(*a[:-1]), 0, 0))
        page_specs = lambda: [pl.BlockSpec((None, PAGE_SIZE, N_HEADS, HEAD_DIM),
                                           lambda *a, f=page(g): (f(*a), 0, 0, 0)) for g in range(group)]
        lf_specs = [pl.BlockSpec((None, N_HEADS, PAGE_SIZE), lambda *a, f=page(g): (f(*a), 0, 0))
                    for g in range(group)]
        cache_lf_t = jnp.swapaxes(cache_lf, 1, 2)
        self.inputs = [tile3(qd), tile3(qf), tile3(kd_new), tile3(vd_new), tile3(kf_new), tile3(vf_new),
                       lf_new.reshape(n_samples, N_HEADS, 1)] + [cache_kd] * group + [cache_vd] * group \
            + [cache_kf] * group + [cache_vf] * group + [cache_lf_t] * group
        self.in_specs = [tile_spec] * 6 + [lfn_spec] + page_specs() + page_specs() + page_specs() + page_specs() \
            + lf_specs
        out_spec = pl.BlockSpec((1, N_HEADS, HEAD_DIM), lambda *a: (sample(*a[:-1]) - first_sample, 0, 0))
        self.out_specs = [out_spec, out_spec]
        out_shape = jax.ShapeDtypeStruct((n_steps // spp, N_HEADS, HEAD_DIM), F32)
        self.out_shapes = [out_shape, out_shape]


def _decode_kernel(pt_ref, lq1_ref, lk1_ref, lq2_ref, lk2_ref, gsub_ref, *refs, lam_init, group, spp):
    del pt_ref
    n_in = N_DEC_SMALL + 5 * group
    t = pl.program_id(0)
    _decode_step(t % spp, spp, t == 0, (lq1_ref, lk1_ref, lq2_ref, lk2_ref), gsub_ref,
                 refs[:N_DEC_SMALL], refs[N_DEC_SMALL:n_in], refs[n_in:n_in + 2], refs[n_in + 2:],
                 lam_init=lam_init, group=group)


def _decode(plan, lams, g_sub, *, n_steps, lam_init):
    small = lambda n: pl.BlockSpec((1, n), lambda t, pt: (0, 0))
    grid_spec = pltpu.PrefetchScalarGridSpec(
        num_scalar_prefetch=1,
        grid=(n_steps,),
        in_specs=[small(DIFF_HALF)] * 4 + [small(LANES)] + plan.in_specs,
        out_specs=tuple(plan.out_specs),
        scratch_shapes=_decode_scratch(plan.group),
    )
    return pl.pallas_call(
        functools.partial(_decode_kernel, lam_init=lam_init, group=plan.group, spp=plan.spp),
        grid_spec=grid_spec,
        out_shape=tuple(plan.out_shapes),
        compiler_params=_params(("arbitrary",), DECODE_VMEM_MIB),
        name="decode",
    )(plan.pt_flat, *lams, g_sub, *plan.inputs)


def _flash_update(s, cq, v_bf, m_ref, l_ref, acc_ref, h):
    reps = s.shape[1] // LANES
    m_prev = m_ref[h]
    m_curr = jnp.max(s, axis=-1, keepdims=True)
    if cq is not None:
        m_curr = m_curr + cq
    m_new = jnp.maximum(m_prev, m_curr)
    alpha = jnp.exp(m_prev - m_new)
    shift = m_new if cq is None else m_new - cq
    p = jnp.exp(s - jnp.tile(shift, (1, reps)))
    part = p[:, :LANES]
    for j in range(1, reps):
        part = part + p[:, j * LANES:(j + 1) * LANES]
    l_ref[h] = alpha * l_ref[h] + part
    acc_ref[h] = alpha * acc_ref[h] + _dot(p.astype(BF16), v_bf)
    m_ref[h] = m_new


def _flash_result(l_ref, acc_ref, h):
    return acc_ref[h] / jnp.sum(l_ref[h], axis=-1, keepdims=True)


def _causal_mask(s, qi, ki, tq, tk):
    rows = lax.broadcasted_iota(jnp.int32, s.shape, 0)
    qpos = qi * tq + jnp.where(rows >= tq, rows - tq, rows)
    kpos = ki * tk + lax.broadcasted_iota(jnp.int32, s.shape, 1)
    return jnp.where(kpos <= qpos, s, NEG)


def _attn_kernel(pt_ref, lq1_ref, lk1_ref, lq2_ref, lk2_ref, gsub_ref, q_ref, k_ref, v_ref, *refs,
                 diff, lam_init, dec):
    del pt_ref
    b, pair, j = pl.program_id(0), pl.program_id(1), pl.program_id(2)
    n_pairs, n_j = pl.num_programs(1), pl.num_programs(2)
    tq, tk = q_ref.shape[0], k_ref.shape[0]
    qi, ki = _folded(pair, j, n_j - 1)
    lam_refs = (lq1_ref, lk1_ref, lq2_ref, lk2_ref)
    n_extra = 0 if diff else 2
    n_dec_in = (N_DEC_SMALL + 5 * dec[0]) if dec else 0
    extra = refs[:n_extra]
    dec_in = refs[n_extra:n_extra + n_dec_in]
    outs = refs[n_extra + n_dec_in:n_extra + n_dec_in + (3 if dec else 1)]
    scratch = refs[n_extra + n_dec_in + len(outs):]
    o_ref = outs[0]
    if diff:
        qs_ref, m_ref, l_ref, acc_ref = scratch[:4]
        dec_scratch = scratch[4:]
    else:
        cq_ref, ckt_ref = extra
        qs_ref, cqr_ref, m_ref, l_ref, acc_ref = scratch[:5]
        dec_scratch = scratch[5:]

    def decode_step():
        if not dec:
            return
        group, spp, first_step, n_steps = dec
        local = (b * n_pairs + pair) * n_j + j

        @pl.when(local < n_steps)
        def _():
            _decode_step((first_step + local) % spp, spp, local == 0, lam_refs, gsub_ref, dec_in[:N_DEC_SMALL],
                         dec_in[N_DEC_SMALL:], outs[1:], dec_scratch, lam_init=lam_init, group=group)

    @pl.when(ki == 0)
    def _():
        if diff:
            lane = _lane_iota((tq, LANES))
            scale = DIFF_HALF ** -0.5
            for h in range(N_HEADS):
                qh = q_ref[:, h * HEAD_DIM:(h + 1) * HEAD_DIM].astype(F32) * scale
                qs_ref[h, :tq, :] = jnp.where(lane < DIFF_HALF, qh, 0.0).astype(BF16)
                qs_ref[h, tq:, :] = jnp.where(lane < DIFF_HALF, 0.0, qh).astype(BF16)
        else:
            qs_ref[...] = q_ref[...]
            for h in range(N_HEADS):
                cqr_ref[h] = jnp.broadcast_to(cq_ref[:, h:h + 1], (tq, LANES))
        m_ref[...] = jnp.full(m_ref.shape, NEG, F32)
        l_ref[...] = jnp.zeros(l_ref.shape, F32)
        acc_ref[...] = jnp.zeros(acc_ref.shape, F32)

    def step(masked):
        for h in range(N_HEADS):
            sl = slice(h * HEAD_DIM, (h + 1) * HEAD_DIM)
            kh = k_ref[:, sl]
            if diff:
                s = _dot_nt(qs_ref[h], kh)
                cq = None
            else:
                s = _dot_nt(qs_ref[:, sl], kh) - ckt_ref[0, h:h + 1, :]
                cq = cqr_ref[h]
            if masked:
                s = _causal_mask(s, qi, ki, tq, tk)
            _flash_update(s, cq, v_ref[:, sl], m_ref, l_ref, acc_ref, h)

    @pl.when(ki < qi)
    def _():
        step(False)
        decode_step()

    @pl.when(ki == qi)
    def _():
        step(True)
        if diff:
            lam = _lambda_value(*lam_refs, lam_init)
        for h in range(N_HEADS):
            o = _flash_result(l_ref, acc_ref, h)
            if diff:
                o = _sub_norm(o[:tq] - lam * o[tq:], gsub_ref[...], lam_init)
            o_ref[:, h * HEAD_DIM:(h + 1) * HEAD_DIM] = o.astype(o_ref.dtype)
        decode_step()


def _folded(pair, j, nq):
    first = j <= pair
    return jnp.where(first, pair, nq - 1 - pair), jnp.where(first, j, j - pair - 1)


def _attention(lams, g_sub, q_arr, q_col, k, v, fox_bias, plan, *, diff, batch, seq, tq, tk, lam_init, name):
    assert tq == tk and (seq // tq) % 2 == 0
    nq = seq // tq
    qrow = lambda b, r, j: b * nq + _folded(r, j, nq)[0]
    krow = lambda b, r, j: b * nq + _folded(r, j, nq)[1]
    small = lambda n: pl.BlockSpec((1, n), lambda b, r, j, pt: (0, 0))
    q_spec = pl.BlockSpec((tq, WIDTH), lambda b, r, j, pt: (qrow(b, r, j), q_col))
    kv_spec = pl.BlockSpec((tk, WIDTH), lambda b, r, j, pt: (krow(b, r, j), 0))
    o_spec = pl.BlockSpec((tq, WIDTH), lambda b, r, j, pt: (qrow(b, r, j), 0))
    in_specs = [small(DIFF_HALF)] * 4 + [small(LANES), q_spec, kv_spec, kv_spec]
    inputs = [*lams, g_sub, q_arr, k, v]
    rows = 2 * tq if diff else tq
    scratch = [pltpu.VMEM((N_HEADS, rows, HEAD_DIM), BF16) if diff else pltpu.VMEM((tq, WIDTH), BF16)]
    if not diff:
        in_specs += [pl.BlockSpec((tq, N_HEADS), lambda b, r, j, pt: (qrow(b, r, j), 0)),
                     pl.BlockSpec((1, N_HEADS, tk), lambda b, r, j, pt: (b, 0, _folded(r, j, nq)[1]))]
        inputs += list(fox_bias)
        scratch.append(pltpu.VMEM((N_HEADS, tq, LANES), F32))
    scratch += [pltpu.VMEM((N_HEADS, rows, LANES), F32), pltpu.VMEM((N_HEADS, rows, LANES), F32),
                pltpu.VMEM((N_HEADS, rows, HEAD_DIM), F32)]
    out_specs = [o_spec]
    out_shapes = [jax.ShapeDtypeStruct((batch * seq, WIDTH), BF16)]
    dec = None
    pt = jnp.zeros((1,), jnp.int32)
    if plan is not None:
        in_specs += plan.in_specs
        inputs += plan.inputs
        out_specs += plan.out_specs
        out_shapes += plan.out_shapes
        scratch += _decode_scratch(plan.group)
        dec = (plan.group, plan.spp, plan.first_step, plan.n_steps)
        pt = plan.pt_flat
    grid_spec = pltpu.PrefetchScalarGridSpec(
        num_scalar_prefetch=1, grid=(batch, nq // 2, nq + 1), in_specs=in_specs, out_specs=tuple(out_specs),
        scratch_shapes=scratch)
    return pl.pallas_call(
        functools.partial(_attn_kernel, diff=diff, lam_init=lam_init, dec=dec),
        grid_spec=grid_spec,
        out_shape=tuple(out_shapes),
        compiler_params=_params(("arbitrary", "arbitrary", "arbitrary"), ATTN_VMEM_MIB),
        name=name,
    )(pt, *inputs)


def _merge_kernel(od_ref, of_ref, sgd_ref, sgf_ref, x_ref, wbd_ref, wbf_ref, wo_ref, h_ref, mg_ref, *, n_col):
    j = pl.program_id(1)

    @pl.when(j < n_col)
    def _():
        a = _dot(od_ref[...].astype(BF16), wbd_ref[...])
        b = _dot(of_ref[...].astype(BF16), wbf_ref[...])
        mg_ref[j] = (sgd_ref[...] * a + sgf_ref[...] * b).astype(BF16)

    @pl.when(j >= n_col)
    def _():
        merged = jnp.concatenate([mg_ref[c] for c in range(n_col)], axis=1)
        h_ref[...] = x_ref[...] + _dot(merged, wo_ref[...])


def _merge(od, of, gates, x, w_bd, w_bf, w_o, *, tm, tn):
    m, d = x.shape
    n_col = d // tn
    first = lambda i, j: jnp.minimum(j, n_col - 1)
    second = lambda i, j: (i, jnp.maximum(j - n_col, 0))
    return pl.pallas_call(
        functools.partial(_merge_kernel, n_col=n_col),
        grid=(m // tm, 2 * n_col),
        in_specs=[
            pl.BlockSpec((tm, WIDTH), lambda i, j: (i, 0)),
            pl.BlockSpec((tm, WIDTH), lambda i, j: (i, 0)),
            pl.BlockSpec((tm, tn), lambda i, j: (i, first(i, j))),
            pl.BlockSpec((tm, tn), lambda i, j: (i, n_col + first(i, j))),
            pl.BlockSpec((tm, tn), second),
            pl.BlockSpec((WIDTH, tn), lambda i, j: (0, first(i, j))),
            pl.BlockSpec((WIDTH, tn), lambda i, j: (0, first(i, j))),
            pl.BlockSpec((d, tn), lambda i, j: (0, jnp.maximum(j - n_col, 0))),
        ],
        out_specs=pl.BlockSpec((tm, tn), second),
        out_shape=jax.ShapeDtypeStruct((m, d), F32),
        scratch_shapes=[pltpu.VMEM((n_col, tm, tn), BF16)],
        compiler_params=_params(("arbitrary", "arbitrary"), MERGE_VMEM_MIB),
        name="merge",
    )(od, of, gates, gates, x, w_bd, w_bf, w_o)


def _ffn_kernel(h_ref, g_ref, wg_ref, wu_ref, wd_ref, o_ref, hn_ref):
    @pl.when(pl.program_id(1) == 0)
    def _():
        h = h_ref[...]
        ms = jnp.mean(h * h, axis=-1, keepdims=True)
        hn_ref[...] = (h * lax.rsqrt(ms + EPS) * g_ref[...]).astype(BF16)
        o_ref[...] = h

    hn = hn_ref[...]
    a = _dot(hn, wg_ref[...])
    u = _dot(hn, wu_ref[...])
    ff = (a * _sigmoid(a) * u).astype(BF16)
    o_ref[...] += _dot(ff, wd_ref[...])


def _ffn(h, g, w_gate, w_up, w_down, *, tm, tf):
    m, d = h.shape
    f = w_gate.shape[1]
    return pl.pallas_call(
        _ffn_kernel,
        grid=(m // tm, f // tf),
        in_specs=[
            pl.BlockSpec((tm, d), lambda i, j: (i, 0)),
            pl.BlockSpec((1, d), lambda i, j: (0, 0)),
            pl.BlockSpec((d, tf), lambda i, j: (0, j)),
            pl.BlockSpec((d, tf), lambda i, j: (0, j)),
            pl.BlockSpec((tf, d), lambda i, j: (j, 0)),
        ],
        out_specs=pl.BlockSpec((tm, d), lambda i, j: (i, 0)),
        out_shape=jax.ShapeDtypeStruct((m, d), F32),
        scratch_shapes=[pltpu.VMEM((tm, d), BF16)],
        compiler_params=_params(("arbitrary", "arbitrary"), FFN_VMEM_MIB),
        name="ffn",
    )(h, g, w_gate, w_up, w_down)


def _rope_tables(pos):
    half = DIFF_HALF // 2
    inv = ROPE_THETA ** (-jnp.arange(half, dtype=F32) / half)
    ang = pos.astype(F32)[:, None] * inv[None, :]
    cos = jnp.concatenate([jnp.cos(ang)] * 4, axis=-1)
    sin = jnp.sin(ang)
    sin_signed = jnp.concatenate([-sin, sin, -sin, sin], axis=-1)
    return cos, sin_signed


def _tile(m, pref):
    return pref if m % pref == 0 else m


def _layer(l, xp, xs, caches, page_table, weights):
    (g_norm_attn, w_in, b_f, g_q_diff, g_k_diff, g_q_fox, g_k_fox, lambda_q1, lambda_k1, lambda_q2, lambda_k2,
     g_sub, w_branch_diff, w_branch_fox, w_o, g_norm_ffn, w_ffn_gate, w_ffn_up, w_ffn_down) = [w[l] for w in weights]
    batch, seq, d = xp.shape
    n_samples, dec_seq, _ = xs.shape
    past_len = page_table.shape[1] * PAGE_SIZE
    lam_init = 0.8 - 0.6 * math.exp(-0.3 * l)

    n_qkv = 6 * WIDTH
    w_all = w_in.astype(BF16)
    w_gates = w_all[:, n_qkv + N_HEADS:]
    w_f = jnp.pad(w_in[:, n_qkv:n_qkv + N_HEADS], ((0, 0), (0, LANES - N_HEADS))).astype(BF16)
    b_f_pad = jnp.pad(b_f, (0, LANES - N_HEADS)).reshape(1, LANES)
    two = lambda g: jnp.concatenate([g, g]).reshape(1, LANES)
    one = lambda g: g.reshape(1, -1)
    lams = tuple(one(v) for v in (lambda_q1, lambda_k1, lambda_q2, lambda_k2))
    w_bd, w_bf, w_out = w_branch_diff.astype(BF16), w_branch_fox.astype(BF16), w_o.astype(BF16)
    w_g, w_u, w_d = w_ffn_gate.astype(BF16), w_ffn_up.astype(BF16), w_ffn_down.astype(BF16)

    def project(x2d, pos, tm, rows_per_table):
        cos, sin_signed = _rope_tables(pos)
        return _proj(x2d, one(g_norm_attn), w_all, w_gates, w_f, b_f_pad, two(g_q_diff), two(g_k_diff), one(g_q_fox),
                     one(g_k_fox), cos, sin_signed, tm=tm, rows_per_table=rows_per_table)

    def finish(x2d, od, of, gates):
        rows = x2d.shape[0]
        h = _merge(od, of, gates, x2d, w_bd, w_bf, w_out, tm=_tile(rows, MERGE_ROWS), tn=MERGE_COLS)
        return _ffn(h, one(g_norm_ffn), w_g, w_u, w_d, tm=_tile(rows, FFN_ROWS), tf=FFN_COLS)

    assert dec_seq == 1
    xp2 = xp.reshape(batch * seq, d)
    xs2 = xs.reshape(n_samples, d)
    q_p, g_p, kd_p, vd_p, kf_p, vf_p, kdb_p, vdb_p, kfb_p, vfb_p, lf_p = project(xp2, jnp.arange(seq), _tile(seq, PROJ_ROWS), seq)
    pos_s = jnp.full((n_samples,), past_len, jnp.int32)
    q_s, g_s, kd_s, vd_s, kf_s, vf_s, _, _, _, _, lf_s = project(xs2, pos_s, n_samples, n_samples)
    q_s = q_s.astype(F32)
    cum, cumt = _cumsum(lf_p, batch, seq)

    n_pages = page_table.shape[1]
    tq = tk = _tile(seq, ATTN_BLOCK)
    nq = seq // tq
    n_host = batch * (nq // 2) * (nq + 1)
    group = math.gcd(DECODE_PAGES, n_pages)
    spp = n_pages // group
    n_half = (n_samples // 2) * spp
    hosted = n_samples % 2 == 0 and n_half <= n_host
    dec_args = (page_table, q_s[:, :WIDTH], q_s[:, WIDTH:], kd_s, vd_s, kf_s, vf_s, lf_s,
                *[c[l] for c in caches])

    def host_step(first):
        return lambda b, r, j: first + jnp.minimum((b * (nq // 2) + r) * (nq + 1) + j, n_half - 1)

    plans = [_DecodePlan(*dec_args, group=group, step_of=host_step(f), first_step=f, n_steps=n_half)
             for f in (0, n_half)] if hosted else [None, None]
    attn = functools.partial(_attention, lams, one(g_sub), q_p, batch=batch, seq=seq, tq=tq, tk=tk,
                             lam_init=lam_init)
    out_d = attn(0, kdb_p, vdb_p, None, plans[0], diff=True, name="diff_attention")
    out_f = attn(1, kfb_p, vfb_p, (cum, cumt), plans[1], diff=False, name="fox_attention")
    if hosted:
        od_s = jnp.concatenate([out_d[1], out_f[1]], axis=0).reshape(n_samples, WIDTH)
        of_s = jnp.concatenate([out_d[2], out_f[2]], axis=0).reshape(n_samples, WIDTH)
    else:
        n_dec = n_samples * spp
        plan = _DecodePlan(*dec_args, group=group, step_of=lambda t: t, first_step=0, n_steps=n_dec)
        od_s, of_s = (o.reshape(n_samples, WIDTH) for o in _decode(plan, lams, one(g_sub), n_steps=n_dec,
                                                                  lam_init=lam_init))
    yp = finish(xp2, out_d[0], out_f[0], g_p)
    ys = finish(xs2, od_s, of_s, g_s)

    heads = lambda a, b, t: a.reshape(b, t, N_HEADS, HEAD_DIM)
    new_p = (heads(kd_p, batch, seq), heads(vd_p, batch, seq), heads(kf_p, batch, seq), heads(vf_p, batch, seq),
             lf_p.reshape(batch, seq, N_HEADS))
    new_s = (heads(kd_s, n_samples, 1), heads(vd_s, n_samples, 1), heads(kf_s, n_samples, 1),
             heads(vf_s, n_samples, 1), lf_s.reshape(n_samples, 1, N_HEADS))
    return yp.reshape(batch, seq, d), ys.reshape(n_samples, dec_seq, d), new_p, new_s


def kernel(x_prompt, x_sample, cache_k_diff, cache_v_diff, cache_k_fox, cache_v_fox, cache_logf_fox, page_table,
           g_norm_attn, w_in, b_f, g_q_diff, g_k_diff, g_q_fox, g_k_fox, lambda_q1, lambda_k1, lambda_q2, lambda_k2,
           g_sub, w_branch_diff, w_branch_fox, w_o, g_norm_ffn, w_ffn_gate, w_ffn_up, w_ffn_down):
    weights = (g_norm_attn, w_in, b_f, g_q_diff, g_k_diff, g_q_fox, g_k_fox, lambda_q1, lambda_k1, lambda_q2,
               lambda_k2, g_sub, w_branch_diff, w_branch_fox, w_o, g_norm_ffn, w_ffn_gate, w_ffn_up, w_ffn_down)
    caches = (cache_k_diff, cache_v_diff, cache_k_fox, cache_v_fox, cache_logf_fox)
    depth = w_in.shape[0]
    xp, xs = x_prompt, x_sample
    new_p, new_s = [], []
    for l in range(depth):
        xp, xs, np_l, ns_l = _layer(l, xp, xs, caches, page_table, weights)
        new_p.append(np_l)
        new_s.append(ns_l)
    stack = lambda lst, i: jnp.stack([t[i] for t in lst], axis=0)
    return (xp, xs) + tuple(stack(new_p, i) for i in range(5)) + tuple(stack(new_s, i) for i in range(5))
```

```python
import functools
import math

import jax
import jax.numpy as jnp
from jax import lax
from jax.experimental import pallas as pl
from jax.experimental.pallas import tpu as pltpu

N_HEADS = 8
HEAD_DIM = 128
DIFF_HALF = HEAD_DIM // 2
WIDTH = N_HEADS * HEAD_DIM
ROPE_THETA = 10000.0
EPS = 1e-6
PAGE_SIZE = 128
LANES = 128
NEG = -1e30
MIB = 1024 * 1024

PROJ_ROWS = 512
PROJ_VMEM_MIB = 60
ATTN_BLOCK = 256
DECODE_PAGES = 8
ATTN_VMEM_MIB = 58
DECODE_VMEM_MIB = 52
MERGE_ROWS, MERGE_COLS = 1024, 512
MERGE_VMEM_MIB = 48
FFN_ROWS, FFN_COLS = 1024, 512
FFN_VMEM_MIB = 58
CUMSUM_VMEM_MIB = 32
GATES_ROWS, GATES_COLS = 1024, 1024
GATES_VMEM_MIB = 40

F32 = jnp.float32
BF16 = jnp.bfloat16
HIGHEST = lax.Precision.HIGHEST


def _params(semantics, vmem_mib):
    return pltpu.CompilerParams(dimension_semantics=semantics, vmem_limit_bytes=vmem_mib * MIB)


def _dot(a, b):
    return jnp.dot(a, b, preferred_element_type=F32)


def _dot_nt(a, b, precision=None):
    return lax.dot_general(a, b, (((1,), (1,)), ((), ())), preferred_element_type=F32, precision=precision)


def _lane_iota(shape):
    return lax.broadcasted_iota(jnp.int32, shape, len(shape) - 1)


def _rms_rope_head(a, g, cos, sin_signed, lane):
    sq = a * a
    lo = lane < DIFF_HALF
    s_lo = jnp.sum(jnp.where(lo, sq, 0.0), axis=-1, keepdims=True)
    s_hi = jnp.sum(jnp.where(lo, 0.0, sq), axis=-1, keepdims=True)
    ms = jnp.where(lo, s_lo, s_hi) * (1.0 / DIFF_HALF)
    y = a * lax.rsqrt(ms + EPS) * g
    first = (lane & (DIFF_HALF - 1)) < (DIFF_HALF // 2)
    rot = jnp.where(first, pltpu.roll(y, LANES - DIFF_HALF // 2, 1), pltpu.roll(y, DIFF_HALF // 2, 1))
    return y * cos + rot * sin_signed


def _rms_head(a, g):
    ms = jnp.mean(a * a, axis=-1, keepdims=True)
    return a * lax.rsqrt(ms + EPS) * g


def _log_sigmoid(z):
    return -(jnp.maximum(-z, 0.0) + jnp.log1p(jnp.exp(-jnp.abs(z))))


def _sigmoid(z):
    return 1.0 / (1.0 + jnp.exp(-z))


def _proj_kernel(x_ref, gn_ref, wqkv_ref, wf_ref, bf_ref, gqd_ref, gkd_ref, gqf_ref, gkf_ref, cos_ref, sin_ref,
                 q_ref, kd_ref, vd_ref, kf_ref, vf_ref, kdb_ref, vdb_ref, kfb_ref, vfb_ref, logf_ref, xn_ref):
    j = pl.program_id(1)
    tm = x_ref.shape[0]

    @pl.when(j == 0)
    def _():
        x = x_ref[...]
        ms = jnp.mean(x * x, axis=-1, keepdims=True)
        xn_ref[...] = (x * lax.rsqrt(ms + EPS) * gn_ref[...]).astype(BF16)
        z = _dot(xn_ref[...], wf_ref[...]) + bf_ref[...]
        logf_ref[...] = _log_sigmoid(z)[:, :N_HEADS]

    lane = _lane_iota((tm, LANES))
    pair = 2 * HEAD_DIM

    def head_pairs(w_ref):
        for c in range(WIDTH // pair):
            acc = _dot(xn_ref[...], w_ref[:, c * pair:(c + 1) * pair])
            for k in range(2):
                yield 2 * c + k, acc[:, k * HEAD_DIM:(k + 1) * HEAD_DIM]

    def heads(fn, out_ref):
        for h, a in head_pairs(wqkv_ref):
            out_ref[:, h * HEAD_DIM:(h + 1) * HEAD_DIM] = fn(a).astype(out_ref.dtype)

    def heads_kv(fn, out_ref, bf_out_ref):
        for h, a in head_pairs(wqkv_ref):
            val = fn(a)
            out_ref[pl.ds(h, tm, stride=N_HEADS), :] = val
            bf_out_ref[:, h * HEAD_DIM:(h + 1) * HEAD_DIM] = val.astype(BF16)

    @pl.when(j == 0)
    def _():
        heads(lambda a: _rms_rope_head(a, gqd_ref[...], cos_ref[...], sin_ref[...], lane), q_ref)

    @pl.when(j == 1)
    def _():
        heads(lambda a: _rms_head(a, gqf_ref[...]) * (HEAD_DIM ** -0.5), q_ref)

    @pl.when(j == 2)
    def _():
        heads_kv(lambda a: _rms_rope_head(a, gkd_ref[...], cos_ref[...], sin_ref[...], lane), kd_ref, kdb_ref)

    @pl.when(j == 3)
    def _():
        heads_kv(lambda a: a, vd_ref, vdb_ref)

    @pl.when(j == 4)
    def _():
        heads_kv(lambda a: _rms_head(a, gkf_ref[...]), kf_ref, kfb_ref)

    @pl.when(j == 5)
    def _():
        heads_kv(lambda a: a, vf_ref, vfb_ref)


def _proj(x, gn, w_qkv, w_f, b_f, gqd, gkd, gqf, gkf, cos, sin_signed, *, tm, rows_per_table):
    m, d = x.shape

    def qkv_tile(j):
        return jnp.where(j == 0, 0, jnp.where(j == 1, 3, jnp.where(j <= 3, j - 1, j)))

    n_tab = rows_per_table // tm
    row = lambda i, j: (i, 0)
    const = lambda i, j: (0, 0)
    kv_spec = pl.BlockSpec((tm * N_HEADS, HEAD_DIM), row)
    kv_shape = jax.ShapeDtypeStruct((m * N_HEADS, HEAD_DIM), F32)
    kvb_spec = pl.BlockSpec((tm, WIDTH), row)
    kvb_shape = jax.ShapeDtypeStruct((m, WIDTH), BF16)
    out_shape = (
        jax.ShapeDtypeStruct((m, 2 * WIDTH), BF16),
        kv_shape, kv_shape, kv_shape, kv_shape,
        kvb_shape, kvb_shape, kvb_shape, kvb_shape,
        jax.ShapeDtypeStruct((m, N_HEADS), F32),
    )
    return pl.pallas_call(
        _proj_kernel,
        grid=(m // tm, 6),
        in_specs=[
            pl.BlockSpec((tm, d), row),
            pl.BlockSpec((1, d), const),
            pl.BlockSpec((d, WIDTH), lambda i, j: (0, qkv_tile(j))),
            pl.BlockSpec((d, LANES), const),
            pl.BlockSpec((1, LANES), const),
            pl.BlockSpec((1, LANES), const),
            pl.BlockSpec((1, LANES), const),
            pl.BlockSpec((1, LANES), const),
            pl.BlockSpec((1, LANES), const),
            pl.BlockSpec((tm, LANES), lambda i, j: (i % n_tab, 0)),
            pl.BlockSpec((tm, LANES), lambda i, j: (i % n_tab, 0)),
        ],
        out_specs=(
            pl.BlockSpec((tm, WIDTH), lambda i, j: (i, jnp.minimum(j, 1))),
            kv_spec, kv_spec, kv_spec, kv_spec,
            kvb_spec, kvb_spec, kvb_spec, kvb_spec,
            pl.BlockSpec((tm, N_HEADS), row),
        ),
        out_shape=out_shape,
        scratch_shapes=[pltpu.VMEM((tm, d), BF16)],
        compiler_params=_params(("arbitrary", "arbitrary"), PROJ_VMEM_MIB),
        name="proj",
    )(x, gn, w_qkv, w_f, b_f, gqd, gkd, gqf, gkf, cos, sin_signed)


def _gates_kernel(x_ref, gn_ref, w_ref, g_ref, xn_ref):
    @pl.when(pl.program_id(1) == 0)
    def _():
        x = x_ref[...]
        ms = jnp.mean(x * x, axis=-1, keepdims=True)
        xn_ref[...] = (x * lax.rsqrt(ms + EPS) * gn_ref[...]).astype(BF16)

    g_ref[...] = _sigmoid(_dot(xn_ref[...], w_ref[...])).astype(g_ref.dtype)


def _gates(x, gn, w_g, *, tm, tn):
    m, d = x.shape
    return pl.pallas_call(
        _gates_kernel,
        grid=(m // tm, w_g.shape[1] // tn),
        in_specs=[
            pl.BlockSpec((tm, d), lambda i, j: (i, 0)),
            pl.BlockSpec((1, d), lambda i, j: (0, 0)),
            pl.BlockSpec((d, tn), lambda i, j: (0, j)),
        ],
        out_specs=pl.BlockSpec((tm, tn), lambda i, j: (i, j)),
        out_shape=jax.ShapeDtypeStruct((m, w_g.shape[1]), BF16),
        scratch_shapes=[pltpu.VMEM((tm, d), BF16)],
        compiler_params=_params(("arbitrary", "arbitrary"), GATES_VMEM_MIB),
        name="gates",
    )(x, gn, w_g)


def _pad_lanes(chunk, lane):
    out = jnp.zeros(lane.shape, F32)
    for h in range(N_HEADS):
        out = jnp.where(lane == h, chunk[:, h:h + 1], out)
    return out


def _cumsum_kernel(lf_ref, cum_ref, cumt_ref):
    s = lf_ref.shape[0]
    r = lax.broadcasted_iota(jnp.int32, (LANES, LANES), 0)
    c = lax.broadcasted_iota(jnp.int32, (LANES, LANES), 1)
    tri = (c <= r).astype(F32)
    carry = jnp.zeros((1, LANES), F32)
    for ci in range(s // LANES):
        rows = slice(ci * LANES, (ci + 1) * LANES)
        pad = _pad_lanes(lf_ref[rows, :], c)
        res = jnp.dot(tri, pad, preferred_element_type=F32, precision=HIGHEST) + carry
        cum_ref[rows, :] = res[:, :N_HEADS]
        cumt_ref[0, :, rows] = res.T[:N_HEADS, :]
        carry = res[LANES - 1:LANES, :]


def _cumsum(logf, batch, seq):
    return pl.pallas_call(
        _cumsum_kernel,
        grid=(batch,),
        in_specs=[pl.BlockSpec((seq, N_HEADS), lambda b: (b, 0))],
        out_specs=(
            pl.BlockSpec((seq, N_HEADS), lambda b: (b, 0)),
            pl.BlockSpec((1, N_HEADS, seq), lambda b: (b, 0, 0)),
        ),
        out_shape=(
            jax.ShapeDtypeStruct((batch * seq, N_HEADS), F32),
            jax.ShapeDtypeStruct((batch, N_HEADS, seq), F32),
        ),
        compiler_params=_params(("arbitrary",), CUMSUM_VMEM_MIB),
        name="cumsum",
    )(logf)


ROWS = 2 * N_HEADS
PAGE_COLS = PAGE_SIZE * N_HEADS
N_DEC_SMALL = 7


def _split3(a):
    hi = a.astype(BF16)
    r1 = a - hi.astype(F32)
    mid = r1.astype(BF16)
    lo = (r1 - mid.astype(F32)).astype(BF16)
    return hi, mid, lo


def _stack2(a):
    return jnp.concatenate([a, a], axis=0)


def _lambda_value(lq1_ref, lk1_ref, lq2_ref, lk2_ref, lam_init):
    a = jnp.sum(lq1_ref[...] * lk1_ref[...], axis=-1, keepdims=True)
    b = jnp.sum(lq2_ref[...] * lk2_ref[...], axis=-1, keepdims=True)
    return jnp.exp(a) - jnp.exp(b) + lam_init


def _sub_norm(o, g, lam_init):
    ms = jnp.mean(o * o, axis=-1, keepdims=True)
    return o * lax.rsqrt(ms + EPS) * g * (1.0 - lam_init)


def _decode_step(p, n_steps, very_first, lam_refs, gsub_ref, small_refs, page_refs, out_refs, scratch_refs,
                 *, lam_init, group):
    qd_ref, qf_ref, kdn_ref, vdn_ref, kfn_ref, vfn_ref, lfn_ref = small_refs
    kd_refs, vd_refs, kf_refs, vf_refs, lf_refs = (page_refs[i * group:(i + 1) * group] for i in range(5))
    od_ref, of_ref = out_refs
    qs_ref, m_ref, l_ref, acc_ref, carry_ref, later_ref, own_ref = scratch_refs

    @pl.when(very_first)
    def _():
        ks = lax.broadcasted_iota(jnp.int32, (PAGE_SIZE, PAGE_COLS), 0)
        kc = lax.broadcasted_iota(jnp.int32, (PAGE_SIZE, PAGE_COLS), 1)
        later_ref[...] = (ks > (kc >> 3)).astype(BF16)
        r = lax.broadcasted_iota(jnp.int32, own_ref.shape, 0)
        c = lax.broadcasted_iota(jnp.int32, own_ref.shape, 1)
        own_ref[...] = jnp.where((r & (N_HEADS - 1)) == (c & (N_HEADS - 1)), 0.0, NEG)

    @pl.when(p == 0)
    def _():
        lane = _lane_iota((N_HEADS, HEAD_DIM))
        q = qd_ref[0] * (DIFF_HALF ** -0.5)
        zeros = jnp.zeros((N_HEADS, HEAD_DIM), F32)
        q_d = jnp.concatenate([jnp.where(lane < DIFF_HALF, q, 0.0), jnp.where(lane < DIFF_HALF, 0.0, q)], axis=0)
        q_f = jnp.concatenate([qf_ref[0], zeros], axis=0)
        qs_ref[:ROWS, :HEAD_DIM] = q_d.astype(BF16)
        qs_ref[:ROWS, HEAD_DIM:] = jnp.zeros((ROWS, HEAD_DIM), BF16)
        qs_ref[ROWS:, :HEAD_DIM] = jnp.zeros((ROWS, HEAD_DIM), BF16)
        qs_ref[ROWS:, HEAD_DIM:] = q_f.astype(BF16)
        m_ref[:ROWS] = jnp.sum(q_d * _stack2(kdn_ref[0]), axis=-1, keepdims=True)
        m_ref[ROWS:] = jnp.sum(q_f * _stack2(kfn_ref[0]), axis=-1, keepdims=True)
        l_ref[...] = jnp.ones(l_ref.shape, F32)
        acc_ref[...] = jnp.concatenate([_stack2(vdn_ref[0]), _stack2(vfn_ref[0])], axis=0)
        carry_ref[...] = _stack2(lfn_ref[0])

    def both(d_refs, f_refs):
        flat = lambda refs: jnp.concatenate(
            [ref[...].reshape(PAGE_COLS, HEAD_DIM).astype(BF16) for ref in refs], axis=0)
        return jnp.concatenate([flat(d_refs), flat(f_refs)], axis=1)

    pages_lf = [_stack2(lf[...]) for lf in lf_refs]
    within = _dot(jnp.concatenate([t for page_lf in pages_lf for t in _split3(page_lf)], axis=0), later_ref[...])
    carry = carry_ref[...]
    biases = []
    for g, page_lf in enumerate(pages_lf):
        w = within[3 * g * ROWS:3 * (g + 1) * ROWS]
        biases.append(carry + w[:ROWS] + w[ROWS:2 * ROWS] + w[2 * ROWS:])
        carry = carry + jnp.sum(page_lf, axis=-1, keepdims=True)
    carry_ref[...] = carry

    halves = _decode_halves(group)
    per = group // halves
    part = lambda refs, i: refs[i * per:(i + 1) * per]
    logits = [_dot_nt(qs_ref[...], both(part(kd_refs, i), part(kf_refs, i))) for i in range(halves)]
    m, l, acc = m_ref[...], l_ref[...], acc_ref[...]
    for i in range(halves):
        bias = jnp.concatenate(biases[i * per:(i + 1) * per], axis=1)
        s = jnp.concatenate([logits[i][:ROWS], logits[i][ROWS:] + bias], axis=0) + own_ref[...]
        m_new = jnp.maximum(m, jnp.max(s, axis=-1, keepdims=True))
        alpha = jnp.exp(m - m_new)
        pr = jnp.exp(s - m_new)
        l = alpha * l + jnp.sum(pr, axis=-1, keepdims=True)
        pv = _dot(pr.astype(BF16), both(part(vd_refs, i), part(vf_refs, i)))
        acc = alpha * acc + jnp.concatenate([pv[:ROWS, :HEAD_DIM], pv[ROWS:, HEAD_DIM:]], axis=0)
        m = m_new
    m_ref[...], l_ref[...], acc_ref[...] = m, l, acc

    @pl.when(p == n_steps - 1)
    def _():
        lam = _lambda_value(*lam_refs, lam_init)
        o = acc_ref[...] / l_ref[...]
        od_ref[0] = _sub_norm(o[:N_HEADS] - lam * o[N_HEADS:ROWS], gsub_ref[...], lam_init)
        of_ref[0] = o[ROWS:ROWS + N_HEADS]


def _decode_halves(group):
    return 2 if group % 2 == 0 else 1


def _decode_scratch(group):
    return [
        pltpu.VMEM((2 * ROWS, 2 * HEAD_DIM), BF16),
        pltpu.VMEM((2 * ROWS, 1), F32),
        pltpu.VMEM((2 * ROWS, 1), F32),
        pltpu.VMEM((2 * ROWS, HEAD_DIM), F32),
        pltpu.VMEM((ROWS, 1), F32),
        pltpu.VMEM((PAGE_SIZE, PAGE_COLS), BF16),
        pltpu.VMEM((2 * ROWS, group // _decode_halves(group) * PAGE_COLS), F32),
    ]


class _DecodePlan:
    def __init__(self, page_table, qd, qf, kd_new, vd_new, kf_new, vf_new, lf_new,
                 cache_kd, cache_vd, cache_kf, cache_vf, cache_lf, *, group, step_of, first_step, n_steps):
        n_samples, n_pages = page_table.shape
        assert n_pages % group == 0
        spp = n_pages // group
        assert first_step % spp == 0 and n_steps % spp == 0
        self.group, self.spp, self.first_step, self.n_steps = group, spp, first_step, n_steps
        self.pt_flat = page_table.reshape(-1)
        tile3 = lambda a: a.reshape(n_samples, N_HEADS, HEAD_DIM)
        first_sample = first_step // spp
        sample = lambda *ids: step_of(*ids) // spp
        page = lambda g: (lambda *ids_pt: ids_pt[-1][
            sample(*ids_pt[:-1]) * n_pages + n_pages - 1 - ((step_of(*ids_pt[:-1]) % spp) * group + g)])
        tile_spec = pl.BlockSpec((1, N_HEADS, HEAD_DIM), lambda *a: (sample(*a[:-1]), 0, 0))
        lfn_spec = pl.BlockSpec((1, N_HEADS, 1), lambda *a: (sample(*a[:-1]), 0, 0))
        page_specs = lambda: [pl.BlockSpec((None, PAGE_SIZE, N_HEADS, HEAD_DIM),
                                           lambda *a, f=page(g): (f(*a), 0, 0, 0)) for g in range(group)]
        lf_specs = [pl.BlockSpec((None, N_HEADS, PAGE_SIZE), lambda *a, f=page(g): (f(*a), 0, 0))
                    for g in range(group)]
        cache_lf_t = jnp.swapaxes(cache_lf, 1, 2)
        self.inputs = [tile3(qd), tile3(qf), tile3(kd_new), tile3(vd_new), tile3(kf_new), tile3(vf_new),
                       lf_new.reshape(n_samples, N_HEADS, 1)] + [cache_kd] * group + [cache_vd] * group \
            + [cache_kf] * group + [cache_vf] * group + [cache_lf_t] * group
        self.in_specs = [tile_spec] * 6 + [lfn_spec] + page_specs() + page_specs() + page_specs() + page_specs() \
            + lf_specs
        out_spec = pl.BlockSpec((1, N_HEADS, HEAD_DIM), lambda *a: (sample(*a[:-1]) - first_sample, 0, 0))
        self.out_specs = [out_spec, out_spec]
        out_shape = jax.ShapeDtypeStruct((n_steps // spp, N_HEADS, HEAD_DIM), F32)
        self.out_shapes = [out_shape, out_shape]


def _decode_kernel(pt_ref, lq1_ref, lk1_ref, lq2_ref, lk2_ref, gsub_ref, *refs, lam_init, group, spp):
    del pt_ref
    n_in = N_DEC_SMALL + 5 * group
    t = pl.program_id(0)
    _decode_step(t % spp, spp, t == 0, (lq1_ref, lk1_ref, lq2_ref, lk2_ref), gsub_ref,
                 refs[:N_DEC_SMALL], refs[N_DEC_SMALL:n_in], refs[n_in:n_in + 2], refs[n_in + 2:],
                 lam_init=lam_init, group=group)


def _decode(plan, lams, g_sub, *, n_steps, lam_init):
    small = lambda n: pl.BlockSpec((1, n), lambda t, pt: (0, 0))
    grid_spec = pltpu.PrefetchScalarGridSpec(
        num_scalar_prefetch=1,
        grid=(n_steps,),
        in_specs=[small(DIFF_HALF)] * 4 + [small(LANES)] + plan.in_specs,
        out_specs=tuple(plan.out_specs),
        scratch_shapes=_decode_scratch(plan.group),
    )
    return pl.pallas_call(
        functools.partial(_decode_kernel, lam_init=lam_init, group=plan.group, spp=plan.spp),
        grid_spec=grid_spec,
        out_shape=tuple(plan.out_shapes),
        compiler_params=_params(("arbitrary",), DECODE_VMEM_MIB),
        name="decode",
    )(plan.pt_flat, *lams, g_sub, *plan.inputs)


def _flash_update(s, cq, v_bf, m_ref, l_ref, acc_ref, h):
    reps = s.shape[1] // LANES
    m_prev = m_ref[h]
    m_curr = jnp.max(s, axis=-1, keepdims=True)
    if cq is not None:
        m_curr = m_curr + cq
    m_new = jnp.maximum(m_prev, m_curr)
    alpha = jnp.exp(m_prev - m_new)
    shift = m_new if cq is None else m_new - cq
    p = jnp.exp(s - jnp.tile(shift, (1, reps)))
    part = p[:, :LANES]
    for j in range(1, reps):
        part = part + p[:, j * LANES:(j + 1) * LANES]
    l_ref[h] = alpha * l_ref[h] + part
    acc_ref[h] = alpha * acc_ref[h] + _dot(p.astype(BF16), v_bf)
    m_ref[h] = m_new


def _flash_result(l_ref, acc_ref, h):
    return acc_ref[h] / jnp.sum(l_ref[h], axis=-1, keepdims=True)


def _causal_mask(s, qi, ki, tq, tk):
    rows = lax.broadcasted_iota(jnp.int32, s.shape, 0)
    qpos = qi * tq + jnp.where(rows >= tq, rows - tq, rows)
    kpos = ki * tk + lax.broadcasted_iota(jnp.int32, s.shape, 1)
    return jnp.where(kpos <= qpos, s, NEG)


def _attn_kernel(pt_ref, lq1_ref, lk1_ref, lq2_ref, lk2_ref, gsub_ref, q_ref, k_ref, v_ref, *refs,
                 diff, lam_init, dec):
    del pt_ref
    b, pair, j = pl.program_id(0), pl.program_id(1), pl.program_id(2)
    n_pairs, n_j = pl.num_programs(1), pl.num_programs(2)
    tq, tk = q_ref.shape[0], k_ref.shape[0]
    qi, ki = _folded(pair, j, n_j - 1)
    lam_refs = (lq1_ref, lk1_ref, lq2_ref, lk2_ref)
    n_extra = 0 if diff else 2
    n_dec_in = (N_DEC_SMALL + 5 * dec[0]) if dec else 0
    extra = refs[:n_extra]
    dec_in = refs[n_extra:n_extra + n_dec_in]
    outs = refs[n_extra + n_dec_in:n_extra + n_dec_in + (3 if dec else 1)]
    scratch = refs[n_extra + n_dec_in + len(outs):]
    o_ref = outs[0]
    if diff:
        qs_ref, m_ref, l_ref, acc_ref = scratch[:4]
        dec_scratch = scratch[4:]
    else:
        cq_ref, ckt_ref = extra
        qs_ref, cqr_ref, m_ref, l_ref, acc_ref = scratch[:5]
        dec_scratch = scratch[5:]

    def decode_step():
        if not dec:
            return
        group, spp, first_step, n_steps = dec
        local = (b * n_pairs + pair) * n_j + j

        @pl.when(local < n_steps)
        def _():
            _decode_step((first_step + local) % spp, spp, local == 0, lam_refs, gsub_ref, dec_in[:N_DEC_SMALL],
                         dec_in[N_DEC_SMALL:], outs[1:], dec_scratch, lam_init=lam_init, group=group)

    @pl.when(ki == 0)
    def _():
        if diff:
            lane = _lane_iota((tq, LANES))
            scale = DIFF_HALF ** -0.5
            for h in range(N_HEADS):
                qh = q_ref[:, h * HEAD_DIM:(h + 1) * HEAD_DIM].astype(F32) * scale
                qs_ref[h, :tq, :] = jnp.where(lane < DIFF_HALF, qh, 0.0).astype(BF16)
                qs_ref[h, tq:, :] = jnp.where(lane < DIFF_HALF, 0.0, qh).astype(BF16)
        else:
            qs_ref[...] = q_ref[...]
            for h in range(N_HEADS):
                cqr_ref[h] = jnp.broadcast_to(cq_ref[:, h:h + 1], (tq, LANES))
        m_ref[...] = jnp.full(m_ref.shape, NEG, F32)
        l_ref[...] = jnp.zeros(l_ref.shape, F32)
        acc_ref[...] = jnp.zeros(acc_ref.shape, F32)

    def step(masked):
        for h in range(N_HEADS):
            sl = slice(h * HEAD_DIM, (h + 1) * HEAD_DIM)
            kh = k_ref[:, sl]
            if diff:
                s = _dot_nt(qs_ref[h], kh)
                cq = None
            else:
                s = _dot_nt(qs_ref[:, sl], kh) - ckt_ref[0, h:h + 1, :]
                cq = cqr_ref[h]
            if masked:
                s = _causal_mask(s, qi, ki, tq, tk)
            _flash_update(s, cq, v_ref[:, sl], m_ref, l_ref, acc_ref, h)

    @pl.when(ki < qi)
    def _():
        step(False)
        decode_step()

    @pl.when(ki == qi)
    def _():
        step(True)
        if diff:
            lam = _lambda_value(*lam_refs, lam_init)
        for h in range(N_HEADS):
            o = _flash_result(l_ref, acc_ref, h)
            if diff:
                o = _sub_norm(o[:tq] - lam * o[tq:], gsub_ref[...], lam_init)
            o_ref[:, h * HEAD_DIM:(h + 1) * HEAD_DIM] = o.astype(o_ref.dtype)
        decode_step()


def _folded(pair, j, nq):
    first = j <= pair
    return jnp.where(first, pair, nq - 1 - pair), jnp.where(first, j, j - pair - 1)


def _attention(lams, g_sub, q_arr, q_col, k, v, fox_bias, plan, *, diff, batch, seq, tq, tk, lam_init, name):
    assert tq == tk and (seq // tq) % 2 == 0
    nq = seq // tq
    qrow = lambda b, r, j: b * nq + _folded(r, j, nq)[0]
    krow = lambda b, r, j: b * nq + _folded(r, j, nq)[1]
    small = lambda n: pl.BlockSpec((1, n), lambda b, r, j, pt: (0, 0))
    q_spec = pl.BlockSpec((tq, WIDTH), lambda b, r, j, pt: (qrow(b, r, j), q_col))
    kv_spec = pl.BlockSpec((tk, WIDTH), lambda b, r, j, pt: (krow(b, r, j), 0))
    o_spec = pl.BlockSpec((tq, WIDTH), lambda b, r, j, pt: (qrow(b, r, j), 0))
    in_specs = [small(DIFF_HALF)] * 4 + [small(LANES), q_spec, kv_spec, kv_spec]
    inputs = [*lams, g_sub, q_arr, k, v]
    rows = 2 * tq if diff else tq
    scratch = [pltpu.VMEM((N_HEADS, rows, HEAD_DIM), BF16) if diff else pltpu.VMEM((tq, WIDTH), BF16)]
    if not diff:
        in_specs += [pl.BlockSpec((tq, N_HEADS), lambda b, r, j, pt: (qrow(b, r, j), 0)),
                     pl.BlockSpec((1, N_HEADS, tk), lambda b, r, j, pt: (b, 0, _folded(r, j, nq)[1]))]
        inputs += list(fox_bias)
        scratch.append(pltpu.VMEM((N_HEADS, tq, LANES), F32))
    scratch += [pltpu.VMEM((N_HEADS, rows, LANES), F32), pltpu.VMEM((N_HEADS, rows, LANES), F32),
                pltpu.VMEM((N_HEADS, rows, HEAD_DIM), F32)]
    out_specs = [o_spec]
    out_shapes = [jax.ShapeDtypeStruct((batch * seq, WIDTH), BF16)]
    dec = None
    pt = jnp.zeros((1,), jnp.int32)
    if plan is not None:
        in_specs += plan.in_specs
        inputs += plan.inputs
        out_specs += plan.out_specs
        out_shapes += plan.out_shapes
        scratch += _decode_scratch(plan.group)
        dec = (plan.group, plan.spp, plan.first_step, plan.n_steps)
        pt = plan.pt_flat
    grid_spec = pltpu.PrefetchScalarGridSpec(
        num_scalar_prefetch=1, grid=(batch, nq // 2, nq + 1), in_specs=in_specs, out_specs=tuple(out_specs),
        scratch_shapes=scratch)
    return pl.pallas_call(
        functools.partial(_attn_kernel, diff=diff, lam_init=lam_init, dec=dec),
        grid_spec=grid_spec,
        out_shape=tuple(out_shapes),
        compiler_params=_params(("arbitrary", "arbitrary", "arbitrary"), ATTN_VMEM_MIB),
        name=name,
    )(pt, *inputs)


def _merge_kernel(od_ref, of_ref, sgd_ref, sgf_ref, x_ref, wbd_ref, wbf_ref, wo_ref, h_ref, mg_ref, *, n_col):
    j = pl.program_id(1)

    @pl.when(j < n_col)
    def _():
        a = _dot(od_ref[...].astype(BF16), wbd_ref[...])
        b = _dot(of_ref[...].astype(BF16), wbf_ref[...])
        mg_ref[j] = (sgd_ref[...] * a + sgf_ref[...] * b).astype(BF16)

    @pl.when(j >= n_col)
    def _():
        merged = jnp.concatenate([mg_ref[c] for c in range(n_col)], axis=1)
        h_ref[...] = x_ref[...] + _dot(merged, wo_ref[...])


def _merge(od, of, gates, x, w_bd, w_bf, w_o, *, tm, tn):
    m, d = x.shape
    n_col = d // tn
    first = lambda i, j: jnp.minimum(j, n_col - 1)
    second = lambda i, j: (i, jnp.maximum(j - n_col, 0))
    return pl.pallas_call(
        functools.partial(_merge_kernel, n_col=n_col),
        grid=(m // tm, 2 * n_col),
        in_specs=[
            pl.BlockSpec((tm, WIDTH), lambda i, j: (i, 0)),
            pl.BlockSpec((tm, WIDTH), lambda i, j: (i, 0)),
            pl.BlockSpec((tm, tn), lambda i, j: (i, first(i, j))),
            pl.BlockSpec((tm, tn), lambda i, j: (i, n_col + first(i, j))),
            pl.BlockSpec((tm, tn), second),
            pl.BlockSpec((WIDTH, tn), lambda i, j: (0, first(i, j))),
            pl.BlockSpec((WIDTH, tn), lambda i, j: (0, first(i, j))),
            pl.BlockSpec((d, tn), lambda i, j: (0, jnp.maximum(j - n_col, 0))),
        ],
        out_specs=pl.BlockSpec((tm, tn), second),
        out_shape=jax.ShapeDtypeStruct((m, d), F32),
        scratch_shapes=[pltpu.VMEM((n_col, tm, tn), BF16)],
        compiler_params=_params(("arbitrary", "arbitrary"), MERGE_VMEM_MIB),
        name="merge",
    )(od, of, gates, gates, x, w_bd, w_bf, w_o)


def _ffn_kernel(h_ref, g_ref, wg_ref, wu_ref, wd_ref, o_ref, hn_ref):
    @pl.when(pl.program_id(1) == 0)
    def _():
        h = h_ref[...]
        ms = jnp.mean(h * h, axis=-1, keepdims=True)
        hn_ref[...] = (h * lax.rsqrt(ms + EPS) * g_ref[...]).astype(BF16)
        o_ref[...] = h

    hn = hn_ref[...]
    a = _dot(hn, wg_ref[...])
    u = _dot(hn, wu_ref[...])
    ff = (a * _sigmoid(a) * u).astype(BF16)
    o_ref[...] += _dot(ff, wd_ref[...])


def _ffn(h, g, w_gate, w_up, w_down, *, tm, tf):
    m, d = h.shape
    f = w_gate.shape[1]
    return pl.pallas_call(
        _ffn_kernel,
        grid=(m // tm, f // tf),
        in_specs=[
            pl.BlockSpec((tm, d), lambda i, j: (i, 0)),
            pl.BlockSpec((1, d), lambda i, j: (0, 0)),
            pl.BlockSpec((d, tf), lambda i, j: (0, j)),
            pl.BlockSpec((d, tf), lambda i, j: (0, j)),
            pl.BlockSpec((tf, d), lambda i, j: (j, 0)),
        ],
        out_specs=pl.BlockSpec((tm, d), lambda i, j: (i, 0)),
        out_shape=jax.ShapeDtypeStruct((m, d), F32),
        scratch_shapes=[pltpu.VMEM((tm, d), BF16)],
        compiler_params=_params(("arbitrary", "arbitrary"), FFN_VMEM_MIB),
        name="ffn",
    )(h, g, w_gate, w_up, w_down)


def _rope_tables(pos):
    half = DIFF_HALF // 2
    inv = ROPE_THETA ** (-jnp.arange(half, dtype=F32) / half)
    ang = pos.astype(F32)[:, None] * inv[None, :]
    cos = jnp.concatenate([jnp.cos(ang)] * 4, axis=-1)
    sin = jnp.sin(ang)
    sin_signed = jnp.concatenate([-sin, sin, -sin, sin], axis=-1)
    return cos, sin_signed


def _tile(m, pref):
    return pref if m % pref == 0 else m


def _layer(l, xp, xs, caches, page_table, weights):
    (g_norm_attn, w_in, b_f, g_q_diff, g_k_diff, g_q_fox, g_k_fox, lambda_q1, lambda_k1, lambda_q2, lambda_k2,
     g_sub, w_branch_diff, w_branch_fox, w_o, g_norm_ffn, w_ffn_gate, w_ffn_up, w_ffn_down) = [w[l] for w in weights]
    batch, seq, d = xp.shape
    n_samples, dec_seq, _ = xs.shape
    past_len = page_table.shape[1] * PAGE_SIZE
    lam_init = 0.8 - 0.6 * math.exp(-0.3 * l)

    n_qkv = 6 * WIDTH
    w_all = w_in.astype(BF16)
    w_gates = w_all[:, n_qkv + N_HEADS:]
    w_f = jnp.pad(w_in[:, n_qkv:n_qkv + N_HEADS], ((0, 0), (0, LANES - N_HEADS))).astype(BF16)
    b_f_pad = jnp.pad(b_f, (0, LANES - N_HEADS)).reshape(1, LANES)
    two = lambda g: jnp.concatenate([g, g]).reshape(1, LANES)
    one = lambda g: g.reshape(1, -1)
    lams = tuple(one(v) for v in (lambda_q1, lambda_k1, lambda_q2, lambda_k2))
    w_bd, w_bf, w_out = w_branch_diff.astype(BF16), w_branch_fox.astype(BF16), w_o.astype(BF16)
    w_g, w_u, w_d = w_ffn_gate.astype(BF16), w_ffn_up.astype(BF16), w_ffn_down.astype(BF16)

    def project(x2d, pos, tm, rows_per_table):
        cos, sin_signed = _rope_tables(pos)
        gates = _gates(x2d, one(g_norm_attn), w_gates, tm=_tile(x2d.shape[0], GATES_ROWS), tn=GATES_COLS)
        outs = _proj(x2d, one(g_norm_attn), w_all, w_f, b_f_pad, two(g_q_diff), two(g_k_diff), one(g_q_fox),
                     one(g_k_fox), cos, sin_signed, tm=tm, rows_per_table=rows_per_table)
        return (outs[0], gates) + tuple(outs[1:])

    def finish(x2d, od, of, gates):
        rows = x2d.shape[0]
        h = _merge(od, of, gates, x2d, w_bd, w_bf, w_out, tm=_tile(rows, MERGE_ROWS), tn=MERGE_COLS)
        return _ffn(h, one(g_norm_ffn), w_g, w_u, w_d, tm=_tile(rows, FFN_ROWS), tf=FFN_COLS)

    assert dec_seq == 1
    xp2 = xp.reshape(batch * seq, d)
    xs2 = xs.reshape(n_samples, d)
    q_p, g_p, kd_p, vd_p, kf_p, vf_p, kdb_p, vdb_p, kfb_p, vfb_p, lf_p = project(xp2, jnp.arange(seq), _tile(seq, PROJ_ROWS), seq)
    pos_s = jnp.full((n_samples,), past_len, jnp.int32)
    q_s, g_s, kd_s, vd_s, kf_s, vf_s, _, _, _, _, lf_s = project(xs2, pos_s, n_samples, n_samples)
    q_s = q_s.astype(F32)
    cum, cumt = _cumsum(lf_p, batch, seq)

    n_pages = page_table.shape[1]
    tq = tk = _tile(seq, ATTN_BLOCK)
    nq = seq // tq
    n_host = batch * (nq // 2) * (nq + 1)
    group = math.gcd(DECODE_PAGES, n_pages)
    spp = n_pages // group
    n_half = (n_samples // 2) * spp
    hosted = n_samples % 2 == 0 and n_half <= n_host
    dec_args = (page_table, q_s[:, :WIDTH], q_s[:, WIDTH:], kd_s, vd_s, kf_s, vf_s, lf_s,
                *[c[l] for c in caches])

    def host_step(first):
        return lambda b, r, j: first + jnp.minimum((b * (nq // 2) + r) * (nq + 1) + j, n_half - 1)

    plans = [_DecodePlan(*dec_args, group=group, step_of=host_step(f), first_step=f, n_steps=n_half)
             for f in (0, n_half)] if hosted else [None, None]
    attn = functools.partial(_attention, lams, one(g_sub), q_p, batch=batch, seq=seq, tq=tq, tk=tk,
                             lam_init=lam_init)
    out_d = attn(0, kdb_p, vdb_p, None, plans[0], diff=True, name="diff_attention")
    out_f = attn(1, kfb_p, vfb_p, (cum, cumt), plans[1], diff=False, name="fox_attention")
    if hosted:
        od_s = jnp.concatenate([out_d[1], out_f[1]], axis=0).reshape(n_samples, WIDTH)
        of_s = jnp.concatenate([out_d[2], out_f[2]], axis=0).reshape(n_samples, WIDTH)
    else:
        n_dec = n_samples * spp
        plan = _DecodePlan(*dec_args, group=group, step_of=lambda t: t, first_step=0, n_steps=n_dec)
        od_s, of_s = (o.reshape(n_samples, WIDTH) for o in _decode(plan, lams, one(g_sub), n_steps=n_dec,
                                                                  lam_init=lam_init))
    yp = finish(xp2, out_d[0], out_f[0], g_p)
    ys = finish(xs2, od_s, of_s, g_s)

    heads = lambda a, b, t: a.reshape(b, t, N_HEADS, HEAD_DIM)
    new_p = (heads(kd_p, batch, seq), heads(vd_p, batch, seq), heads(kf_p, batch, seq), heads(vf_p, batch, seq),
             lf_p.reshape(batch, seq, N_HEADS))
    new_s = (heads(kd_s, n_samples, 1), heads(vd_s, n_samples, 1), heads(kf_s, n_samples, 1),
             heads(vf_s, n_samples, 1), lf_s.reshape(n_samples, 1, N_HEADS))
    return yp.reshape(batch, seq, d), ys.reshape(n_samples, dec_seq, d), new_p, new_s


def kernel(x_prompt, x_sample, cache_k_diff, cache_v_diff, cache_k_fox, cache_v_fox, cache_logf_fox, page_table,
           g_norm_attn, w_in, b_f, g_q_diff, g_k_diff, g_q_fox, g_k_fox, lambda_q1, lambda_k1, lambda_q2, lambda_k2,
           g_sub, w_branch_diff, w_branch_fox, w_o, g_norm_ffn, w_ffn_gate, w_ffn_up, w_ffn_down):
    weights = (g_norm_attn, w_in, b_f, g_q_diff, g_k_diff, g_q_fox, g_k_fox, lambda_q1, lambda_k1, lambda_q2,
               lambda_k2, g_sub, w_branch_diff, w_branch_fox, w_o, g_norm_ffn, w_ffn_gate, w_ffn_up, w_ffn_down)
    caches = (cache_k_diff, cache_v_diff, cache_k_fox, cache_v_fox, cache_logf_fox)
    depth = w_in.shape[0]
    xp, xs = x_prompt, x_sample
    new_p, new_s = [], []
    for l in range(depth):
        xp, xs, np_l, ns_l = _layer(l, xp, xs, caches, page_table, weights)
        new_p.append(np_l)
        new_s.append(ns_l)
    stack = lambda lst, i: jnp.stack([t[i] for t in lst], axis=0)
    return (xp, xs) + tuple(stack(new_p, i) for i in range(5)) + tuple(stack(new_s, i) for i in range(5))
```

```python
import functools
import math

import jax
import jax.numpy as jnp
from jax import lax
from jax.experimental import pallas as pl
from jax.experimental.pallas import tpu as pltpu

N_HEADS = 8
HEAD_DIM = 128
DIFF_HALF = HEAD_DIM // 2
WIDTH = N_HEADS * HEAD_DIM
ROPE_THETA = 10000.0
EPS = 1e-6
PAGE_SIZE = 128
LANES = 128
NEG = -1e30
MIB = 1024 * 1024

PROJ_ROWS = 512
PROJ_VMEM_MIB = 60
ATTN_BLOCK = 256
DECODE_PAGES = 8
ATTN_VMEM_MIB = 58
DECODE_VMEM_MIB = 52
MERGE_ROWS, MERGE_COLS = 1024, 1024
MERGE_VMEM_MIB = 48
FFN_ROWS, FFN_COLS = 1024, 512
FFN_VMEM_MIB = 58
CUMSUM_VMEM_MIB = 32
GATES_ROWS, GATES_COLS = 1024, 1024
GATES_VMEM_MIB = 40

F32 = jnp.float32
BF16 = jnp.bfloat16
HIGHEST = lax.Precision.HIGHEST


def _params(semantics, vmem_mib):
    return pltpu.CompilerParams(dimension_semantics=semantics, vmem_limit_bytes=vmem_mib * MIB)


def _dot(a, b):
    return jnp.dot(a, b, preferred_element_type=F32)


def _dot_nt(a, b, precision=None):
    return lax.dot_general(a, b, (((1,), (1,)), ((), ())), preferred_element_type=F32, precision=precision)


def _lane_iota(shape):
    return lax.broadcasted_iota(jnp.int32, shape, len(shape) - 1)


def _rms_rope_head(a, g, cos, sin_signed, lane):
    sq = a * a
    lo = lane < DIFF_HALF
    s_lo = jnp.sum(jnp.where(lo, sq, 0.0), axis=-1, keepdims=True)
    s_hi = jnp.sum(jnp.where(lo, 0.0, sq), axis=-1, keepdims=True)
    ms = jnp.where(lo, s_lo, s_hi) * (1.0 / DIFF_HALF)
    y = a * lax.rsqrt(ms + EPS) * g
    first = (lane & (DIFF_HALF - 1)) < (DIFF_HALF // 2)
    rot = jnp.where(first, pltpu.roll(y, LANES - DIFF_HALF // 2, 1), pltpu.roll(y, DIFF_HALF // 2, 1))
    return y * cos + rot * sin_signed


def _rms_head(a, g):
    ms = jnp.mean(a * a, axis=-1, keepdims=True)
    return a * lax.rsqrt(ms + EPS) * g


def _log_sigmoid(z):
    return -(jnp.maximum(-z, 0.0) + jnp.log1p(jnp.exp(-jnp.abs(z))))


def _sigmoid(z):
    return 1.0 / (1.0 + jnp.exp(-z))


def _proj_kernel(x_ref, gn_ref, wqkv_ref, wf_ref, bf_ref, gqd_ref, gkd_ref, gqf_ref, gkf_ref, cos_ref, sin_ref,
                 q_ref, kd_ref, vd_ref, kf_ref, vf_ref, kdb_ref, vdb_ref, kfb_ref, vfb_ref, logf_ref, xn_ref):
    j = pl.program_id(1)
    tm = x_ref.shape[0]

    @pl.when(j == 0)
    def _():
        x = x_ref[...]
        ms = jnp.mean(x * x, axis=-1, keepdims=True)
        xn_ref[...] = (x * lax.rsqrt(ms + EPS) * gn_ref[...]).astype(BF16)
        z = _dot(xn_ref[...], wf_ref[...]) + bf_ref[...]
        logf_ref[...] = _log_sigmoid(z)[:, :N_HEADS]

    lane = _lane_iota((tm, LANES))
    pair = 2 * HEAD_DIM

    def head_pairs(w_ref):
        for c in range(WIDTH // pair):
            acc = _dot(xn_ref[...], w_ref[:, c * pair:(c + 1) * pair])
            for k in range(2):
                yield 2 * c + k, acc[:, k * HEAD_DIM:(k + 1) * HEAD_DIM]

    def heads(fn, out_ref):
        for h, a in head_pairs(wqkv_ref):
            out_ref[:, h * HEAD_DIM:(h + 1) * HEAD_DIM] = fn(a).astype(out_ref.dtype)

    def heads_kv(fn, out_ref, bf_out_ref):
        for h, a in head_pairs(wqkv_ref):
            val = fn(a)
            out_ref[pl.ds(h, tm, stride=N_HEADS), :] = val
            bf_out_ref[:, h * HEAD_DIM:(h + 1) * HEAD_DIM] = val.astype(BF16)

    @pl.when(j == 0)
    def _():
        heads(lambda a: _rms_rope_head(a, gqd_ref[...], cos_ref[...], sin_ref[...], lane), q_ref)

    @pl.when(j == 1)
    def _():
        heads(lambda a: _rms_head(a, gqf_ref[...]) * (HEAD_DIM ** -0.5), q_ref)

    @pl.when(j == 2)
    def _():
        heads_kv(lambda a: _rms_rope_head(a, gkd_ref[...], cos_ref[...], sin_ref[...], lane), kd_ref, kdb_ref)

    @pl.when(j == 3)
    def _():
        heads_kv(lambda a: a, vd_ref, vdb_ref)

    @pl.when(j == 4)
    def _():
        heads_kv(lambda a: _rms_head(a, gkf_ref[...]), kf_ref, kfb_ref)

    @pl.when(j == 5)
    def _():
        heads_kv(lambda a: a, vf_ref, vfb_ref)


def _proj(x, gn, w_qkv, w_f, b_f, gqd, gkd, gqf, gkf, cos, sin_signed, *, tm, rows_per_table):
    m, d = x.shape

    def qkv_tile(j):
        return jnp.where(j == 0, 0, jnp.where(j == 1, 3, jnp.where(j <= 3, j - 1, j)))

    n_tab = rows_per_table // tm
    row = lambda i, j: (i, 0)
    const = lambda i, j: (0, 0)
    kv_spec = pl.BlockSpec((tm * N_HEADS, HEAD_DIM), row)
    kv_shape = jax.ShapeDtypeStruct((m * N_HEADS, HEAD_DIM), F32)
    kvb_spec = pl.BlockSpec((tm, WIDTH), row)
    kvb_shape = jax.ShapeDtypeStruct((m, WIDTH), BF16)
    out_shape = (
        jax.ShapeDtypeStruct((m, 2 * WIDTH), BF16),
        kv_shape, kv_shape, kv_shape, kv_shape,
        kvb_shape, kvb_shape, kvb_shape, kvb_shape,
        jax.ShapeDtypeStruct((m, N_HEADS), F32),
    )
    return pl.pallas_call(
        _proj_kernel,
        grid=(m // tm, 6),
        in_specs=[
            pl.BlockSpec((tm, d), row),
            pl.BlockSpec((1, d), const),
            pl.BlockSpec((d, WIDTH), lambda i, j: (0, qkv_tile(j))),
            pl.BlockSpec((d, LANES), const),
            pl.BlockSpec((1, LANES), const),
            pl.BlockSpec((1, LANES), const),
            pl.BlockSpec((1, LANES), const),
            pl.BlockSpec((1, LANES), const),
            pl.BlockSpec((1, LANES), const),
            pl.BlockSpec((tm, LANES), lambda i, j: (i % n_tab, 0)),
            pl.BlockSpec((tm, LANES), lambda i, j: (i % n_tab, 0)),
        ],
        out_specs=(
            pl.BlockSpec((tm, WIDTH), lambda i, j: (i, jnp.minimum(j, 1))),
            kv_spec, kv_spec, kv_spec, kv_spec,
            kvb_spec, kvb_spec, kvb_spec, kvb_spec,
            pl.BlockSpec((tm, N_HEADS), row),
        ),
        out_shape=out_shape,
        scratch_shapes=[pltpu.VMEM((tm, d), BF16)],
        compiler_params=_params(("arbitrary", "arbitrary"), PROJ_VMEM_MIB),
        name="proj",
    )(x, gn, w_qkv, w_f, b_f, gqd, gkd, gqf, gkf, cos, sin_signed)


def _gates_kernel(x_ref, gn_ref, w_ref, g_ref, xn_ref):
    @pl.when(pl.program_id(1) == 0)
    def _():
        x = x_ref[...]
        ms = jnp.mean(x * x, axis=-1, keepdims=True)
        xn_ref[...] = (x * lax.rsqrt(ms + EPS) * gn_ref[...]).astype(BF16)

    g_ref[...] = _sigmoid(_dot(xn_ref[...], w_ref[...])).astype(g_ref.dtype)


def _gates(x, gn, w_g, *, tm, tn):
    m, d = x.shape
    return pl.pallas_call(
        _gates_kernel,
        grid=(m // tm, w_g.shape[1] // tn),
        in_specs=[
            pl.BlockSpec((tm, d), lambda i, j: (i, 0)),
            pl.BlockSpec((1, d), lambda i, j: (0, 0)),
            pl.BlockSpec((d, tn), lambda i, j: (0, j)),
        ],
        out_specs=pl.BlockSpec((tm, tn), lambda i, j: (i, j)),
        out_shape=jax.ShapeDtypeStruct((m, w_g.shape[1]), BF16),
        scratch_shapes=[pltpu.VMEM((tm, d), BF16)],
        compiler_params=_params(("arbitrary", "arbitrary"), GATES_VMEM_MIB),
        name="gates",
    )(x, gn, w_g)


def _pad_lanes(chunk, lane):
    out = jnp.zeros(lane.shape, F32)
    for h in range(N_HEADS):
        out = jnp.where(lane == h, chunk[:, h:h + 1], out)
    return out


def _cumsum_kernel(lf_ref, cum_ref, cumt_ref):
    s = lf_ref.shape[0]
    r = lax.broadcasted_iota(jnp.int32, (LANES, LANES), 0)
    c = lax.broadcasted_iota(jnp.int32, (LANES, LANES), 1)
    tri = (c <= r).astype(F32)
    carry = jnp.zeros((1, LANES), F32)
    for ci in range(s // LANES):
        rows = slice(ci * LANES, (ci + 1) * LANES)
        pad = _pad_lanes(lf_ref[rows, :], c)
        res = jnp.dot(tri, pad, preferred_element_type=F32, precision=HIGHEST) + carry
        cum_ref[rows, :] = res[:, :N_HEADS]
        cumt_ref[0, :, rows] = res.T[:N_HEADS, :]
        carry = res[LANES - 1:LANES, :]


def _cumsum(logf, batch, seq):
    return pl.pallas_call(
        _cumsum_kernel,
        grid=(batch,),
        in_specs=[pl.BlockSpec((seq, N_HEADS), lambda b: (b, 0))],
        out_specs=(
            pl.BlockSpec((seq, N_HEADS), lambda b: (b, 0)),
            pl.BlockSpec((1, N_HEADS, seq), lambda b: (b, 0, 0)),
        ),
        out_shape=(
            jax.ShapeDtypeStruct((batch * seq, N_HEADS), F32),
            jax.ShapeDtypeStruct((batch, N_HEADS, seq), F32),
        ),
        compiler_params=_params(("arbitrary",), CUMSUM_VMEM_MIB),
        name="cumsum",
    )(logf)


ROWS = 2 * N_HEADS
PAGE_COLS = PAGE_SIZE * N_HEADS
N_DEC_SMALL = 7


def _split3(a):
    hi = a.astype(BF16)
    r1 = a - hi.astype(F32)
    mid = r1.astype(BF16)
    lo = (r1 - mid.astype(F32)).astype(BF16)
    return hi, mid, lo


def _stack2(a):
    return jnp.concatenate([a, a], axis=0)


def _lambda_value(lq1_ref, lk1_ref, lq2_ref, lk2_ref, lam_init):
    a = jnp.sum(lq1_ref[...] * lk1_ref[...], axis=-1, keepdims=True)
    b = jnp.sum(lq2_ref[...] * lk2_ref[...], axis=-1, keepdims=True)
    return jnp.exp(a) - jnp.exp(b) + lam_init


def _sub_norm(o, g, lam_init):
    ms = jnp.mean(o * o, axis=-1, keepdims=True)
    return o * lax.rsqrt(ms + EPS) * g * (1.0 - lam_init)


def _decode_step(p, n_steps, very_first, lam_refs, gsub_ref, small_refs, page_refs, out_refs, scratch_refs,
                 *, lam_init, group):
    qd_ref, qf_ref, kdn_ref, vdn_ref, kfn_ref, vfn_ref, lfn_ref = small_refs
    kd_refs, vd_refs, kf_refs, vf_refs, lf_refs = (page_refs[i * group:(i + 1) * group] for i in range(5))
    od_ref, of_ref = out_refs
    qs_ref, m_ref, l_ref, acc_ref, carry_ref, later_ref, own_ref = scratch_refs

    @pl.when(very_first)
    def _():
        ks = lax.broadcasted_iota(jnp.int32, (PAGE_SIZE, PAGE_COLS), 0)
        kc = lax.broadcasted_iota(jnp.int32, (PAGE_SIZE, PAGE_COLS), 1)
        later_ref[...] = (ks > (kc >> 3)).astype(BF16)
        r = lax.broadcasted_iota(jnp.int32, own_ref.shape, 0)
        c = lax.broadcasted_iota(jnp.int32, own_ref.shape, 1)
        own_ref[...] = jnp.where((r & (N_HEADS - 1)) == (c & (N_HEADS - 1)), 0.0, NEG)

    @pl.when(p == 0)
    def _():
        lane = _lane_iota((N_HEADS, HEAD_DIM))
        q = qd_ref[0] * (DIFF_HALF ** -0.5)
        zeros = jnp.zeros((N_HEADS, HEAD_DIM), F32)
        q_d = jnp.concatenate([jnp.where(lane < DIFF_HALF, q, 0.0), jnp.where(lane < DIFF_HALF, 0.0, q)], axis=0)
        q_f = jnp.concatenate([qf_ref[0], zeros], axis=0)
        qs_ref[:ROWS, :HEAD_DIM] = q_d.astype(BF16)
        qs_ref[:ROWS, HEAD_DIM:] = jnp.zeros((ROWS, HEAD_DIM), BF16)
        qs_ref[ROWS:, :HEAD_DIM] = jnp.zeros((ROWS, HEAD_DIM), BF16)
        qs_ref[ROWS:, HEAD_DIM:] = q_f.astype(BF16)
        m_ref[:ROWS] = jnp.sum(q_d * _stack2(kdn_ref[0]), axis=-1, keepdims=True)
        m_ref[ROWS:] = jnp.sum(q_f * _stack2(kfn_ref[0]), axis=-1, keepdims=True)
        l_ref[...] = jnp.ones(l_ref.shape, F32)
        acc_ref[...] = jnp.concatenate([_stack2(vdn_ref[0]), _stack2(vfn_ref[0])], axis=0)
        carry_ref[...] = _stack2(lfn_ref[0])

    def both(d_refs, f_refs):
        flat = lambda refs: jnp.concatenate(
            [ref[...].reshape(PAGE_COLS, HEAD_DIM).astype(BF16) for ref in refs], axis=0)
        return jnp.concatenate([flat(d_refs), flat(f_refs)], axis=1)

    pages_lf = [_stack2(lf[...]) for lf in lf_refs]
    within = _dot(jnp.concatenate([t for page_lf in pages_lf for t in _split3(page_lf)], axis=0), later_ref[...])
    carry = carry_ref[...]
    biases = []
    for g, page_lf in enumerate(pages_lf):
        w = within[3 * g * ROWS:3 * (g + 1) * ROWS]
        biases.append(carry + w[:ROWS] + w[ROWS:2 * ROWS] + w[2 * ROWS:])
        carry = carry + jnp.sum(page_lf, axis=-1, keepdims=True)
    carry_ref[...] = carry

    halves = _decode_halves(group)
    per = group // halves
    part = lambda refs, i: refs[i * per:(i + 1) * per]
    logits = [_dot_nt(qs_ref[...], both(part(kd_refs, i), part(kf_refs, i))) for i in range(halves)]
    m, l, acc = m_ref[...], l_ref[...], acc_ref[...]
    for i in range(halves):
        bias = jnp.concatenate(biases[i * per:(i + 1) * per], axis=1)
        s = jnp.concatenate([logits[i][:ROWS], logits[i][ROWS:] + bias], axis=0) + own_ref[...]
        m_new = jnp.maximum(m, jnp.max(s, axis=-1, keepdims=True))
        alpha = jnp.exp(m - m_new)
        pr = jnp.exp(s - m_new)
        l = alpha * l + jnp.sum(pr, axis=-1, keepdims=True)
        pv = _dot(pr.astype(BF16), both(part(vd_refs, i), part(vf_refs, i)))
        acc = alpha * acc + jnp.concatenate([pv[:ROWS, :HEAD_DIM], pv[ROWS:, HEAD_DIM:]], axis=0)
        m = m_new
    m_ref[...], l_ref[...], acc_ref[...] = m, l, acc

    @pl.when(p == n_steps - 1)
    def _():
        lam = _lambda_value(*lam_refs, lam_init)
        o = acc_ref[...] / l_ref[...]
        od_ref[0] = _sub_norm(o[:N_HEADS] - lam * o[N_HEADS:ROWS], gsub_ref[...], lam_init)
        of_ref[0] = o[ROWS:ROWS + N_HEADS]


def _decode_halves(group):
    return 2 if group % 2 == 0 else 1


def _decode_scratch(group):
    return [
        pltpu.VMEM((2 * ROWS, 2 * HEAD_DIM), BF16),
        pltpu.VMEM((2 * ROWS, 1), F32),
        pltpu.VMEM((2 * ROWS, 1), F32),
        pltpu.VMEM((2 * ROWS, HEAD_DIM), F32),
        pltpu.VMEM((ROWS, 1), F32),
        pltpu.VMEM((PAGE_SIZE, PAGE_COLS), BF16),
        pltpu.VMEM((2 * ROWS, group // _decode_halves(group) * PAGE_COLS), F32),
    ]


class _DecodePlan:
    def __init__(self, page_table, qd, qf, kd_new, vd_new, kf_new, vf_new, lf_new,
                 cache_kd, cache_vd, cache_kf, cache_vf, cache_lf, *, group, step_of, first_step, n_steps):
        n_samples, n_pages = page_table.shape
        assert n_pages % group == 0
        spp = n_pages // group
        assert first_step % spp == 0 and n_steps % spp == 0
        self.group, self.spp, self.first_step, self.n_steps = group, spp, first_step, n_steps
        self.pt_flat = page_table.reshape(-1)
        tile3 = lambda a: a.reshape(n_samples, N_HEADS, HEAD_DIM)
        first_sample = first_step // spp
        sample = lambda *ids: step_of(*ids) // spp
        page = lambda g: (lambda *ids_pt: ids_pt[-1][
            sample(*ids_pt[:-1]) * n_pages + n_pages - 1 - ((step_of(*ids_pt[:-1]) % spp) * group + g)])
        tile_spec = pl.BlockSpec((1, N_HEADS, HEAD_DIM), lambda *a: (sample(*a[:-1]), 0, 0))
        lfn_spec = pl.BlockSpec((1, N_HEADS, 1), lambda *a: (sample(*a[:-1]), 0, 0))
        page_specs = lambda: [pl.BlockSpec((None, PAGE_SIZE, N_HEADS, HEAD_DIM),
                                           lambda *a, f=page(g): (f(*a), 0, 0, 0)) for g in range(group)]
        lf_specs = [pl.BlockSpec((None, N_HEADS, PAGE_SIZE), lambda *a, f=page(g): (f(*a), 0, 0))
                    for g in range(group)]
        cache_lf_t = jnp.swapaxes(cache_lf, 1, 2)
        self.inputs = [tile3(qd), tile3(qf), tile3(kd_new), tile3(vd_new), tile3(kf_new), tile3(vf_new),
                       lf_new.reshape(n_samples, N_HEADS, 1)] + [cache_kd] * group + [cache_vd] * group \
            + [cache_kf] * group + [cache_vf] * group + [cache_lf_t] * group
        self.in_specs = [tile_spec] * 6 + [lfn_spec] + page_specs() + page_specs() + page_specs() + page_specs() \
            + lf_specs
        out_spec = pl.BlockSpec((1, N_HEADS, HEAD_DIM), lambda *a: (sample(*a[:-1]) - first_sample, 0, 0))
        self.out_specs = [out_spec, out_spec]
        out_shape = jax.ShapeDtypeStruct((n_steps // spp, N_HEADS, HEAD_DIM), F32)
        self.out_shapes = [out_shape, out_shape]


def _decode_kernel(pt_ref, lq1_ref, lk1_ref, lq2_ref, lk2_ref, gsub_ref, *refs, lam_init, group, spp):
    del pt_ref
    n_in = N_DEC_SMALL + 5 * group
    t = pl.program_id(0)
    _decode_step(t % spp, spp, t == 0, (lq1_ref, lk1_ref, lq2_ref, lk2_ref), gsub_ref,
                 refs[:N_DEC_SMALL], refs[N_DEC_SMALL:n_in], refs[n_in:n_in + 2], refs[n_in + 2:],
                 lam_init=lam_init, group=group)


def _decode(plan, lams, g_sub, *, n_steps, lam_init):
    small = lambda n: pl.BlockSpec((1, n), lambda t, pt: (0, 0))
    grid_spec = pltpu.PrefetchScalarGridSpec(
        num_scalar_prefetch=1,
        grid=(n_steps,),
        in_specs=[small(DIFF_HALF)] * 4 + [small(LANES)] + plan.in_specs,
        out_specs=tuple(plan.out_specs),
        scratch_shapes=_decode_scratch(plan.group),
    )
    return pl.pallas_call(
        functools.partial(_decode_kernel, lam_init=lam_init, group=plan.group, spp=plan.spp),
        grid_spec=grid_spec,
        out_shape=tuple(plan.out_shapes),
        compiler_params=_params(("arbitrary",), DECODE_VMEM_MIB),
        name="decode",
    )(plan.pt_flat, *lams, g_sub, *plan.inputs)


def _flash_update(s, cq, v_bf, m_ref, l_ref, acc_ref, h):
    reps = s.shape[1] // LANES
    m_prev = m_ref[h]
    m_curr = jnp.max(s, axis=-1, keepdims=True)
    if cq is not None:
        m_curr = m_curr + cq
    m_new = jnp.maximum(m_prev, m_curr)
    alpha = jnp.exp(m_prev - m_new)
    shift = m_new if cq is None else m_new - cq
    p = jnp.exp(s - jnp.tile(shift, (1, reps)))
    part = p[:, :LANES]
    for j in range(1, reps):
        part = part + p[:, j * LANES:(j + 1) * LANES]
    l_ref[h] = alpha * l_ref[h] + part
    acc_ref[h] = alpha * acc_ref[h] + _dot(p.astype(BF16), v_bf)
    m_ref[h] = m_new


def _flash_result(l_ref, acc_ref, h):
    return acc_ref[h] / jnp.sum(l_ref[h], axis=-1, keepdims=True)


def _causal_mask(s, qi, ki, tq, tk):
    rows = lax.broadcasted_iota(jnp.int32, s.shape, 0)
    qpos = qi * tq + jnp.where(rows >= tq, rows - tq, rows)
    kpos = ki * tk + lax.broadcasted_iota(jnp.int32, s.shape, 1)
    return jnp.where(kpos <= qpos, s, NEG)


def _attn_kernel(pt_ref, lq1_ref, lk1_ref, lq2_ref, lk2_ref, gsub_ref, q_ref, k_ref, v_ref, *refs,
                 diff, lam_init, dec):
    del pt_ref
    b, pair, j = pl.program_id(0), pl.program_id(1), pl.program_id(2)
    n_pairs, n_j = pl.num_programs(1), pl.num_programs(2)
    tq, tk = q_ref.shape[0], k_ref.shape[0]
    qi, ki = _folded(pair, j, n_j - 1)
    lam_refs = (lq1_ref, lk1_ref, lq2_ref, lk2_ref)
    n_extra = 0 if diff else 2
    n_dec_in = (N_DEC_SMALL + 5 * dec[0]) if dec else 0
    extra = refs[:n_extra]
    dec_in = refs[n_extra:n_extra + n_dec_in]
    outs = refs[n_extra + n_dec_in:n_extra + n_dec_in + (3 if dec else 1)]
    scratch = refs[n_extra + n_dec_in + len(outs):]
    o_ref = outs[0]
    if diff:
        qs_ref, m_ref, l_ref, acc_ref = scratch[:4]
        dec_scratch = scratch[4:]
    else:
        cq_ref, ckt_ref = extra
        qs_ref, cqr_ref, m_ref, l_ref, acc_ref = scratch[:5]
        dec_scratch = scratch[5:]

    def decode_step():
        if not dec:
            return
        group, spp, first_step, n_steps = dec
        local = (b * n_pairs + pair) * n_j + j

        @pl.when(local < n_steps)
        def _():
            _decode_step((first_step + local) % spp, spp, local == 0, lam_refs, gsub_ref, dec_in[:N_DEC_SMALL],
                         dec_in[N_DEC_SMALL:], outs[1:], dec_scratch, lam_init=lam_init, group=group)

    @pl.when(ki == 0)
    def _():
        if diff:
            lane = _lane_iota((tq, LANES))
            scale = DIFF_HALF ** -0.5
            for h in range(N_HEADS):
                qh = q_ref[:, h * HEAD_DIM:(h + 1) * HEAD_DIM].astype(F32) * scale
                qs_ref[h, :tq, :] = jnp.where(lane < DIFF_HALF, qh, 0.0).astype(BF16)
                qs_ref[h, tq:, :] = jnp.where(lane < DIFF_HALF, 0.0, qh).astype(BF16)
        else:
            qs_ref[...] = q_ref[...]
            for h in range(N_HEADS):
                cqr_ref[h] = jnp.broadcast_to(cq_ref[:, h:h + 1], (tq, LANES))
        m_ref[...] = jnp.full(m_ref.shape, NEG, F32)
        l_ref[...] = jnp.zeros(l_ref.shape, F32)
        acc_ref[...] = jnp.zeros(acc_ref.shape, F32)

    def step(masked):
        for h in range(N_HEADS):
            sl = slice(h * HEAD_DIM, (h + 1) * HEAD_DIM)
            kh = k_ref[:, sl]
            if diff:
                s = _dot_nt(qs_ref[h], kh)
                cq = None
            else:
                s = _dot_nt(qs_ref[:, sl], kh) - ckt_ref[0, h:h + 1, :]
                cq = cqr_ref[h]
            if masked:
                s = _causal_mask(s, qi, ki, tq, tk)
            _flash_update(s, cq, v_ref[:, sl], m_ref, l_ref, acc_ref, h)

    @pl.when(ki < qi)
    def _():
        step(False)
        decode_step()

    @pl.when(ki == qi)
    def _():
        step(True)
        if diff:
            lam = _lambda_value(*lam_refs, lam_init)
        for h in range(N_HEADS):
            o = _flash_result(l_ref, acc_ref, h)
            if diff:
                o = _sub_norm(o[:tq] - lam * o[tq:], gsub_ref[...], lam_init)
            o_ref[:, h * HEAD_DIM:(h + 1) * HEAD_DIM] = o.astype(o_ref.dtype)
        decode_step()


def _folded(pair, j, nq):
    first = j <= pair
    return jnp.where(first, pair, nq - 1 - pair), jnp.where(first, j, j - pair - 1)


def _attention(lams, g_sub, q_arr, q_col, k, v, fox_bias, plan, *, diff, batch, seq, tq, tk, lam_init, name):
    assert tq == tk and (seq // tq) % 2 == 0
    nq = seq // tq
    qrow = lambda b, r, j: b * nq + _folded(r, j, nq)[0]
    krow = lambda b, r, j: b * nq + _folded(r, j, nq)[1]
    small = lambda n: pl.BlockSpec((1, n), lambda b, r, j, pt: (0, 0))
    q_spec = pl.BlockSpec((tq, WIDTH), lambda b, r, j, pt: (qrow(b, r, j), q_col))
    kv_spec = pl.BlockSpec((tk, WIDTH), lambda b, r, j, pt: (krow(b, r, j), 0))
    o_spec = pl.BlockSpec((tq, WIDTH), lambda b, r, j, pt: (qrow(b, r, j), 0))
    in_specs = [small(DIFF_HALF)] * 4 + [small(LANES), q_spec, kv_spec, kv_spec]
    inputs = [*lams, g_sub, q_arr, k, v]
    rows = 2 * tq if diff else tq
    scratch = [pltpu.VMEM((N_HEADS, rows, HEAD_DIM), BF16) if diff else pltpu.VMEM((tq, WIDTH), BF16)]
    if not diff:
        in_specs += [pl.BlockSpec((tq, N_HEADS), lambda b, r, j, pt: (qrow(b, r, j), 0)),
                     pl.BlockSpec((1, N_HEADS, tk), lambda b, r, j, pt: (b, 0, _folded(r, j, nq)[1]))]
        inputs += list(fox_bias)
        scratch.append(pltpu.VMEM((N_HEADS, tq, LANES), F32))
    scratch += [pltpu.VMEM((N_HEADS, rows, LANES), F32), pltpu.VMEM((N_HEADS, rows, LANES), F32),
                pltpu.VMEM((N_HEADS, rows, HEAD_DIM), F32)]
    out_specs = [o_spec]
    out_shapes = [jax.ShapeDtypeStruct((batch * seq, WIDTH), BF16)]
    dec = None
    pt = jnp.zeros((1,), jnp.int32)
    if plan is not None:
        in_specs += plan.in_specs
        inputs += plan.inputs
        out_specs += plan.out_specs
        out_shapes += plan.out_shapes
        scratch += _decode_scratch(plan.group)
        dec = (plan.group, plan.spp, plan.first_step, plan.n_steps)
        pt = plan.pt_flat
    grid_spec = pltpu.PrefetchScalarGridSpec(
        num_scalar_prefetch=1, grid=(batch, nq // 2, nq + 1), in_specs=in_specs, out_specs=tuple(out_specs),
        scratch_shapes=scratch)
    return pl.pallas_call(
        functools.partial(_attn_kernel, diff=diff, lam_init=lam_init, dec=dec),
        grid_spec=grid_spec,
        out_shape=tuple(out_shapes),
        compiler_params=_params(("arbitrary", "arbitrary", "arbitrary"), ATTN_VMEM_MIB),
        name=name,
    )(pt, *inputs)


def _branch_kernel(od_ref, of_ref, sgd_ref, sgf_ref, wbd_ref, wbf_ref, mg_ref):
    a = _dot(od_ref[...].astype(BF16), wbd_ref[...])
    b = _dot(of_ref[...].astype(BF16), wbf_ref[...])
    mg_ref[...] = (sgd_ref[...] * a + sgf_ref[...] * b).astype(BF16)


def _outproj_kernel(mg_ref, x_ref, wo_ref, h_ref):
    h_ref[...] = x_ref[...] + _dot(mg_ref[...], wo_ref[...])


def _merge(od, of, gates, x, w_bd, w_bf, w_o, *, tm, tn):
    m, d = x.shape
    n_col = d // tn
    merged = pl.pallas_call(
        _branch_kernel,
        grid=(m // tm, n_col),
        in_specs=[
            pl.BlockSpec((tm, WIDTH), lambda i, j: (i, 0)),
            pl.BlockSpec((tm, WIDTH), lambda i, j: (i, 0)),
            pl.BlockSpec((tm, tn), lambda i, j: (i, j)),
            pl.BlockSpec((tm, tn), lambda i, j: (i, n_col + j)),
            pl.BlockSpec((WIDTH, tn), lambda i, j: (0, j)),
            pl.BlockSpec((WIDTH, tn), lambda i, j: (0, j)),
        ],
        out_specs=pl.BlockSpec((tm, tn), lambda i, j: (i, j)),
        out_shape=jax.ShapeDtypeStruct((m, d), BF16),
        compiler_params=_params(("arbitrary", "arbitrary"), MERGE_VMEM_MIB),
        name="branch_merge",
    )(od, of, gates, gates, w_bd, w_bf)
    return pl.pallas_call(
        _outproj_kernel,
        grid=(m // tm, n_col),
        in_specs=[
            pl.BlockSpec((tm, d), lambda i, j: (i, 0)),
            pl.BlockSpec((tm, tn), lambda i, j: (i, j)),
            pl.BlockSpec((d, tn), lambda i, j: (0, j)),
        ],
        out_specs=pl.BlockSpec((tm, tn), lambda i, j: (i, j)),
        out_shape=jax.ShapeDtypeStruct((m, d), F32),
        compiler_params=_params(("arbitrary", "arbitrary"), MERGE_VMEM_MIB),
        name="out_proj",
    )(merged, x, w_o)


def _ffn_kernel(h_ref, g_ref, wg_ref, wu_ref, wd_ref, o_ref, hn_ref):
    @pl.when(pl.program_id(1) == 0)
    def _():
        h = h_ref[...]
        ms = jnp.mean(h * h, axis=-1, keepdims=True)
        hn_ref[...] = (h * lax.rsqrt(ms + EPS) * g_ref[...]).astype(BF16)
        o_ref[...] = h

    hn = hn_ref[...]
    a = _dot(hn, wg_ref[...])
    u = _dot(hn, wu_ref[...])
    ff = (a * _sigmoid(a) * u).astype(BF16)
    o_ref[...] += _dot(ff, wd_ref[...])


def _ffn(h, g, w_gate, w_up, w_down, *, tm, tf):
    m, d = h.shape
    f = w_gate.shape[1]
    return pl.pallas_call(
        _ffn_kernel,
        grid=(m // tm, f // tf),
        in_specs=[
            pl.BlockSpec((tm, d), lambda i, j: (i, 0)),
            pl.BlockSpec((1, d), lambda i, j: (0, 0)),
            pl.BlockSpec((d, tf), lambda i, j: (0, j)),
            pl.BlockSpec((d, tf), lambda i, j: (0, j)),
            pl.BlockSpec((tf, d), lambda i, j: (j, 0)),
        ],
        out_specs=pl.BlockSpec((tm, d), lambda i, j: (i, 0)),
        out_shape=jax.ShapeDtypeStruct((m, d), F32),
        scratch_shapes=[pltpu.VMEM((tm, d), BF16)],
        compiler_params=_params(("arbitrary", "arbitrary"), FFN_VMEM_MIB),
        name="ffn",
    )(h, g, w_gate, w_up, w_down)


def _rope_tables(pos):
    half = DIFF_HALF // 2
    inv = ROPE_THETA ** (-jnp.arange(half, dtype=F32) / half)
    ang = pos.astype(F32)[:, None] * inv[None, :]
    cos = jnp.concatenate([jnp.cos(ang)] * 4, axis=-1)
    sin = jnp.sin(ang)
    sin_signed = jnp.concatenate([-sin, sin, -sin, sin], axis=-1)
    return cos, sin_signed


def _tile(m, pref):
    return pref if m % pref == 0 else m


def _layer(l, xp, xs, caches, page_table, weights):
    (g_norm_attn, w_in, b_f, g_q_diff, g_k_diff, g_q_fox, g_k_fox, lambda_q1, lambda_k1, lambda_q2, lambda_k2,
     g_sub, w_branch_diff, w_branch_fox, w_o, g_norm_ffn, w_ffn_gate, w_ffn_up, w_ffn_down) = [w[l] for w in weights]
    batch, seq, d = xp.shape
    n_samples, dec_seq, _ = xs.shape
    past_len = page_table.shape[1] * PAGE_SIZE
    lam_init = 0.8 - 0.6 * math.exp(-0.3 * l)

    n_qkv = 6 * WIDTH
    w_all = w_in.astype(BF16)
    w_gates = w_all[:, n_qkv + N_HEADS:]
    w_f = jnp.pad(w_in[:, n_qkv:n_qkv + N_HEADS], ((0, 0), (0, LANES - N_HEADS))).astype(BF16)
    b_f_pad = jnp.pad(b_f, (0, LANES - N_HEADS)).reshape(1, LANES)
    two = lambda g: jnp.concatenate([g, g]).reshape(1, LANES)
    one = lambda g: g.reshape(1, -1)
    lams = tuple(one(v) for v in (lambda_q1, lambda_k1, lambda_q2, lambda_k2))
    w_bd, w_bf, w_out = w_branch_diff.astype(BF16), w_branch_fox.astype(BF16), w_o.astype(BF16)
    w_g, w_u, w_d = w_ffn_gate.astype(BF16), w_ffn_up.astype(BF16), w_ffn_down.astype(BF16)

    def project(x2d, pos, tm, rows_per_table):
        cos, sin_signed = _rope_tables(pos)
        gates = _gates(x2d, one(g_norm_attn), w_gates, tm=_tile(x2d.shape[0], GATES_ROWS), tn=GATES_COLS)
        outs = _proj(x2d, one(g_norm_attn), w_all, w_f, b_f_pad, two(g_q_diff), two(g_k_diff), one(g_q_fox),
                     one(g_k_fox), cos, sin_signed, tm=tm, rows_per_table=rows_per_table)
        return (outs[0], gates) + tuple(outs[1:])

    def finish(x2d, od, of, gates):
        rows = x2d.shape[0]
        h = _merge(od, of, gates, x2d, w_bd, w_bf, w_out, tm=_tile(rows, MERGE_ROWS), tn=MERGE_COLS)
        return _ffn(h, one(g_norm_ffn), w_g, w_u, w_d, tm=_tile(rows, FFN_ROWS), tf=FFN_COLS)

    assert dec_seq == 1
    xp2 = xp.reshape(batch * seq, d)
    xs2 = xs.reshape(n_samples, d)
    q_p, g_p, kd_p, vd_p, kf_p, vf_p, kdb_p, vdb_p, kfb_p, vfb_p, lf_p = project(xp2, jnp.arange(seq), _tile(seq, PROJ_ROWS), seq)
    pos_s = jnp.full((n_samples,), past_len, jnp.int32)
    q_s, g_s, kd_s, vd_s, kf_s, vf_s, _, _, _, _, lf_s = project(xs2, pos_s, n_samples, n_samples)
    q_s = q_s.astype(F32)
    cum, cumt = _cumsum(lf_p, batch, seq)

    n_pages = page_table.shape[1]
    tq = tk = _tile(seq, ATTN_BLOCK)
    nq = seq // tq
    n_host = batch * (nq // 2) * (nq + 1)
    group = math.gcd(DECODE_PAGES, n_pages)
    spp = n_pages // group
    n_half = (n_samples // 2) * spp
    hosted = n_samples % 2 == 0 and n_half <= n_host
    dec_args = (page_table, q_s[:, :WIDTH], q_s[:, WIDTH:], kd_s, vd_s, kf_s, vf_s, lf_s,
                *[c[l] for c in caches])

    def host_step(first):
        return lambda b, r, j: first + jnp.minimum((b * (nq // 2) + r) * (nq + 1) + j, n_half - 1)

    plans = [_DecodePlan(*dec_args, group=group, step_of=host_step(f), first_step=f, n_steps=n_half)
             for f in (0, n_half)] if hosted else [None, None]
    attn = functools.partial(_attention, lams, one(g_sub), q_p, batch=batch, seq=seq, tq=tq, tk=tk,
                             lam_init=lam_init)
    out_d = attn(0, kdb_p, vdb_p, None, plans[0], diff=True, name="diff_attention")
    out_f = attn(1, kfb_p, vfb_p, (cum, cumt), plans[1], diff=False, name="fox_attention")
    if hosted:
        od_s = jnp.concatenate([out_d[1], out_f[1]], axis=0).reshape(n_samples, WIDTH)
        of_s = jnp.concatenate([out_d[2], out_f[2]], axis=0).reshape(n_samples, WIDTH)
    else:
        n_dec = n_samples * spp
        plan = _DecodePlan(*dec_args, group=group, step_of=lambda t: t, first_step=0, n_steps=n_dec)
        od_s, of_s = (o.reshape(n_samples, WIDTH) for o in _decode(plan, lams, one(g_sub), n_steps=n_dec,
                                                                  lam_init=lam_init))
    yp = finish(xp2, out_d[0], out_f[0], g_p)
    ys = finish(xs2, od_s, of_s, g_s)

    heads = lambda a, b, t: a.reshape(b, t, N_HEADS, HEAD_DIM)
    new_p = (heads(kd_p, batch, seq), heads(vd_p, batch, seq), heads(kf_p, batch, seq), heads(vf_p, batch, seq),
             lf_p.reshape(batch, seq, N_HEADS))
    new_s = (heads(kd_s, n_samples, 1), heads(vd_s, n_samples, 1), heads(kf_s, n_samples, 1),
             heads(vf_s, n_samples, 1), lf_s.reshape(n_samples, 1, N_HEADS))
    return yp.reshape(batch, seq, d), ys.reshape(n_samples, dec_seq, d), new_p, new_s


def kernel(x_prompt, x_sample, cache_k_diff, cache_v_diff, cache_k_fox, cache_v_fox, cache_logf_fox, page_table,
           g_norm_attn, w_in, b_f, g_q_diff, g_k_diff, g_q_fox, g_k_fox, lambda_q1, lambda_k1, lambda_q2, lambda_k2,
           g_sub, w_branch_diff, w_branch_fox, w_o, g_norm_ffn, w_ffn_gate, w_ffn_up, w_ffn_down):
    weights = (g_norm_attn, w_in, b_f, g_q_diff, g_k_diff, g_q_fox, g_k_fox, lambda_q1, lambda_k1, lambda_q2,
               lambda_k2, g_sub, w_branch_diff, w_branch_fox, w_o, g_norm_ffn, w_ffn_gate, w_ffn_up, w_ffn_down)
    caches = (cache_k_diff, cache_v_diff, cache_k_fox, cache_v_fox, cache_logf_fox)
    depth = w_in.shape[0]
    xp, xs = x_prompt, x_sample
    new_p, new_s = [], []
    for l in range(depth):
        xp, xs, np_l, ns_l = _layer(l, xp, xs, caches, page_table, weights)
        new_p.append(np_l)
        new_s.append(ns_l)
    stack = lambda lst, i: jnp.stack([t[i] for t in lst], axis=0)
    return (xp, xs) + tuple(stack(new_p, i) for i in range(5)) + tuple(stack(new_s, i) for i in range(5))
```
